```python
import math
import jax, jax.numpy as jnp
from jax import lax
import numpy as np

D_MODEL = 1024
BATCH = 8
SEQ = 2048
DEPTH = 4
DEC_BATCH = 128
DEC_SEQ = 1
PAST_LEN = 16384
PAGE_SIZE = 128

N_EVEN = (DEPTH + 1) // 2
N_ODD = DEPTH // 2
EPS = 1e-6
A_HEADS = 4
A_WIDTH = D_MODEL // 2
A_DK = A_WIDTH // A_HEADS
A_DV = A_DK
A_CONV = 4
A_CHUNK = 64
B_HEAD = 64
B_WIDTH = D_MODEL // 2
B_HEADS = B_WIDTH // B_HEAD
B_LORA_W = 64
B_LORA_A = 64
B_LORA_G = 128
B_SHIFT_WIDTH = 3 * B_WIDTH + B_LORA_W + B_LORA_A + B_LORA_G
B_LN_EPS = 64e-5
EVEN_PROJ = 4 * A_WIDTH + 2 * A_HEADS + B_SHIFT_WIDTH
C_DK = 128
C_DV = 128
C_HEADS = D_MODEL // C_DK
C_WIDTH = C_HEADS * C_DK
C_CHUNK = 32
P_HEADS = 8
P_NKEYS = 128
P_NEXPERTS = P_NKEYS * P_NKEYS
P_DKEY = 256
P_TOPK = 16
P_BLOCK = 64

kernel_name = 'hybrid_mlstm_rwkv7_hgrn2_peer_step'


def _rmsnorm(x, g):
    xf = x.astype(jnp.float32)
    y = xf * lax.rsqrt(jnp.mean(xf * xf, axis=-1, keepdims=True) + EPS)
    return (y * g.astype(jnp.float32)).astype(x.dtype)


def _head_rms(y, g):
    return y * lax.rsqrt(jnp.mean(y * y, axis=-1, keepdims=True) + EPS) * g


def _causal_conv(u, buf, w, b):
    T = u.shape[1]
    full = jnp.concatenate([buf, u], axis=1)
    y = b + full[:, 0:T] * w[0]
    for j in range(1, w.shape[0]):
        y = y + full[:, j:j + T] * w[j]
    return y, full[:, T:]


def _mlstm_chunkwise(q, k, v, logi, logf, C0, n0, m0):
    B, T, H, _ = q.shape
    L = math.gcd(T, A_CHUNK)
    nc = T // L
    causal = jnp.tril(jnp.ones((L, L), dtype=bool))[None, :, :, None]

    def chunks(a):
        return jnp.moveaxis(a.reshape((B, nc, L) + a.shape[2:]), 1, 0)

    def step(carry, inp):
        C, n, m = carry
        qc, kc, vc, ic, fc = inp
        b = jnp.cumsum(fc, axis=1)
        d = b[:, :, None, :] - b[:, None, :, :] + ic[:, None, :, :]
        d = jnp.where(causal, d, -jnp.inf)
        inter = b + m[:, None, :]
        m_t = jnp.maximum(inter, jnp.max(d, axis=2))
        w_intra = jnp.exp(d - m_t[:, :, None, :])
        w_inter = jnp.exp(inter - m_t)
        s = jnp.einsum('bthd,bshd->btsh', qc, kc) * w_intra
        num = jnp.einsum('btsh,bshv->bthv', s, vc) + w_inter[..., None] * jnp.einsum('bthd,bhdv->bthv', qc, C)
        den = jnp.sum(s, axis=2) + w_inter * jnp.einsum('bthd,bhd->bth', qc, n)
        h = num / jnp.maximum(jnp.abs(den), jnp.exp(-m_t))[..., None]
        m_new = m_t[:, -1]
        w_end = jnp.exp(b[:, -1:] - b + ic - m_new[:, None])
        carry_decay = jnp.exp(b[:, -1] + m - m_new)
        C_new = carry_decay[..., None, None] * C + jnp.einsum('bsh,bshd,bshv->bhdv', w_end, kc, vc)
        n_new = carry_decay[..., None] * n + jnp.einsum('bsh,bshd->bhd', w_end, kc)
        return (C_new, n_new, m_new), h

    (C1, n1, m1), h = lax.scan(step, (C0, n0, m0), (chunks(q), chunks(k), chunks(v), chunks(logi), chunks(logf)))
    return jnp.moveaxis(h, 0, 1).reshape(B, T, H, A_DV), C1, n1, m1


def _rwkv7_scan(r, w, k, v, kk, a, S0):
    def step(S, inp):
        rt, wt, kt, vt, kkt, at = inp
        sa = jnp.einsum('bhvk,bhk->bhv', S, -kkt)
        S = S * wt[:, :, None, :] + sa[..., None] * (kkt * at)[:, :, None, :] + vt[..., None] * kt[:, :, None, :]
        return S, jnp.einsum('bhvk,bhk->bhv', S, rt)

    xs = tuple(jnp.moveaxis(t, 1, 0) for t in (r, w, k, v, kk, a))
    S1, y = lax.scan(step, S0, xs)
    return jnp.moveaxis(y, 0, 1), S1


def _hgrn2_chunkwise(q, k, v, logf, S0):
    B, T, H, _ = q.shape
    L = math.gcd(T, C_CHUNK)
    nc = T // L
    causal = jnp.tril(jnp.ones((L, L), dtype=bool))[None, :, :, None, None]

    def chunks(a):
        return jnp.moveaxis(a.reshape((B, nc, L) + a.shape[2:]), 1, 0)

    def step(S, inp):
        qc, kc, vc, gc = inp
        b = jnp.cumsum(gc, axis=1)
        decay = jnp.exp(jnp.where(causal, b[:, :, None] - b[:, None, :], -jnp.inf))
        A = jnp.einsum('bthd,bshd,btshd->btsh', qc, kc, decay)
        o = jnp.einsum('btsh,bshv->bthv', A, vc) + jnp.einsum('bthd,bhdv->bthv', qc * jnp.exp(b), S)
        S_new = jnp.exp(b[:, -1])[..., None] * S + jnp.einsum('bshd,bshv->bhdv', kc * jnp.exp(b[:, -1:] - b), vc)
        return S_new, o

    S1, o = lax.scan(step, S0, (chunks(q), chunks(k), chunks(v), chunks(logf)))
    return jnp.moveaxis(o, 0, 1).reshape(B, T, H, C_DV), S1


def _even_mixer(h, st, prm):
    C0, n0, m0, conv0, S0, shift0 = st
    (w_in, w_out, conv_w, conv_b, gate_b, a_norm_g, mu, w0, w2, a0, a2, g2,
     k_k, k_a, r_k, ln_g, ln_b) = prm
    f32 = jnp.float32
    B, T, _ = h.shape
    p = jnp.einsum('btd,de->bte', h, w_in).astype(f32)
    qk, v, o, ig, fg, pb = jnp.split(
        p, [2 * A_WIDTH, 3 * A_WIDTH, 4 * A_WIDTH, 4 * A_WIDTH + A_HEADS, 4 * A_WIDTH + 2 * A_HEADS], axis=-1)
    qk, conv1 = _causal_conv(qk, conv0.astype(f32), conv_w, conv_b)
    q, k = jnp.split(jax.nn.silu(qk), 2, axis=-1)
    q = q.reshape(B, T, A_HEADS, A_DK) * (A_DK ** -0.5)
    k = k.reshape(B, T, A_HEADS, A_DK)
    v = v.reshape(B, T, A_HEADS, A_DV)
    logi = ig + gate_b[:A_HEADS]
    logf = jax.nn.log_sigmoid(fg + gate_b[A_HEADS:])
    ha, C1, n1, m1 = _mlstm_chunkwise(q, k, v, logi, logf, C0.astype(f32), n0.astype(f32), m0.astype(f32))
    ha = _head_rms(ha, a_norm_g.reshape(A_HEADS, A_DV)).reshape(B, T, A_WIDTH) * jax.nn.sigmoid(o)
    full = jnp.concatenate([shift0.astype(f32), pb], axis=1)
    shift1 = full[:, T:]
    xb = pb + mu * (full[:, :T] - pb)
    r, kb, vb, wl, al, gl = jnp.split(
        xb, [B_WIDTH, 2 * B_WIDTH, 3 * B_WIDTH, 3 * B_WIDTH + B_LORA_W, 3 * B_WIDTH + B_LORA_W + B_LORA_A], axis=-1)
    w = -jax.nn.softplus(-(w0 + jnp.tanh(wl) @ w2)) - 0.5
    decay = jnp.exp(-jnp.exp(w))
    a = jax.nn.sigmoid(a0 + al @ a2)
    g = jax.nn.sigmoid(gl) @ g2
    heads = lambda t: t.reshape(B, T, B_HEADS, B_HEAD)
    kk = heads(kb * k_k)
    kk = kk * lax.rsqrt(jnp.maximum(jnp.sum(kk * kk, axis=-1, keepdims=True), 1e-24))
    kb = kb * (1.0 + (a - 1.0) * k_a)
    r, kb, vb, a, decay = heads(r), heads(kb), heads(vb), heads(a), heads(decay)
    y, S1 = _rwkv7_scan(r, decay, kb, vb, kk, a, S0.astype(f32))
    mean = jnp.mean(y, axis=-1, keepdims=True)
    var = jnp.mean(jnp.square(y - mean), axis=-1, keepdims=True)
    y = (y - mean) * lax.rsqrt(var + B_LN_EPS) * ln_g.reshape(B_HEADS, B_HEAD) + ln_b.reshape(B_HEADS, B_HEAD)
    y = y + jnp.sum(r * kb * r_k.reshape(B_HEADS, B_HEAD), axis=-1, keepdims=True) * vb
    y = y.reshape(B, T, B_WIDTH) * g
    out = jnp.concatenate([ha, y], axis=-1) @ w_out
    return out, (C1, n1, m1, conv1, S1, shift1)


def _odd_mixer(h, S0, lb, prm):
    w_in, w_out, norm_g = prm
    f32 = jnp.float32
    B, T, _ = h.shape
    p = jnp.einsum('btd,de->bte', h, w_in).astype(f32)
    q, f, i, g = jnp.split(p, 4, axis=-1)
    q = jax.nn.silu(q)
    f = lb + (1.0 - lb) * jax.nn.sigmoid(f)
    heads = lambda t: t.reshape(B, T, C_HEADS, -1)
    o, S1 = _hgrn2_chunkwise(heads(q), heads(1.0 - f), heads(i), heads(jnp.log(f)), S0.astype(f32))
    o = _head_rms(o, norm_g) * heads(jax.nn.silu(g))
    return o.reshape(B, T, C_WIDTH) @ w_out, S1


def _peer(h, w_q, keys, u_tab, v_tab):
    f32 = jnp.float32
    B, T, D = h.shape
    n = B * T
    nb = -(-n // P_BLOCK)
    xt = jnp.pad(h.reshape(n, D), ((0, nb * P_BLOCK - n), (0, 0))).reshape(nb, P_BLOCK, D)

    def block(xb):
        q = (xb @ w_q).astype(f32).reshape(P_BLOCK, P_HEADS, 2, P_DKEY // 2)
        s = jnp.einsum('thcd,hckd->thck', q, keys.astype(f32))
        s_top, i_top = lax.top_k(s, P_TOPK)
        cand = s_top[:, :, 0, :, None] + s_top[:, :, 1, None, :]
        cand_idx = i_top[:, :, 0, :, None] * P_NKEYS + i_top[:, :, 1, None, :]
        best, pos = lax.top_k(cand.reshape(P_BLOCK, P_HEADS, P_TOPK * P_TOPK), P_TOPK)
        experts = jnp.take_along_axis(cand_idx.reshape(P_BLOCK, P_HEADS, P_TOPK * P_TOPK), pos, axis=-1)
        gate = jax.nn.softmax(best, axis=-1)
        u = jnp.take(u_tab, experts, axis=0)
        act = jax.nn.gelu(jnp.einsum('thkd,td->thk', u, xb).astype(f32))
        vv = jnp.take(v_tab, experts, axis=0).astype(f32)
        return jnp.einsum('thk,thkd->td', gate * act, vv)

    out = lax.map(block, xt)
    return out.reshape(nb * P_BLOCK, D)[:n].reshape(B, T, D)


def _trunk(x, c, st, prm):
    mC, mn, mm, mconv, rS, rshift, hS = st
    (norm_mix_g, norm_ffn_g, norm_final_g, ada_w, ada_b, even_w_in, even_w_out, mlstm_conv_w, mlstm_conv_b,
     mlstm_gate_b, mlstm_norm_g, rwkv_mu, rwkv_w0, rwkv_w2, rwkv_a0, rwkv_a2, rwkv_g2, rwkv_k_k, rwkv_k_a,
     rwkv_r_k, rwkv_ln_g, rwkv_ln_b, odd_w_in, odd_w_out, hgrn_lb_logits, hgrn_norm_g,
     peer_w_q, peer_keys, peer_u, peer_v) = prm
    f32 = jnp.float32
    sc = jax.nn.silu(c.astype(f32))
    lb_soft = jax.nn.softmax(hgrn_lb_logits.astype(f32), axis=0)
    lbs = jnp.cumsum(lb_soft, axis=0) - lb_soft[0]
    new_even = []
    new_odd = []
    for l in range(DEPTH):
        j = l // 2
        mod = (sc @ ada_w[l] + ada_b[l]).astype(x.dtype)[:, None, :]
        sh1, sc1, g1, sh2, sc2, g2 = jnp.split(mod, 6, axis=-1)
        h = _rmsnorm(x, norm_mix_g[l]) * (1 + sc1) + sh1
        if l % 2 == 0:
            out, s_new = _even_mixer(
                h, (mC[j], mn[j], mm[j], mconv[j], rS[j], rshift[j]),
                (even_w_in[j], even_w_out[j], mlstm_conv_w[j], mlstm_conv_b[j], mlstm_gate_b[j], mlstm_norm_g[j],
                 rwkv_mu[j], rwkv_w0[j], rwkv_w2[j], rwkv_a0[j], rwkv_a2[j], rwkv_g2[j], rwkv_k_k[j], rwkv_k_a[j],
                 rwkv_r_k[j], rwkv_ln_g[j], rwkv_ln_b[j]))
            new_even.append(s_new)
        else:
            out, s_new = _odd_mixer(h, hS[j], lbs[j], (odd_w_in[j], odd_w_out[j], hgrn_norm_g[j]))
            new_odd.append(s_new)
        x = x + g1 * out.astype(x.dtype)
        h = _rmsnorm(x, norm_ffn_g[l]) * (1 + sc2) + sh2
        x = x + g2 * _peer(h, peer_w_q[l], peer_keys[l], peer_u[l], peer_v[l]).astype(x.dtype)
    y = _rmsnorm(x, norm_final_g)
    ev = [jnp.stack([s[i] for s in new_even]).astype(st[i].dtype) for i in range(6)]
    hS_new = jnp.stack(new_odd).astype(hS.dtype)
    return (y, ev[0], ev[1], ev[2], ev[3], ev[4], ev[5], hS_new)


def setup_inputs(seed: int = 0) -> dict:
    key = jax.random.key(seed)
    keys = iter(jax.random.split(key, 48))

    def nrm(shape, scale):
        return jax.random.normal(next(keys), shape, jnp.float32) * scale

    def unif(shape, lo, hi):
        return jax.random.uniform(next(keys), shape, jnp.float32, lo, hi)

    D = D_MODEL
    return {
        'x_prompt': nrm((BATCH, SEQ, D), 1.0),
        'x_sample': nrm((DEC_BATCH, DEC_SEQ, D), 1.0),
        'c_prompt': nrm((BATCH, D), 1.0),
        'c_sample': nrm((DEC_BATCH, D), 1.0),
        'state_mlstm_C': nrm((N_EVEN, DEC_BATCH, A_HEADS, A_DK, A_DV), 0.1),
        'state_mlstm_n': nrm((N_EVEN, DEC_BATCH, A_HEADS, A_DK), 0.5),
        'state_mlstm_m': nrm((N_EVEN, DEC_BATCH, A_HEADS), 0.5),
        'state_mlstm_conv': nrm((N_EVEN, DEC_BATCH, A_CONV - 1, 2 * A_WIDTH), 1.0),
        'state_rwkv_S': nrm((N_EVEN, DEC_BATCH, B_HEADS, B_HEAD, B_HEAD), 0.1),
        'state_rwkv_shift': nrm((N_EVEN, DEC_BATCH, 1, B_SHIFT_WIDTH), 1.0),
        'state_hgrn_S': nrm((N_ODD, DEC_BATCH, C_HEADS, C_DK, C_DV), 0.5),
        'norm_mix_g': 1.0 + nrm((DEPTH, D), 0.05),
        'norm_ffn_g': 1.0 + nrm((DEPTH, D), 0.05),
        'norm_final_g': 1.0 + nrm((D,), 0.05),
        'ada_w': nrm((DEPTH, D, 6 * D), 0.3 * D ** -0.5),
        'ada_b': nrm((DEPTH, 6 * D), 0.1),
        'even_w_in': nrm((N_EVEN, D, EVEN_PROJ), D ** -0.5),
        'even_w_out': nrm((N_EVEN, A_WIDTH + B_WIDTH, D), (A_WIDTH + B_WIDTH) ** -0.5),
        'mlstm_conv_w': nrm((N_EVEN, A_CONV, 2 * A_WIDTH), 0.5),
        'mlstm_conv_b': nrm((N_EVEN, 2 * A_WIDTH), 0.02),
        'mlstm_gate_b': jnp.concatenate([nrm((N_EVEN, A_HEADS), 0.1), unif((N_EVEN, A_HEADS), 3.0, 6.0)], axis=-1),
        'mlstm_norm_g': 1.0 + nrm((N_EVEN, A_WIDTH), 0.05),
        'rwkv_mu': unif((N_EVEN, B_SHIFT_WIDTH), 0.0, 1.0),
        'rwkv_w0': nrm((N_EVEN, B_WIDTH), 0.5),
        'rwkv_w2': nrm((N_EVEN, B_LORA_W, B_WIDTH), 0.1),
        'rwkv_a0': nrm((N_EVEN, B_WIDTH), 0.5),
        'rwkv_a2': nrm((N_EVEN, B_LORA_A, B_WIDTH), 0.1),
        'rwkv_g2': nrm((N_EVEN, B_LORA_G, B_WIDTH), B_LORA_G ** -0.5),
        'rwkv_k_k': 0.85 + nrm((N_EVEN, B_WIDTH), 0.05),
        'rwkv_k_a': 1.0 + nrm((N_EVEN, B_WIDTH), 0.05),
        'rwkv_r_k': nrm((N_EVEN, B_WIDTH), 0.1),
        'rwkv_ln_g': 1.0 + nrm((N_EVEN, B_WIDTH), 0.05),
        'rwkv_ln_b': nrm((N_EVEN, B_WIDTH), 0.02),
        'odd_w_in': nrm((N_ODD, D, 4 * C_WIDTH), D ** -0.5),
        'odd_w_out': nrm((N_ODD, C_WIDTH, D), C_WIDTH ** -0.5),
        'hgrn_lb_logits': 1.0 + nrm((N_ODD, C_WIDTH), 0.1),
        'hgrn_norm_g': 1.0 + nrm((N_ODD, C_DV), 0.05),
        'peer_w_q': nrm((DEPTH, D, P_HEADS * P_DKEY), D ** -0.5),
        'peer_keys': nrm((DEPTH, P_HEADS, 2, P_NKEYS, P_DKEY // 2), (P_DKEY // 2) ** -0.5),
        'peer_u': nrm((DEPTH, P_NEXPERTS, D), D ** -0.5),
        'peer_v': nrm((DEPTH, P_NEXPERTS, D), 1.0),
    }


def reference(x_prompt, x_sample, c_prompt, c_sample, state_mlstm_C, state_mlstm_n, state_mlstm_m,
              state_mlstm_conv, state_rwkv_S, state_rwkv_shift, state_hgrn_S, norm_mix_g, norm_ffn_g,
              norm_final_g, ada_w, ada_b, even_w_in, even_w_out, mlstm_conv_w, mlstm_conv_b, mlstm_gate_b,
              mlstm_norm_g, rwkv_mu, rwkv_w0, rwkv_w2, rwkv_a0, rwkv_a2, rwkv_g2, rwkv_k_k, rwkv_k_a, rwkv_r_k,
              rwkv_ln_g, rwkv_ln_b, odd_w_in, odd_w_out, hgrn_lb_logits, hgrn_norm_g, peer_w_q, peer_keys,
              peer_u, peer_v):
    prm = (norm_mix_g, norm_ffn_g, norm_final_g, ada_w, ada_b, even_w_in, even_w_out, mlstm_conv_w, mlstm_conv_b,
           mlstm_gate_b, mlstm_norm_g, rwkv_mu, rwkv_w0, rwkv_w2, rwkv_a0, rwkv_a2, rwkv_g2, rwkv_k_k, rwkv_k_a,
           rwkv_r_k, rwkv_ln_g, rwkv_ln_b, odd_w_in, odd_w_out, hgrn_lb_logits, hgrn_norm_g,
           peer_w_q, peer_keys, peer_u, peer_v)
    bp = x_prompt.shape[0]
    dt = x_prompt.dtype
    zero_st = (jnp.zeros((N_EVEN, bp, A_HEADS, A_DK, A_DV), dt),
               jnp.zeros((N_EVEN, bp, A_HEADS, A_DK), dt),
               jnp.zeros((N_EVEN, bp, A_HEADS), dt),
               jnp.zeros((N_EVEN, bp, A_CONV - 1, 2 * A_WIDTH), dt),
               jnp.zeros((N_EVEN, bp, B_HEADS, B_HEAD, B_HEAD), dt),
               jnp.zeros((N_EVEN, bp, 1, B_SHIFT_WIDTH), dt),
               jnp.zeros((N_ODD, bp, C_HEADS, C_DK, C_DV), dt))
    (y_prompt, p_mC, p_mn, p_mm, p_mconv, p_rS, p_rshift, p_hS) = _trunk(x_prompt, c_prompt, zero_st, prm)
    sample_st = (state_mlstm_C, state_mlstm_n, state_mlstm_m, state_mlstm_conv, state_rwkv_S, state_rwkv_shift,
                 state_hgrn_S)
    (y_sample, s_mC, s_mn, s_mm, s_mconv, s_rS, s_rshift, s_hS) = _trunk(x_sample, c_sample, sample_st, prm)
    return (y_prompt, y_sample, p_mC, p_mn, p_mm, p_mconv, p_rS, p_rshift, p_hS,
            s_mC, s_mn, s_mm, s_mconv, s_rS, s_rshift, s_hS)
```

```python
import functools
import math

import jax
import jax.numpy as jnp
from jax import lax
from jax.experimental import pallas as pl
from jax.experimental.pallas import tpu as pltpu

F32 = jnp.float32
BF16 = jnp.bfloat16

EPS = 1e-6
A_HEADS = 4
A_DK = 128
A_CONV = 4
B_HEADS = 8
B_HEAD = 64
B_LORA_W = 64
B_LORA_A = 64
B_LORA_G = 128
B_LN_EPS = 64e-5
C_HEADS = 8
C_DK = 128
P_HEADS = 8
P_NKEYS = 128
P_TOPK = 16

LANES = 128
SUBLANES = 8
VMEM_LIMIT = 56 * 1024 * 1024
NEG_INF = float("-inf")


def _bdot(a, b):
    return jnp.dot(a.astype(BF16), b.astype(BF16), preferred_element_type=F32)


def _dot_nt(a, b):
    return lax.dot_general(a.astype(BF16), b.astype(BF16), (((1,), (1,)), ((), ())), preferred_element_type=F32)


def _dot_tn(a, b):
    return lax.dot_general(a.astype(BF16), b.astype(BF16), (((0,), (0,)), ((), ())), preferred_element_type=F32)


def _split(x, terms):
    parts = []
    rem = x
    for _ in range(terms):
        p = rem.astype(BF16)
        parts.append(p)
        rem = rem - p.astype(F32)
    return parts


def _dot_split(a, b_bf, terms):
    out = None
    for p in _split(a, terms):
        d = jnp.dot(p, b_bf, preferred_element_type=F32)
        out = d if out is None else out + d
    return out


def _dot_split_left(a_bf, b, terms):
    out = None
    for p in _split(b, terms):
        d = jnp.dot(a_bf, p, preferred_element_type=F32)
        out = d if out is None else out + d
    return out


def _sigmoid(x):
    return 1.0 / (1.0 + jnp.exp(-x))


def _silu(x):
    return x * _sigmoid(x)


def _softplus(x):
    return jnp.maximum(x, 0.0) + jnp.log(1.0 + jnp.exp(-jnp.abs(x)))


def _gelu_tanh(x):
    return 0.5 * x * (1.0 + jnp.tanh(math.sqrt(2.0 / math.pi) * (x + 0.044715 * (x * x * x))))


def _rmsnorm(x, g):
    return x * lax.rsqrt(jnp.mean(x * x, axis=-1, keepdims=True) + EPS) * g


def _group_matrix(n, group, value):
    r = lax.broadcasted_iota(jnp.int32, (n, n), 0) // group
    c = lax.broadcasted_iota(jnp.int32, (n, n), 1) // group
    return jnp.where(r == c, value, 0.0).astype(BF16)


def _full_spec(arr):
    nd = arr.ndim
    return pl.BlockSpec(arr.shape, lambda *_: (0,) * nd)


def _params(n_grid):
    return pltpu.CompilerParams(dimension_semantics=("arbitrary",) * n_grid, vmem_limit_bytes=VMEM_LIMIT)


def _tok_spec(tt, width):
    return pl.BlockSpec((1, tt, width), lambda g, i: (g, i, 0))


def _mod_spec(mod3, tt):
    width = mod3.shape[2]
    if mod3.shape[1] == 1:
        return pl.BlockSpec((1, 1, width), lambda g, i: (g, 0, 0))
    return pl.BlockSpec((1, tt, width), lambda g, i: (g, i, 0))


def _ada_body(c_ref, w_ref, b_ref, o_ref):
    c = c_ref[...]
    o_ref[0] = _bdot(_silu(c), w_ref[0]) + b_ref[0]


def _ada(c_all, ada_w, ada_b):
    depth, d, n6 = ada_w.shape
    bc = c_all.shape[0]
    tn = n6 // 4
    return pl.pallas_call(
        _ada_body,
        grid=(depth, n6 // tn),
        in_specs=[pl.BlockSpec((bc, d), lambda l, j: (0, 0)),
                  pl.BlockSpec((1, d, tn), lambda l, j: (l, 0, j)),
                  pl.BlockSpec((1, 1, tn), lambda l, j: (l, 0, j))],
        out_specs=pl.BlockSpec((1, bc, tn), lambda l, j: (l, 0, j)),
        out_shape=jax.ShapeDtypeStruct((depth, bc, n6), F32),
        compiler_params=_params(2),
        name="ada_mod",
    )(c_all, ada_w, ada_b.reshape(depth, 1, n6))


def _even_pre_body(seq, tt, d, aw, bw, *refs):
    n_in = 15 if seq else 19
    ins, outs = refs[:n_in], refs[n_in:]
    (x_ref, mod_ref, ng_ref, w_ref, cw_ref, cb_ref, gb_ref, mu_ref, w0_ref, a0_ref, wla_ref, g2_ref,
     kk_ref, ka_ref, gs_ref) = ins[:15]
    (q_ref, k_ref, v_ref, so_ref, g8_ref, g8t_ref, r_ref, dec_ref, kt_ref, vb_ref, al_ref, be_ref, gg_ref,
     ctail_ref, stail_ref) = outs[:15]
    a2 = 2 * aw
    sw = 3 * bw + B_LORA_W + B_LORA_A + B_LORA_G

    x = x_ref[0]
    mod = mod_ref[0]
    h = _rmsnorm(x, ng_ref[...]) * (1.0 + mod[:, d:2 * d]) + mod[:, 0:d]
    p = jnp.dot(h.astype(BF16), w_ref[...], preferred_element_type=F32)
    u = p[:, 0:a2]
    v = p[:, a2:a2 + aw]
    o = p[:, a2 + aw:a2 + 2 * aw]
    pb = p[:, 4 * aw:4 * aw + sw]
    gt = p[:, 4 * aw + sw:4 * aw + sw + LANES]

    if seq:
        ubuf, pbuf = outs[15], outs[16]

        @pl.when(pl.program_id(1) == 0)
        def _():
            ubuf[0:SUBLANES, :] = jnp.zeros((SUBLANES, a2), F32)
            pbuf[0:SUBLANES, :] = jnp.zeros((SUBLANES, sw), F32)

        ubuf[SUBLANES:SUBLANES + tt, :] = u
        pbuf[SUBLANES:SUBLANES + tt, :] = pb
        u1 = ubuf[SUBLANES - 1:SUBLANES - 1 + tt, :]
        u2 = ubuf[SUBLANES - 2:SUBLANES - 2 + tt, :]
        u3 = ubuf[SUBLANES - 3:SUBLANES - 3 + tt, :]
        pprev = pbuf[SUBLANES - 1:SUBLANES - 1 + tt, :]
        ubuf[0:SUBLANES, :] = u[tt - SUBLANES:tt, :]
        pbuf[0:SUBLANES, :] = pb[tt - SUBLANES:tt, :]
        ctail_ref[0] = u[tt - SUBLANES:tt, :]
        stail_ref[0] = pb[tt - SUBLANES:tt, :]
    else:
        u3, u2, u1, pprev = ins[15][0], ins[16][0], ins[17][0], ins[18][0]
        ctail_ref[0] = u
        stail_ref[0] = pb

    cw = cw_ref[...]
    y = cb_ref[...] + u3 * cw[0:1, :] + u2 * cw[1:2, :] + u1 * cw[2:3, :] + u * cw[3:4, :]
    qk = _silu(y)
    q_ref[0] = qk[:, 0:aw] * (A_DK ** -0.5)
    k_ref[0] = qk[:, aw:a2]
    v_ref[0] = v
    so_ref[0] = _sigmoid(o)
    g = gt + gb_ref[...]
    lane = lax.broadcasted_iota(jnp.int32, g.shape, 1)
    gates = jnp.where(lane < A_HEADS, g, -_softplus(-g))
    g8_ref[0] = gates[:, 0:2 * A_HEADS]
    g8t_ref[0] = gates.T[0:2 * A_HEADS, :]

    xb = pb + mu_ref[...] * (pprev - pb)
    r = xb[:, 0:bw]
    kb = xb[:, bw:2 * bw]
    vb = xb[:, 2 * bw:3 * bw]
    la = xb[:, 3 * bw:3 * bw + LANES]
    gl = xb[:, 3 * bw + LANES:3 * bw + 2 * LANES]
    lane2 = lax.broadcasted_iota(jnp.int32, la.shape, 1)
    la_act = jnp.where(lane2 < B_LORA_W, jnp.tanh(la), la)
    lw = jnp.dot(la_act.astype(BF16), wla_ref[...], preferred_element_type=F32)
    wlog = -_softplus(-(w0_ref[...] + lw[:, 0:bw])) - 0.5
    decay = jnp.exp(-jnp.exp(wlog))
    a = _sigmoid(a0_ref[...] + lw[:, bw:2 * bw])
    gg = jnp.dot(_sigmoid(gl).astype(BF16), g2_ref[...], preferred_element_type=F32)
    kk = kb * kk_ref[...]
    ss = _dot_split(kk * kk, gs_ref[...], 2)
    kkn = kk * lax.rsqrt(jnp.maximum(ss, 1e-24))
    r_ref[0] = r
    dec_ref[0] = decay
    kt_ref[0] = kb * (1.0 + (a - 1.0) * ka_ref[...])
    vb_ref[0] = vb
    al_ref[0] = -kkn
    be_ref[0] = kkn * a
    gg_ref[0] = gg


def _even_pre(x3, mod3, ng, wcat, cw, cb, gb, mu, w0, a0, wla, g2, k_k, k_a, prev, tt):
    g_, ttot, d = x3.shape
    aw = d // 2
    bw = d // 2
    a2 = 2 * aw
    sw = 3 * bw + B_LORA_W + B_LORA_A + B_LORA_G
    seq = prev is None
    gsum = _group_matrix(bw, B_HEAD, 1.0)
    consts = [ng, wcat, cw, cb, gb, mu, w0, a0, wla, g2, k_k, k_a, gsum]
    ins = [x3, mod3] + consts
    in_specs = [_tok_spec(tt, d), _mod_spec(mod3, tt)] + [_full_spec(c) for c in consts]
    if not seq:
        ins += list(prev)
        in_specs += [_tok_spec(tt, a2)] * 3 + [_tok_spec(tt, sw)]
    nt = ttot // tt
    tail_rows = SUBLANES if seq else tt
    tail_tot = g_ * SUBLANES if seq else ttot

    def tail_spec(width):
        if seq:
            return pl.BlockSpec((1, SUBLANES, width), lambda g, i: (g, 0, 0))
        return _tok_spec(tt, width)

    def tail_shape(width):
        if seq:
            return jax.ShapeDtypeStruct((g_, SUBLANES, width), F32)
        return jax.ShapeDtypeStruct((g_, ttot, width), F32)

    tok = lambda w: jax.ShapeDtypeStruct((g_, ttot, w), F32)
    out_shape = [tok(aw), tok(aw), tok(aw), tok(aw), tok(2 * A_HEADS),
                 jax.ShapeDtypeStruct((g_, 2 * A_HEADS, ttot), F32),
                 tok(bw), tok(bw), tok(bw), tok(bw), tok(bw), tok(bw), tok(bw),
                 tail_shape(a2), tail_shape(sw)]
    out_specs = [_tok_spec(tt, aw)] * 4 + [_tok_spec(tt, 2 * A_HEADS),
                                           pl.BlockSpec((1, 2 * A_HEADS, tt), lambda g, i: (g, 0, i))]
    out_specs += [_tok_spec(tt, bw)] * 7 + [tail_spec(a2), tail_spec(sw)]
    scratch = [pltpu.VMEM((tt + SUBLANES, a2), F32), pltpu.VMEM((tt + SUBLANES, sw), F32)] if seq else []
    del tail_rows, tail_tot
    return pl.pallas_call(
        functools.partial(_even_pre_body, seq, tt, d, aw, bw),
        grid=(g_, nt),
        in_specs=in_specs,
        out_specs=out_specs,
        out_shape=out_shape,
        scratch_shapes=scratch,
        compiler_params=_params(2),
        name="even_pre_seq" if seq else "even_pre_step",
    )(*ins)


def _mlstm_chunk_body(L, q_ref, k_ref, v_ref, g8_ref, g8t_ref, c0_ref, n0_ref, m0_ref,
                      h_ref, c1_ref, n1_ref, m1_ref, C_s, n_s, m_s):
    c = pl.program_id(1)

    @pl.when(c == 0)
    def _():
        C_s[...] = c0_ref[0]
        n_s[...] = n0_ref[0]
        m_s[...] = m0_ref[0]

    row = lax.broadcasted_iota(jnp.int32, (L, L), 0)
    col = lax.broadcasted_iota(jnp.int32, (L, L), 1)
    causal = col <= row
    tril = jnp.where(causal, 1.0, 0.0).astype(BF16)
    triu = jnp.where(row <= col, 1.0, 0.0).astype(BF16)
    g8 = g8_ref[0]
    g8t = g8t_ref[0]
    bcols = _dot_split_left(tril, g8, 3)
    brows = _dot_split(g8t, triu, 3)
    for hd in range(A_HEADS):
        sl = slice(hd * A_DK, (hd + 1) * A_DK)
        q = q_ref[0, :, sl]
        k = k_ref[0, :, sl]
        v = v_ref[0, :, sl]
        i_row = g8t[hd:hd + 1, :]
        i_col = g8[:, hd:hd + 1]
        b_row = brows[A_HEADS + hd:A_HEADS + hd + 1, :]
        b_col = bcols[:, A_HEADS + hd:A_HEADS + hd + 1]
        m_prev = m_s[hd:hd + 1, 0:1]
        C = C_s[hd]
        n = n_s[hd:hd + 1, :]
        dmat = jnp.where(causal, b_col - b_row + i_row, NEG_INF)
        inter = b_col + m_prev
        m_t = jnp.maximum(inter, jnp.max(dmat, axis=1, keepdims=True))
        w_intra = jnp.exp(dmat - m_t)
        w_inter = jnp.exp(inter - m_t)
        s = _dot_nt(q, k) * w_intra
        num = _bdot(s, v) + w_inter * _bdot(q, C)
        den = jnp.sum(s, axis=1, keepdims=True) + w_inter * jnp.sum(q * n, axis=1, keepdims=True)
        h_ref[0, :, sl] = num / jnp.maximum(jnp.abs(den), jnp.exp(-m_t))
        m_new = m_t[L - 1:L, :]
        b_last = b_col[L - 1:L, :]
        kw = k * jnp.exp(b_last - b_col + i_col - m_new)
        carry = jnp.exp(b_last + m_prev - m_new)
        C_s[hd] = carry * C + _dot_tn(kw, v)
        n_s[hd:hd + 1, :] = carry * n + jnp.sum(kw, axis=0, keepdims=True)
        m_s[hd:hd + 1, :] = jnp.broadcast_to(m_new, (1, LANES))

    @pl.when(c == pl.num_programs(1) - 1)
    def _():
        c1_ref[0] = C_s[...]
        n1_ref[0] = n_s[...]
        m1_ref[0] = m_s[...]


def _mlstm_chunk(q, k, v, g8, g8t, C0, n0p, m0p, L):
    b, t, aw = q.shape
    st = lambda *s: pl.BlockSpec((1,) + s, lambda bb, c: (bb,) + (0,) * len(s))
    return pl.pallas_call(
        functools.partial(_mlstm_chunk_body, L),
        grid=(b, t // L),
        in_specs=[_tok_spec(L, aw)] * 3 + [_tok_spec(L, 2 * A_HEADS),
                                           pl.BlockSpec((1, 2 * A_HEADS, L), lambda bb, c: (bb, 0, c)),
                                           st(A_HEADS, A_DK, A_DK), st(SUBLANES, LANES), st(SUBLANES, LANES)],
        out_specs=[_tok_spec(L, aw), st(A_HEADS, A_DK, A_DK), st(SUBLANES, LANES), st(SUBLANES, LANES)],
        out_shape=[jax.ShapeDtypeStruct((b, t, aw), F32), jax.ShapeDtypeStruct(C0.shape, F32),
                   jax.ShapeDtypeStruct(n0p.shape, F32), jax.ShapeDtypeStruct(m0p.shape, F32)],
        scratch_shapes=[pltpu.VMEM((A_HEADS, A_DK, A_DK), F32), pltpu.VMEM((SUBLANES, LANES), F32),
                        pltpu.VMEM((SUBLANES, LANES), F32)],
        compiler_params=_params(2),
        name="mlstm_chunk",
    )(q, k, v, g8, g8t, C0, n0p, m0p)


def _row8(x):
    r = lax.broadcasted_iota(jnp.int32, (SUBLANES, x.shape[1]), 0)
    return jnp.where(r == 0, jnp.broadcast_to(x, (SUBLANES, x.shape[1])), 0.0)


def _mlstm_step_body(bb, q_ref, k_ref, v_ref, g8_ref, c0_ref, n0_ref, m0_ref, h_ref, c1_ref, n1_ref, m1_ref):
    g8 = g8_ref[...]
    m0 = m0_ref[...]
    for hd in range(A_HEADS):
        sl = slice(hd * A_DK, (hd + 1) * A_DK)
        li = g8[:, hd:hd + 1]
        lf = g8[:, A_HEADS + hd:A_HEADS + hd + 1]
        mp = m0[:, hd:hd + 1]
        m1 = jnp.maximum(lf + mp, li)
        wi = jnp.exp(li - m1)
        wf = jnp.exp(lf + mp - m1)
        floor = jnp.exp(-m1)
        for b in range(bb):
            q = q_ref[b:b + 1, sl]
            k = k_ref[b:b + 1, sl]
            v = v_ref[b:b + 1, sl]
            wib = wi[b:b + 1, :]
            wfb = wf[b:b + 1, :]
            C1 = wfb * c0_ref[b, hd] + wib * _dot_tn(_row8(k), _row8(v))
            n1 = wfb * n0_ref[b, hd:hd + 1, :] + wib * k
            num = _bdot(_row8(q), C1)[0:1, :]
            den = jnp.sum(q * n1, axis=1, keepdims=True)
            h_ref[b:b + 1, sl] = num / jnp.maximum(jnp.abs(den), floor[b:b + 1, :])
            c1_ref[b, hd] = C1
            n1_ref[b, hd:hd + 1, :] = n1
        m1_ref[:, hd:hd + 1] = m1


def _mlstm_step(q, k, v, g8, C0, n0, m0, bb):
    b, aw = q.shape
    row = lambda w: pl.BlockSpec((bb, w), lambda i: (i, 0))
    return pl.pallas_call(
        functools.partial(_mlstm_step_body, bb),
        grid=(b // bb,),
        in_specs=[row(aw)] * 3 + [row(2 * A_HEADS),
                                  pl.BlockSpec((bb, A_HEADS, A_DK, A_DK), lambda i: (i, 0, 0, 0)),
                                  pl.BlockSpec((bb, A_HEADS, A_DK), lambda i: (i, 0, 0)), row(A_HEADS)],
        out_specs=[row(aw), pl.BlockSpec((bb, A_HEADS, A_DK, A_DK), lambda i: (i, 0, 0, 0)),
                   pl.BlockSpec((bb, A_HEADS, A_DK), lambda i: (i, 0, 0)), row(A_HEADS)],
        out_shape=[jax.ShapeDtypeStruct((b, aw), F32), jax.ShapeDtypeStruct(C0.shape, F32),
                   jax.ShapeDtypeStruct(n0.shape, F32), jax.ShapeDtypeStruct(m0.shape, F32)],
        compiler_params=_params(1),
        name="mlstm_step",
    )(q, k, v, g8, C0, n0, m0)


def _rwkv_body(bb, tb, al_ref, w_ref, be_ref, kt_ref, r_ref, v_ref, s0_ref, y_ref, s1_ref, S_s):
    @pl.when(pl.program_id(1) == 0)
    def _():
        S_s[...] = s0_ref[...]

    npair = B_HEADS // 2
    row = lax.broadcasted_iota(jnp.int32, (B_HEAD, LANES), 0)
    lane = lax.broadcasted_iota(jnp.int32, (B_HEAD, LANES), 1)
    eye_pair = row == (lane % B_HEAD)
    wred = _group_matrix(LANES, B_HEAD, 1.0)
    r8 = lax.broadcasted_iota(jnp.int32, (SUBLANES, LANES), 0)
    l8 = lax.broadcasted_iota(jnp.int32, (SUBLANES, LANES), 1)
    rmask = ((r8 == 0) & (l8 < B_HEAD)) | ((r8 == 1) & (l8 >= B_HEAD))

    nrow = min(tb, SUBLANES)

    def group(t8, carry):
        base = t8 * nrow if isinstance(t8, int) else pl.multiple_of(t8 * nrow, nrow)
        for b in range(bb):
            for p in range(npair):
                sl = pl.ds(p * LANES, LANES)
                rows = pl.ds(base, nrow)
                al, w, be = al_ref[b, rows, sl], w_ref[b, rows, sl], be_ref[b, rows, sl]
                kt, r, v = kt_ref[b, rows, sl], r_ref[b, rows, sl], v_ref[b, rows, sl]
                S = S_s[b, p]
                for i in range(nrow):
                    ri = slice(i, i + 1)
                    sa = _dot_split(S * al[ri], wred, 2)
                    vcol = _dot_split(jnp.where(eye_pair, v[ri], 0.0), wred, 2)
                    S = S * w[ri] + sa * be[ri] + vcol * kt[ri]
                    y2 = _dot_nt(jnp.where(rmask, r[ri], 0.0), S)
                    y_ref[b, base + i, 2 * p:2 * p + 2, :] = y2[0:2, :]
                S_s[b, p] = S
        return carry

    if tb == nrow:
        group(0, 0)
    else:
        lax.fori_loop(0, tb // nrow, group, 0)

    @pl.when(pl.program_id(1) == pl.num_programs(1) - 1)
    def _():
        s1_ref[...] = S_s[...]


def _rwkv_scan(al, w, be, kt, r, v, S0p, bb, tb):
    b, t, bw = al.shape
    npair = B_HEADS // 2
    tok = pl.BlockSpec((bb, tb, bw), lambda i, j: (i, j, 0))
    st = pl.BlockSpec((bb, npair, B_HEAD, LANES), lambda i, j: (i, 0, 0, 0))
    return pl.pallas_call(
        functools.partial(_rwkv_body, bb, tb),
        grid=(b // bb, t // tb),
        in_specs=[tok] * 6 + [st],
        out_specs=[pl.BlockSpec((bb, tb, B_HEADS, B_HEAD), lambda i, j: (i, j, 0, 0)), st],
        out_shape=[jax.ShapeDtypeStruct((b, t, B_HEADS, B_HEAD), F32), jax.ShapeDtypeStruct(S0p.shape, F32)],
        scratch_shapes=[pltpu.VMEM((bb, npair, B_HEAD, LANES), F32)],
        compiler_params=_params(2),
        name="rwkv_scan",
    )(al, w, be, kt, r, v, S0p)


def _even_post_body(d, aw, bw, x_ref, mod_ref, ha_ref, so_ref, y_ref, r_ref, kt_ref, vb_ref, gg_ref,
                    ang_ref, lng_ref, lnb_ref, rk_ref, wo_ref, o_ref):
    ga = _group_matrix(aw, A_DK, 1.0 / A_DK)
    gb_mean = _group_matrix(bw, B_HEAD, 1.0 / B_HEAD)
    gb_sum = _group_matrix(bw, B_HEAD, 1.0)
    ha = ha_ref[0]
    ha = ha * lax.rsqrt(_dot_split(ha * ha, ga, 2) + EPS) * ang_ref[...] * so_ref[0]
    y = y_ref[0]
    yc = y - _dot_split(y, gb_mean, 2)
    var = _dot_split(yc * yc, gb_mean, 2)
    yn = yc * lax.rsqrt(var + B_LN_EPS) * lng_ref[...] + lnb_ref[...]
    bonus = _dot_split(r_ref[0] * kt_ref[0] * rk_ref[...], gb_sum, 2)
    y2 = (yn + bonus * vb_ref[0]) * gg_ref[0]
    out = (jnp.dot(ha.astype(BF16), wo_ref[0:aw, :], preferred_element_type=F32)
           + jnp.dot(y2.astype(BF16), wo_ref[aw:aw + bw, :], preferred_element_type=F32))
    o_ref[0] = x_ref[0] + mod_ref[0][:, 2 * d:3 * d] * out


def _even_post(x3, mod3, ha, so, y, r, kt, vb, gg, ang, lng, lnb, rk, wo, tt):
    g_, ttot, d = x3.shape
    aw = d // 2
    bw = d // 2
    consts = [ang, lng, lnb, rk, wo]
    return pl.pallas_call(
        functools.partial(_even_post_body, d, aw, bw),
        grid=(g_, ttot // tt),
        in_specs=[_tok_spec(tt, d), _mod_spec(mod3, tt)] + [_tok_spec(tt, aw)] * 7 + [_full_spec(c) for c in consts],
        out_specs=_tok_spec(tt, d),
        out_shape=jax.ShapeDtypeStruct(x3.shape, F32),
        compiler_params=_params(2),
        name="even_post",
    )(x3, mod3, ha, so, y, r, kt, vb, gg, *consts)


def _odd_pre_body(j, d, x_ref, mod_ref, ng_ref, w_ref, lbl_ref, q_ref, k_ref, v_ref, lf_ref, sg_ref):
    x = x_ref[0]
    mod = mod_ref[0]
    h = _rmsnorm(x, ng_ref[...]) * (1.0 + mod[:, d:2 * d]) + mod[:, 0:d]
    p = jnp.dot(h.astype(BF16), w_ref[...], preferred_element_type=F32)
    lbl = lbl_ref[...]
    n_odd = lbl.shape[0]
    mx = lbl[0:1, :]
    for i in range(1, n_odd):
        mx = jnp.maximum(mx, lbl[i:i + 1, :])
    ex = [jnp.exp(lbl[i:i + 1, :] - mx) for i in range(n_odd)]
    tot = ex[0]
    for i in range(1, n_odd):
        tot = tot + ex[i]
    lb = jnp.zeros_like(mx)
    for i in range(1, j + 1):
        lb = lb + ex[i] / tot
    f = lb + (1.0 - lb) * _sigmoid(p[:, d:2 * d])
    q_ref[0] = _silu(p[:, 0:d])
    k_ref[0] = 1.0 - f
    v_ref[0] = p[:, 2 * d:3 * d]
    lf_ref[0] = jnp.log(f)
    sg_ref[0] = _silu(p[:, 3 * d:4 * d])


def _odd_pre(x3, mod3, ng, w_in, lbl, j, tt):
    g_, ttot, d = x3.shape
    consts = [ng, w_in, lbl]
    return pl.pallas_call(
        functools.partial(_odd_pre_body, j, d),
        grid=(g_, ttot // tt),
        in_specs=[_tok_spec(tt, d), _mod_spec(mod3, tt)] + [_full_spec(c) for c in consts],
        out_specs=[_tok_spec(tt, d)] * 5,
        out_shape=[jax.ShapeDtypeStruct(x3.shape, F32)] * 5,
        compiler_params=_params(2),
        name="odd_pre",
    )(x3, mod3, *consts)


def _col_bcast(row, terms):
    parts = _split(row, terms)
    r = lax.broadcasted_iota(jnp.int32, (SUBLANES, row.shape[1]), 0)
    lhs = jnp.zeros((SUBLANES, row.shape[1]), F32)
    for i, p in enumerate(parts):
        lhs = jnp.where(r == i, jnp.broadcast_to(p.astype(F32), lhs.shape), lhs)
    ones = jnp.where(lax.broadcasted_iota(jnp.int32, (SUBLANES, LANES), 0) < terms, 1.0, 0.0).astype(BF16)
    return lax.dot_general(lhs.astype(BF16), ones, (((0,), (0,)), ((), ())), preferred_element_type=F32)


def _hgrn_chunk_body(L, sub, nchunk, q_ref, k_ref, v_ref, g_ref, s0_ref, o_ref, s1_ref, S_s):
    @pl.when(pl.program_id(2) == 0)
    def _():
        S_s[...] = s0_ref[0, 0]

    nj = L // sub
    rowi = lax.broadcasted_iota(jnp.int32, (L, C_DK), 0)
    rr = lax.broadcasted_iota(jnp.int32, (L, L), 0)
    cc = lax.broadcasted_iota(jnp.int32, (L, L), 1)
    tril = jnp.where(cc <= rr, 1.0, 0.0).astype(BF16)
    ones = jnp.ones((C_DK, LANES), BF16)

    def chunk(ci, carry):
        base = pl.multiple_of(ci * L, L)
        q = q_ref[0, pl.ds(base, L), :]
        k = k_ref[0, pl.ds(base, L), :]
        v = v_ref[0, pl.ds(base, L), :]
        g = g_ref[0, pl.ds(base, L), :]
        b = _dot_split_left(tril, g, 3)
        S = S_s[...]
        o = _bdot(q * jnp.exp(b), S)
        for dlt in range(sub):
            if dlt == 0:
                pr = q * k
                vs = v
            else:
                ok = (rowi % sub) >= dlt
                e = jnp.exp(jnp.where(ok, b - pltpu.roll(b, dlt, 0), NEG_INF))
                pr = q * pltpu.roll(k, dlt, 0) * e
                vs = pltpu.roll(v, dlt, 0)
            o = o + _bdot(pr, ones) * vs
        if nj > 1:
            amat = None
            for jj in range(nj - 1):
                end = (jj + 1) * sub
                bref = b[end - 1:end, :]
                qj = q * jnp.exp(jnp.where(rowi >= end, b - bref, NEG_INF))
                kj = k * jnp.exp(jnp.where((rowi >= end - sub) & (rowi < end), bref - b, NEG_INF))
                a = _dot_nt(qj, kj)
                amat = a if amat is None else amat + a
            o = o + _bdot(amat, v)
        o_ref[0, pl.ds(base, L), :] = o
        b_last = b[L - 1:L, :]
        S_s[...] = _col_bcast(jnp.exp(b_last), 3) * S + _dot_tn(k * jnp.exp(b_last - b), v)
        return carry

    lax.fori_loop(0, nchunk, chunk, 0)

    @pl.when(pl.program_id(2) == pl.num_programs(2) - 1)
    def _():
        s1_ref[0, 0] = S_s[...]


def _hgrn_chunk(q, k, v, lf, S0, tblk, L, sub):
    b, t, d = q.shape
    tok = pl.BlockSpec((1, tblk, C_DK), lambda bb, h, c: (bb, c, h))
    st = pl.BlockSpec((1, 1, C_DK, C_DK), lambda bb, h, c: (bb, h, 0, 0))
    return pl.pallas_call(
        functools.partial(_hgrn_chunk_body, L, sub, tblk // L),
        grid=(b, C_HEADS, t // tblk),
        in_specs=[tok] * 4 + [st],
        out_specs=[tok, st],
        out_shape=[jax.ShapeDtypeStruct((b, t, d), F32), jax.ShapeDtypeStruct(S0.shape, F32)],
        scratch_shapes=[pltpu.VMEM((C_DK, C_DK), F32)],
        compiler_params=_params(3),
        name="hgrn_chunk",
    )(q, k, v, lf, S0)


def _hgrn_step_body(bb, q_ref, k_ref, v_ref, g_ref, s0_ref, o_ref, s1_ref):
    for b in range(bb):
        for hd in range(C_HEADS):
            sl = slice(hd * C_DK, (hd + 1) * C_DK)
            q = q_ref[b:b + 1, sl]
            k = k_ref[b:b + 1, sl]
            v = v_ref[b:b + 1, sl]
            f = jnp.exp(g_ref[b:b + 1, sl])
            S1 = _col_bcast(f, 3) * s0_ref[b, hd] + _dot_tn(_row8(k), _row8(v))
            o_ref[b:b + 1, sl] = _bdot(_row8(q), S1)[0:1, :]
            s1_ref[b, hd] = S1


def _hgrn_step(q, k, v, lf, S0, bb):
    b, d = q.shape
    row = pl.BlockSpec((bb, d), lambda i: (i, 0))
    st = pl.BlockSpec((bb, C_HEADS, C_DK, C_DK), lambda i: (i, 0, 0, 0))
    return pl.pallas_call(
        functools.partial(_hgrn_step_body, bb),
        grid=(b // bb,),
        in_specs=[row] * 4 + [st],
        out_specs=[row, st],
        out_shape=[jax.ShapeDtypeStruct((b, d), F32), jax.ShapeDtypeStruct(S0.shape, F32)],
        compiler_params=_params(1),
        name="hgrn_step",
    )(q, k, v, lf, S0)


def _odd_post_body(d, x_ref, mod_ref, o_ref, sg_ref, ng_ref, wo_ref, out_ref):
    gm = _group_matrix(d, C_DK, 1.0 / C_DK)
    o = o_ref[0]
    o = o * lax.rsqrt(_dot_split(o * o, gm, 2) + EPS) * ng_ref[...] * sg_ref[0]
    out = jnp.dot(o.astype(BF16), wo_ref[...], preferred_element_type=F32)
    out_ref[0] = x_ref[0] + mod_ref[0][:, 2 * d:3 * d] * out


def _odd_post(x3, mod3, o, sg, ng_tiled, wo, tt):
    g_, ttot, d = x3.shape
    consts = [ng_tiled, wo]
    return pl.pallas_call(
        functools.partial(_odd_post_body, d),
        grid=(g_, ttot // tt),
        in_specs=[_tok_spec(tt, d), _mod_spec(mod3, tt)] + [_tok_spec(tt, d)] * 2 + [_full_spec(c) for c in consts],
        out_specs=_tok_spec(tt, d),
        out_shape=jax.ShapeDtypeStruct(x3.shape, F32),
        compiler_params=_params(2),
        name="odd_post",
    )(x3, mod3, o, sg, *consts)


def _top_sorted(s, count):
    ridx = lax.broadcasted_iota(jnp.int32, (count, s.shape[1]), 0)
    acc = jnp.zeros((count, s.shape[1]), F32)
    cur = s
    for i in range(count):
        m = jnp.max(cur, axis=0, keepdims=True)
        acc = jnp.where(ridx == i, m, acc)
        cur = jnp.where(cur == m, NEG_INF, cur)
    return acc


def _peer_body(tm, te, d, x_ref, mod_ref, ng_ref, wqt_ref, keys_ref, u_ref, vt_ref, o_ref,
               ht_s, s1_s, e1_s, s2_s, e2_s, tau_s, act_s, w_s, acc_s):
    j = pl.program_id(2)
    dk2 = keys_ref.shape[3]
    nsub = te // P_NKEYS

    @pl.when(j == 0)
    def _():
        x = x_ref[0]
        mod = mod_ref[0]
        h = _rmsnorm(x, ng_ref[...]) * (1.0 + mod[:, 4 * d:5 * d]) + mod[:, 3 * d:4 * d]
        ht = h.T.astype(BF16)
        ht_s[...] = ht
        qt = jnp.dot(wqt_ref[...], ht, preferred_element_type=F32)
        r8 = lax.broadcasted_iota(jnp.int32, (SUBLANES, tm), 0)
        for hd in range(P_HEADS):
            s1 = _bdot(keys_ref[hd, 0], qt[(2 * hd) * dk2:(2 * hd + 1) * dk2, :])
            s2 = _bdot(keys_ref[hd, 1], qt[(2 * hd + 1) * dk2:(2 * hd + 2) * dk2, :])
            a = _top_sorted(s1, P_TOPK)
            b = _top_sorted(s2, P_TOPK)
            blocks = [a[0:1, :] + b, a[1:2, :] + b[0:SUBLANES, :]]
            for i in range(3, SUBLANES + 1):
                blocks.append(jnp.where(r8 < P_TOPK // i, a[i - 1:i, :] + b[0:SUBLANES, :], NEG_INF))
            blocks.append(a[SUBLANES:2 * SUBLANES, :] + b[0:1, :])
            cand = jnp.concatenate(blocks, axis=0)
            cur = cand
            tau = None
            for _ in range(P_TOPK):
                tau = jnp.max(cur, axis=0, keepdims=True)
                cur = jnp.where(cur == tau, NEG_INF, cur)
            top = a[0:1, :] + b[0:1, :]
            z = jnp.sum(jnp.where(cand >= tau, jnp.exp(cand - top), 0.0), axis=0, keepdims=True)
            s1_s[hd] = s1
            e1_s[hd] = jnp.exp(s1 - a[0:1, :])
            s2_s[hd] = s2
            e2_s[hd] = jnp.exp(s2 - b[0:1, :]) / z
            tau_s[hd:hd + 1, :] = tau
        acc_s[...] = jnp.zeros_like(acc_s)

    act_s[...] = _gelu_tanh(jnp.dot(u_ref[0], ht_s[...], preferred_element_type=F32))

    assert nsub == SUBLANES
    grp = pl.ds(pl.multiple_of(j * nsub, nsub), nsub)
    for tc in range(tm // LANES):
        ls = slice(tc * LANES, (tc + 1) * LANES)
        s1g = [s1_s[hd, grp, ls] for hd in range(P_HEADS)]
        e1g = [e1_s[hd, grp, ls] for hd in range(P_HEADS)]
        for ii in range(nsub):
            rows = slice(ii * P_NKEYS, (ii + 1) * P_NKEYS)
            gate = jnp.zeros((P_NKEYS, LANES), F32)
            for hd in range(P_HEADS):
                keep = (s1g[hd][ii:ii + 1, :] + s2_s[hd, :, ls]) >= tau_s[hd:hd + 1, ls]
                gate = gate + jnp.where(keep, e2_s[hd, :, ls], 0.0) * e1g[hd][ii:ii + 1, :]
            w_s[rows, ls] = (gate * act_s[rows, ls]).astype(BF16)
    acc_s[...] += jnp.dot(vt_ref[0], w_s[...], preferred_element_type=F32)

    @pl.when(j == pl.num_programs(2) - 1)
    def _():
        o_ref[0] = x_ref[0] + mod_ref[0][:, 5 * d:6 * d] * acc_s[...].T


def _peer(x3, mod3, ng, wqt, keys_bf, u_bf, vt_bf, tm, te):
    g_, ttot, d = x3.shape
    ne = u_bf.shape[0] // te
    nk = P_NKEYS
    if mod3.shape[1] == 1:
        mod_spec = pl.BlockSpec((1, 1, mod3.shape[2]), lambda g, i, j: (g, 0, 0))
    else:
        mod_spec = pl.BlockSpec((1, tm, mod3.shape[2]), lambda g, i, j: (g, i, 0))
    full = lambda arr: pl.BlockSpec(arr.shape, lambda g, i, j: (0,) * arr.ndim)
    u3 = u_bf.reshape(ne, te, d)
    vt3 = vt_bf.reshape(d, ne, te).transpose(1, 0, 2)
    return pl.pallas_call(
        functools.partial(_peer_body, tm, te, d),
        grid=(g_, ttot // tm, ne),
        in_specs=[pl.BlockSpec((1, tm, d), lambda g, i, j: (g, i, 0)), mod_spec, full(ng), full(wqt), full(keys_bf),
                  pl.BlockSpec((1, te, d), lambda g, i, j: (j, 0, 0)),
                  pl.BlockSpec((1, d, te), lambda g, i, j: (j, 0, 0))],
        out_specs=pl.BlockSpec((1, tm, d), lambda g, i, j: (g, i, 0)),
        out_shape=jax.ShapeDtypeStruct(x3.shape, F32),
        scratch_shapes=[pltpu.VMEM((d, tm), BF16),
                        pltpu.VMEM((P_HEADS, nk, tm), F32), pltpu.VMEM((P_HEADS, nk, tm), F32),
                        pltpu.VMEM((P_HEADS, nk, tm), F32), pltpu.VMEM((P_HEADS, nk, tm), F32),
                        pltpu.VMEM((P_HEADS, tm), F32),
                        pltpu.VMEM((te, tm), F32), pltpu.VMEM((te, tm), BF16), pltpu.VMEM((d, tm), F32)],
        compiler_params=_params(3),
        name="peer",
    )(x3, mod3, ng, wqt, keys_bf, u3, vt3)


def _final_norm_body(x_ref, g_ref, o_ref):
    o_ref[0] = _rmsnorm(x_ref[0], g_ref[...])


def _final_norm(x3, g, tt):
    g_, ttot, d = x3.shape
    return pl.pallas_call(
        _final_norm_body,
        grid=(g_, ttot // tt),
        in_specs=[_tok_spec(tt, d), _full_spec(g)],
        out_specs=_tok_spec(tt, d),
        out_shape=jax.ShapeDtypeStruct(x3.shape, F32),
        compiler_params=_params(2),
        name="final_norm",
    )(x3, g)


def _pair_state(S):
    b = S.shape[0]
    return S.reshape(b, B_HEADS // 2, 2, B_HEAD, B_HEAD).transpose(0, 1, 3, 2, 4).reshape(b, B_HEADS // 2, B_HEAD, LANES)


def _unpair_state(Sp):
    b = Sp.shape[0]
    return Sp.reshape(b, B_HEADS // 2, B_HEAD, 2, B_HEAD).transpose(0, 1, 3, 2, 4).reshape(b, B_HEADS, B_HEAD, B_HEAD)


def _trunk(x, mod_all, st, w, seq):
    bsz, t, d = x.shape
    depth = mod_all.shape[0]
    aw = d // 2
    bw = d // 2
    if seq:
        x3 = x
        tt = min(256, t)
        tm = min(512, t)
        mods = [mod_all[l][:, None, :] for l in range(depth)]
    else:
        x3 = x.reshape(1, bsz * t, d)
        tt = bsz * t
        tm = bsz * t
        mods = [mod_all[l][None] for l in range(depth)]
    row = lambda v_: v_.reshape(1, -1)
    new_even, new_odd = [], []
    for l in range(depth):
        j = l // 2
        if l % 2 == 0:
            e = w["even"][j]
            if seq:
                prev = None
            else:
                conv0, shift0 = st[3][j], st[5][j]
                prev = (conv0[:, 0][None], conv0[:, 1][None], conv0[:, 2][None], shift0[:, 0][None])
            (q, k, v, so, g8, g8t, r, dec, kt, vb, al, be, gg, ctail, stail) = _even_pre(
                x3, mods[l], row(w["norm_mix_g"][l]), e["wcat"], e["conv_w"], row(e["conv_b"]), e["gate_b"],
                row(e["mu"]), row(e["w0"]), row(e["a0"]), e["wla"], e["g2"], row(e["k_k"]), row(e["k_a"]), prev, tt)
            if seq:
                C0 = jnp.zeros((bsz, A_HEADS, A_DK, A_DK), F32)
                nm0 = jnp.zeros((bsz, SUBLANES, LANES), F32)
                ha, C1, n1p, m1p = _mlstm_chunk(q, k, v, g8, g8t, C0, nm0, nm0, min(256, t))
                n1 = n1p[:, :A_HEADS, :]
                m1 = m1p[:, :A_HEADS, 0]
                S0p = jnp.zeros((bsz, B_HEADS // 2, B_HEAD, LANES), F32)
                y4, S1p = _rwkv_scan(al, dec, be, kt, r, vb, S0p, min(8, bsz), min(128, t))
                y = y4.reshape(bsz, t, bw)
                conv1 = ctail[:, SUBLANES - (A_CONV - 1):, :]
                shift1 = stail[:, SUBLANES - 1:, :]
            else:
                ha2, C1, n1, m1 = _mlstm_step(q[0], k[0], v[0], g8[0], st[0][j], st[1][j], st[2][j], 8)
                ha = ha2[None]
                tok = lambda a_: a_[0][:, None, :]
                y4, S1p = _rwkv_scan(tok(al), tok(dec), tok(be), tok(kt), tok(r), tok(vb), _pair_state(st[4][j]), 8, 1)
                y = y4.reshape(1, bsz, bw)
                conv1 = jnp.concatenate([st[3][j][:, 1:], ctail[0][:, None, :]], axis=1)
                shift1 = stail[0][:, None, :]
            x3 = _even_post(x3, mods[l], ha, so, y, r, kt, vb, gg, row(e["a_norm_g"]), row(e["ln_g"]),
                            row(e["ln_b"]), row(e["r_k"]), e["w_out"], tt)
            new_even.append((C1, n1, m1, conv1, _unpair_state(S1p), shift1))
        else:
            o_ = w["odd"][j]
            q, k, v, lf, sg = _odd_pre(x3, mods[l], row(w["norm_mix_g"][l]), o_["w_in"], w["lb_logits"], j, tt)
            if seq:
                S0 = jnp.zeros((bsz, C_HEADS, C_DK, C_DK), F32)
                o, S1 = _hgrn_chunk(q, k, v, lf, S0, min(256, t), min(64, t), min(16, t))
            else:
                o2, S1 = _hgrn_step(q[0], k[0], v[0], lf[0], st[6][j], 8)
                o = o2[None]
            x3 = _odd_post(x3, mods[l], o, sg, row(jnp.tile(o_["norm_g"], C_HEADS)), o_["w_out"], tt)
            new_odd.append(S1)
        pw = w["peer"][l]
        x3 = _peer(x3, mods[l], row(w["norm_ffn_g"][l]), pw["wqt"], pw["keys"], pw["u"], pw["vt"], tm, 1024)
    y = _final_norm(x3, row(w["norm_final_g"]), tt).reshape(bsz, t, d)
    ev = [jnp.stack([s[i] for s in new_even]) for i in range(6)]
    return (y, ev[0], ev[1], ev[2], ev[3], ev[4], ev[5], jnp.stack(new_odd))


def _prepare_weights(norm_mix_g, norm_ffn_g, norm_final_g, even_w_in, even_w_out, mlstm_conv_w, mlstm_conv_b,
                     mlstm_gate_b, mlstm_norm_g, rwkv_mu, rwkv_w0, rwkv_w2, rwkv_a0, rwkv_a2, rwkv_g2, rwkv_k_k,
                     rwkv_k_a, rwkv_r_k, rwkv_ln_g, rwkv_ln_b, odd_w_in, odd_w_out, hgrn_lb_logits, hgrn_norm_g,
                     peer_w_q, peer_keys, peer_u, peer_v):
    d = even_w_in.shape[1]
    aw = d // 2
    bw = d // 2
    sw = 3 * bw + B_LORA_W + B_LORA_A + B_LORA_G
    even = []
    for j in range(even_w_in.shape[0]):
        wi = even_w_in[j]
        gates_w = wi[:, 4 * aw:4 * aw + 2 * A_HEADS]
        wcat = jnp.concatenate([wi[:, 0:4 * aw], wi[:, 4 * aw + 2 * A_HEADS:],
                                jnp.pad(gates_w, ((0, 0), (0, LANES - 2 * A_HEADS)))], axis=1).astype(BF16)
        assert wcat.shape[1] == 4 * aw + sw + LANES
        wla = jnp.zeros((B_LORA_W + B_LORA_A, 2 * bw), F32)
        wla = wla.at[:B_LORA_W, :bw].set(rwkv_w2[j]).at[B_LORA_W:, bw:].set(rwkv_a2[j]).astype(BF16)
        even.append(dict(
            wcat=wcat, conv_w=mlstm_conv_w[j], conv_b=mlstm_conv_b[j],
            gate_b=jnp.pad(mlstm_gate_b[j], (0, LANES - 2 * A_HEADS)).reshape(1, LANES),
            mu=rwkv_mu[j], w0=rwkv_w0[j], a0=rwkv_a0[j], wla=wla, g2=rwkv_g2[j].astype(BF16),
            k_k=rwkv_k_k[j], k_a=rwkv_k_a[j], r_k=rwkv_r_k[j], ln_g=rwkv_ln_g[j], ln_b=rwkv_ln_b[j],
            a_norm_g=mlstm_norm_g[j], w_out=even_w_out[j].astype(BF16)))
    odd = [dict(w_in=odd_w_in[j].astype(BF16), w_out=odd_w_out[j].astype(BF16), norm_g=hgrn_norm_g[j])
           for j in range(odd_w_in.shape[0])]
    peer = [dict(wqt=peer_w_q[l].T.astype(BF16), keys=peer_keys[l].astype(BF16), u=peer_u[l].astype(BF16),
                 vt=peer_v[l].T.astype(BF16)) for l in range(peer_w_q.shape[0])]
    return dict(norm_mix_g=norm_mix_g, norm_ffn_g=norm_ffn_g, norm_final_g=norm_final_g, even=even, odd=odd,
                peer=peer, lb_logits=hgrn_lb_logits)


def kernel(x_prompt, x_sample, c_prompt, c_sample, state_mlstm_C, state_mlstm_n, state_mlstm_m, state_mlstm_conv, state_rwkv_S, state_rwkv_shift, state_hgrn_S, norm_mix_g, norm_ffn_g, norm_final_g, ada_w, ada_b, even_w_in, even_w_out, mlstm_conv_w, mlstm_conv_b, mlstm_gate_b, mlstm_norm_g, rwkv_mu, rwkv_w0, rwkv_w2, rwkv_a0, rwkv_a2, rwkv_g2, rwkv_k_k, rwkv_k_a, rwkv_r_k, rwkv_ln_g, rwkv_ln_b, odd_w_in, odd_w_out, hgrn_lb_logits, hgrn_norm_g, peer_w_q, peer_keys, peer_u, peer_v):
    w = _prepare_weights(norm_mix_g, norm_ffn_g, norm_final_g, even_w_in, even_w_out, mlstm_conv_w, mlstm_conv_b,
                         mlstm_gate_b, mlstm_norm_g, rwkv_mu, rwkv_w0, rwkv_w2, rwkv_a0, rwkv_a2, rwkv_g2, rwkv_k_k,
                         rwkv_k_a, rwkv_r_k, rwkv_ln_g, rwkv_ln_b, odd_w_in, odd_w_out, hgrn_lb_logits, hgrn_norm_g,
                         peer_w_q, peer_keys, peer_u, peer_v)
    bp = x_prompt.shape[0]
    mod = _ada(jnp.concatenate([c_prompt, c_sample], axis=0), ada_w, ada_b)
    out_p = _trunk(x_prompt, mod[:, :bp], None, w, True)
    st = (state_mlstm_C, state_mlstm_n, state_mlstm_m, state_mlstm_conv, state_rwkv_S, state_rwkv_shift,
          state_hgrn_S)
    out_s = _trunk(x_sample, mod[:, bp:], st, w, False)
    return (out_p[0], out_s[0]) + tuple(out_p[1:]) + tuple(out_s[1:])
```

```python
import functools
import math

import jax
import jax.numpy as jnp
from jax import lax
from jax.experimental import pallas as pl
from jax.experimental.pallas import tpu as pltpu

F32 = jnp.float32
BF16 = jnp.bfloat16

EPS = 1e-6
A_HEADS = 4
A_DK = 128
A_CONV = 4
B_HEADS = 8
B_HEAD = 64
B_LORA_W = 64
B_LORA_A = 64
B_LORA_G = 128
B_LN_EPS = 64e-5
C_HEADS = 8
C_DK = 128
P_HEADS = 8
P_NKEYS = 128
P_TOPK = 16

LANES = 128
SUBLANES = 8
VMEM_LIMIT = 56 * 1024 * 1024
NEG_INF = float("-inf")


def _bdot(a, b):
    return jnp.dot(a.astype(BF16), b.astype(BF16), preferred_element_type=F32)


def _dot_nt(a, b):
    return lax.dot_general(a.astype(BF16), b.astype(BF16), (((1,), (1,)), ((), ())), preferred_element_type=F32)


def _dot_tn(a, b):
    return lax.dot_general(a.astype(BF16), b.astype(BF16), (((0,), (0,)), ((), ())), preferred_element_type=F32)


def _split(x, terms):
    parts = []
    rem = x
    for _ in range(terms):
        p = rem.astype(BF16)
        parts.append(p)
        rem = rem - p.astype(F32)
    return parts


def _dot_split(a, b_bf, terms):
    out = None
    for p in _split(a, terms):
        d = jnp.dot(p, b_bf, preferred_element_type=F32)
        out = d if out is None else out + d
    return out


def _dot_split_left(a_bf, b, terms):
    out = None
    for p in _split(b, terms):
        d = jnp.dot(a_bf, p, preferred_element_type=F32)
        out = d if out is None else out + d
    return out


def _sigmoid(x):
    return 1.0 / (1.0 + jnp.exp(-x))


def _silu(x):
    return x * _sigmoid(x)


def _softplus(x):
    return jnp.maximum(x, 0.0) + jnp.log(1.0 + jnp.exp(-jnp.abs(x)))


def _gelu_tanh(x):
    return 0.5 * x * (1.0 + jnp.tanh(math.sqrt(2.0 / math.pi) * (x + 0.044715 * (x * x * x))))


def _rmsnorm(x, g):
    return x * lax.rsqrt(jnp.mean(x * x, axis=-1, keepdims=True) + EPS) * g


def _group_matrix(n, group, value):
    r = lax.broadcasted_iota(jnp.int32, (n, n), 0) // group
    c = lax.broadcasted_iota(jnp.int32, (n, n), 1) // group
    return jnp.where(r == c, value, 0.0).astype(BF16)


def _full_spec(arr):
    nd = arr.ndim
    return pl.BlockSpec(arr.shape, lambda *_: (0,) * nd)


def _params(n_grid):
    return pltpu.CompilerParams(dimension_semantics=("arbitrary",) * n_grid, vmem_limit_bytes=VMEM_LIMIT)


def _tok_spec(tt, width):
    return pl.BlockSpec((1, tt, width), lambda g, i: (g, i, 0))


def _mod_spec(mod3, tt):
    width = mod3.shape[2]
    if mod3.shape[1] == 1:
        return pl.BlockSpec((1, 1, width), lambda g, i: (g, 0, 0))
    return pl.BlockSpec((1, tt, width), lambda g, i: (g, i, 0))


def _ada_body(c_ref, w_ref, b_ref, o_ref):
    c = c_ref[...]
    o_ref[0] = _bdot(_silu(c), w_ref[0]) + b_ref[0]


def _ada(c_all, ada_w, ada_b):
    depth, d, n6 = ada_w.shape
    bc = c_all.shape[0]
    tn = n6 // 4
    return pl.pallas_call(
        _ada_body,
        grid=(depth, n6 // tn),
        in_specs=[pl.BlockSpec((bc, d), lambda l, j: (0, 0)),
                  pl.BlockSpec((1, d, tn), lambda l, j: (l, 0, j)),
                  pl.BlockSpec((1, 1, tn), lambda l, j: (l, 0, j))],
        out_specs=pl.BlockSpec((1, bc, tn), lambda l, j: (l, 0, j)),
        out_shape=jax.ShapeDtypeStruct((depth, bc, n6), F32),
        compiler_params=_params(2),
        name="ada_mod",
    )(c_all, ada_w, ada_b.reshape(depth, 1, n6))


def _even_pre_body(seq, tt, d, aw, bw, *refs):
    n_in = 15 if seq else 19
    ins, outs = refs[:n_in], refs[n_in:]
    (x_ref, mod_ref, ng_ref, w_ref, cw_ref, cb_ref, gb_ref, mu_ref, w0_ref, a0_ref, wla_ref, g2_ref,
     kk_ref, ka_ref, gs_ref) = ins[:15]
    (q_ref, k_ref, v_ref, so_ref, g8_ref, g8t_ref, r_ref, dec_ref, kt_ref, vb_ref, al_ref, be_ref, gg_ref,
     ctail_ref, stail_ref) = outs[:15]
    a2 = 2 * aw
    sw = 3 * bw + B_LORA_W + B_LORA_A + B_LORA_G

    x = x_ref[0]
    mod = mod_ref[0]
    h = _rmsnorm(x, ng_ref[...]) * (1.0 + mod[:, d:2 * d]) + mod[:, 0:d]
    p = jnp.dot(h.astype(BF16), w_ref[...], preferred_element_type=F32)
    u = p[:, 0:a2]
    v = p[:, a2:a2 + aw]
    o = p[:, a2 + aw:a2 + 2 * aw]
    pb = p[:, 4 * aw:4 * aw + sw]
    gt = p[:, 4 * aw + sw:4 * aw + sw + LANES]

    if seq:
        ubuf, pbuf = outs[15], outs[16]

        @pl.when(pl.program_id(1) == 0)
        def _():
            ubuf[0:SUBLANES, :] = jnp.zeros((SUBLANES, a2), F32)
            pbuf[0:SUBLANES, :] = jnp.zeros((SUBLANES, sw), F32)

        ubuf[SUBLANES:SUBLANES + tt, :] = u
        pbuf[SUBLANES:SUBLANES + tt, :] = pb
        u1 = ubuf[SUBLANES - 1:SUBLANES - 1 + tt, :]
        u2 = ubuf[SUBLANES - 2:SUBLANES - 2 + tt, :]
        u3 = ubuf[SUBLANES - 3:SUBLANES - 3 + tt, :]
        pprev = pbuf[SUBLANES - 1:SUBLANES - 1 + tt, :]
        ubuf[0:SUBLANES, :] = u[tt - SUBLANES:tt, :]
        pbuf[0:SUBLANES, :] = pb[tt - SUBLANES:tt, :]
        ctail_ref[0] = u[tt - SUBLANES:tt, :]
        stail_ref[0] = pb[tt - SUBLANES:tt, :]
    else:
        u3, u2, u1, pprev = ins[15][0], ins[16][0], ins[17][0], ins[18][0]
        ctail_ref[0] = u
        stail_ref[0] = pb

    cw = cw_ref[...]
    y = cb_ref[...] + u3 * cw[0:1, :] + u2 * cw[1:2, :] + u1 * cw[2:3, :] + u * cw[3:4, :]
    qk = _silu(y)
    q_ref[0] = qk[:, 0:aw] * (A_DK ** -0.5)
    k_ref[0] = qk[:, aw:a2]
    v_ref[0] = v
    so_ref[0] = _sigmoid(o)
    g = gt + gb_ref[...]
    lane = lax.broadcasted_iota(jnp.int32, g.shape, 1)
    gates = jnp.where(lane < A_HEADS, g, -_softplus(-g))
    g8_ref[0] = gates[:, 0:2 * A_HEADS]
    g8t_ref[0] = gates.T[0:2 * A_HEADS, :]

    xb = pb + mu_ref[...] * (pprev - pb)
    r = xb[:, 0:bw]
    kb = xb[:, bw:2 * bw]
    vb = xb[:, 2 * bw:3 * bw]
    la = xb[:, 3 * bw:3 * bw + LANES]
    gl = xb[:, 3 * bw + LANES:3 * bw + 2 * LANES]
    lane2 = lax.broadcasted_iota(jnp.int32, la.shape, 1)
    la_act = jnp.where(lane2 < B_LORA_W, jnp.tanh(la), la)
    lw = jnp.dot(la_act.astype(BF16), wla_ref[...], preferred_element_type=F32)
    wlog = -_softplus(-(w0_ref[...] + lw[:, 0:bw])) - 0.5
    decay = jnp.exp(-jnp.exp(wlog))
    a = _sigmoid(a0_ref[...] + lw[:, bw:2 * bw])
    gg = jnp.dot(_sigmoid(gl).astype(BF16), g2_ref[...], preferred_element_type=F32)
    kk = kb * kk_ref[...]
    ss = _dot_split(kk * kk, gs_ref[...], 2)
    kkn = kk * lax.rsqrt(jnp.maximum(ss, 1e-24))
    r_ref[0] = r
    dec_ref[0] = decay
    kt_ref[0] = kb * (1.0 + (a - 1.0) * ka_ref[...])
    vb_ref[0] = vb
    al_ref[0] = -kkn
    be_ref[0] = kkn * a
    gg_ref[0] = gg


def _even_pre(x3, mod3, ng, wcat, cw, cb, gb, mu, w0, a0, wla, g2, k_k, k_a, prev, tt):
    g_, ttot, d = x3.shape
    aw = d // 2
    bw = d // 2
    a2 = 2 * aw
    sw = 3 * bw + B_LORA_W + B_LORA_A + B_LORA_G
    seq = prev is None
    gsum = _group_matrix(bw, B_HEAD, 1.0)
    consts = [ng, wcat, cw, cb, gb, mu, w0, a0, wla, g2, k_k, k_a, gsum]
    ins = [x3, mod3] + consts
    in_specs = [_tok_spec(tt, d), _mod_spec(mod3, tt)] + [_full_spec(c) for c in consts]
    if not seq:
        ins += list(prev)
        in_specs += [_tok_spec(tt, a2)] * 3 + [_tok_spec(tt, sw)]
    nt = ttot // tt
    tail_rows = SUBLANES if seq else tt
    tail_tot = g_ * SUBLANES if seq else ttot

    def tail_spec(width):
        if seq:
            return pl.BlockSpec((1, SUBLANES, width), lambda g, i: (g, 0, 0))
        return _tok_spec(tt, width)

    def tail_shape(width):
        if seq:
            return jax.ShapeDtypeStruct((g_, SUBLANES, width), F32)
        return jax.ShapeDtypeStruct((g_, ttot, width), F32)

    tok = lambda w: jax.ShapeDtypeStruct((g_, ttot, w), F32)
    out_shape = [tok(aw), tok(aw), tok(aw), tok(aw), tok(2 * A_HEADS),
                 jax.ShapeDtypeStruct((g_, 2 * A_HEADS, ttot), F32),
                 tok(bw), tok(bw), tok(bw), tok(bw), tok(bw), tok(bw), tok(bw),
                 tail_shape(a2), tail_shape(sw)]
    out_specs = [_tok_spec(tt, aw)] * 4 + [_tok_spec(tt, 2 * A_HEADS),
                                           pl.BlockSpec((1, 2 * A_HEADS, tt), lambda g, i: (g, 0, i))]
    out_specs += [_tok_spec(tt, bw)] * 7 + [tail_spec(a2), tail_spec(sw)]
    scratch = [pltpu.VMEM((tt + SUBLANES, a2), F32), pltpu.VMEM((tt + SUBLANES, sw), F32)] if seq else []
    del tail_rows, tail_tot
    return pl.pallas_call(
        functools.partial(_even_pre_body, seq, tt, d, aw, bw),
        grid=(g_, nt),
        in_specs=in_specs,
        out_specs=out_specs,
        out_shape=out_shape,
        scratch_shapes=scratch,
        compiler_params=_params(2),
        name="even_pre_seq" if seq else "even_pre_step",
    )(*ins)


def _mlstm_chunk_body(L, q_ref, k_ref, v_ref, g8_ref, g8t_ref, c0_ref, n0_ref, m0_ref,
                      h_ref, c1_ref, n1_ref, m1_ref, C_s, n_s, m_s):
    c = pl.program_id(1)

    @pl.when(c == 0)
    def _():
        C_s[...] = c0_ref[0]
        n_s[...] = n0_ref[0]
        m_s[...] = m0_ref[0]

    row = lax.broadcasted_iota(jnp.int32, (L, L), 0)
    col = lax.broadcasted_iota(jnp.int32, (L, L), 1)
    causal = col <= row
    tril = jnp.where(causal, 1.0, 0.0).astype(BF16)
    triu = jnp.where(row <= col, 1.0, 0.0).astype(BF16)
    g8 = g8_ref[0]
    g8t = g8t_ref[0]
    bcols = _dot_split_left(tril, g8, 3)
    brows = _dot_split(g8t, triu, 3)
    for hd in range(A_HEADS):
        sl = slice(hd * A_DK, (hd + 1) * A_DK)
        q = q_ref[0, :, sl]
        k = k_ref[0, :, sl]
        v = v_ref[0, :, sl]
        i_row = g8t[hd:hd + 1, :]
        i_col = g8[:, hd:hd + 1]
        b_row = brows[A_HEADS + hd:A_HEADS + hd + 1, :]
        b_col = bcols[:, A_HEADS + hd:A_HEADS + hd + 1]
        m_prev = m_s[hd:hd + 1, 0:1]
        C = C_s[hd]
        n = n_s[hd:hd + 1, :]
        dmat = jnp.where(causal, b_col - b_row + i_row, NEG_INF)
        inter = b_col + m_prev
        m_t = jnp.maximum(inter, jnp.max(dmat, axis=1, keepdims=True))
        w_intra = jnp.exp(dmat - m_t)
        w_inter = jnp.exp(inter - m_t)
        s = _dot_nt(q, k) * w_intra
        num = _bdot(s, v) + w_inter * _bdot(q, C)
        den = jnp.sum(s, axis=1, keepdims=True) + w_inter * jnp.sum(q * n, axis=1, keepdims=True)
        h_ref[0, :, sl] = num / jnp.maximum(jnp.abs(den), jnp.exp(-m_t))
        m_new = m_t[L - 1:L, :]
        b_last = b_col[L - 1:L, :]
        kw = k * jnp.exp(b_last - b_col + i_col - m_new)
        carry = jnp.exp(b_last + m_prev - m_new)
        C_s[hd] = carry * C + _dot_tn(kw, v)
        n_s[hd:hd + 1, :] = carry * n + jnp.sum(kw, axis=0, keepdims=True)
        m_s[hd:hd + 1, :] = jnp.broadcast_to(m_new, (1, LANES))

    @pl.when(c == pl.num_programs(1) - 1)
    def _():
        c1_ref[0] = C_s[...]
        n1_ref[0] = n_s[...]
        m1_ref[0] = m_s[...]


def _mlstm_chunk(q, k, v, g8, g8t, C0, n0p, m0p, L):
    b, t, aw = q.shape
    st = lambda *s: pl.BlockSpec((1,) + s, lambda bb, c: (bb,) + (0,) * len(s))
    return pl.pallas_call(
        functools.partial(_mlstm_chunk_body, L),
        grid=(b, t // L),
        in_specs=[_tok_spec(L, aw)] * 3 + [_tok_spec(L, 2 * A_HEADS),
                                           pl.BlockSpec((1, 2 * A_HEADS, L), lambda bb, c: (bb, 0, c)),
                                           st(A_HEADS, A_DK, A_DK), st(SUBLANES, LANES), st(SUBLANES, LANES)],
        out_specs=[_tok_spec(L, aw), st(A_HEADS, A_DK, A_DK), st(SUBLANES, LANES), st(SUBLANES, LANES)],
        out_shape=[jax.ShapeDtypeStruct((b, t, aw), F32), jax.ShapeDtypeStruct(C0.shape, F32),
                   jax.ShapeDtypeStruct(n0p.shape, F32), jax.ShapeDtypeStruct(m0p.shape, F32)],
        scratch_shapes=[pltpu.VMEM((A_HEADS, A_DK, A_DK), F32), pltpu.VMEM((SUBLANES, LANES), F32),
                        pltpu.VMEM((SUBLANES, LANES), F32)],
        compiler_params=_params(2),
        name="mlstm_chunk",
    )(q, k, v, g8, g8t, C0, n0p, m0p)


def _row8(x):
    r = lax.broadcasted_iota(jnp.int32, (SUBLANES, x.shape[1]), 0)
    return jnp.where(r == 0, jnp.broadcast_to(x, (SUBLANES, x.shape[1])), 0.0)


def _mlstm_step_body(bb, q_ref, k_ref, v_ref, g8_ref, c0_ref, n0_ref, m0_ref, h_ref, c1_ref, n1_ref, m1_ref):
    g8 = g8_ref[...]
    m0 = m0_ref[...]
    for hd in range(A_HEADS):
        sl = slice(hd * A_DK, (hd + 1) * A_DK)
        li = g8[:, hd:hd + 1]
        lf = g8[:, A_HEADS + hd:A_HEADS + hd + 1]
        mp = m0[:, hd:hd + 1]
        m1 = jnp.maximum(lf + mp, li)
        wi = jnp.exp(li - m1)
        wf = jnp.exp(lf + mp - m1)
        floor = jnp.exp(-m1)
        for b in range(bb):
            q = q_ref[b:b + 1, sl]
            k = k_ref[b:b + 1, sl]
            v = v_ref[b:b + 1, sl]
            wib = wi[b:b + 1, :]
            wfb = wf[b:b + 1, :]
            C1 = wfb * c0_ref[b, hd] + wib * _dot_tn(_row8(k), _row8(v))
            n1 = wfb * n0_ref[b, hd:hd + 1, :] + wib * k
            num = _bdot(_row8(q), C1)[0:1, :]
            den = jnp.sum(q * n1, axis=1, keepdims=True)
            h_ref[b:b + 1, sl] = num / jnp.maximum(jnp.abs(den), floor[b:b + 1, :])
            c1_ref[b, hd] = C1
            n1_ref[b, hd:hd + 1, :] = n1
        m1_ref[:, hd:hd + 1] = m1


def _mlstm_step(q, k, v, g8, C0, n0, m0, bb):
    b, aw = q.shape
    row = lambda w: pl.BlockSpec((bb, w), lambda i: (i, 0))
    return pl.pallas_call(
        functools.partial(_mlstm_step_body, bb),
        grid=(b // bb,),
        in_specs=[row(aw)] * 3 + [row(2 * A_HEADS),
                                  pl.BlockSpec((bb, A_HEADS, A_DK, A_DK), lambda i: (i, 0, 0, 0)),
                                  pl.BlockSpec((bb, A_HEADS, A_DK), lambda i: (i, 0, 0)), row(A_HEADS)],
        out_specs=[row(aw), pl.BlockSpec((bb, A_HEADS, A_DK, A_DK), lambda i: (i, 0, 0, 0)),
                   pl.BlockSpec((bb, A_HEADS, A_DK), lambda i: (i, 0, 0)), row(A_HEADS)],
        out_shape=[jax.ShapeDtypeStruct((b, aw), F32), jax.ShapeDtypeStruct(C0.shape, F32),
                   jax.ShapeDtypeStruct(n0.shape, F32), jax.ShapeDtypeStruct(m0.shape, F32)],
        compiler_params=_params(1),
        name="mlstm_step",
    )(q, k, v, g8, C0, n0, m0)


def _rwkv_body(bb, tb, ty, al_ref, w_ref, be_ref, kt_ref, r_ref, v_ref, s0_ref, y_ref, s1_ref, S_s, t3_s, y_s):
    @pl.when(pl.program_id(1) == 0)
    def _():
        S_s[...] = s0_ref[...]

    npair = B_HEADS // 2
    ntile = bb * npair
    nrow = min(tb, SUBLANES)
    row = lax.broadcasted_iota(jnp.int32, (B_HEAD, LANES), 0)
    lane = lax.broadcasted_iota(jnp.int32, (B_HEAD, LANES), 1)
    eye_pair = row == (lane % B_HEAD)
    wred = _group_matrix(LANES, B_HEAD, 1.0)
    kk = lax.broadcasted_iota(jnp.int32, (nrow * LANES, LANES), 0)
    ll = lax.broadcasted_iota(jnp.int32, (nrow * LANES, LANES), 1)
    sel_head = ((kk % LANES) // B_HEAD) == (ll // B_HEAD)
    sel_off = (ll % B_HEAD) - (kk // LANES)

    def group(t8, carry):
        base = t8 * nrow if isinstance(t8, int) else pl.multiple_of(t8 * nrow, nrow)
        rows = pl.ds(base, nrow)
        for i in range(nrow):
            ri = slice(i, i + 1)
            lhs = []
            for b in range(bb):
                for p in range(npair):
                    sl = pl.ds(p * LANES, LANES)
                    lhs.append((S_s[b, p] * al_ref[b, rows, sl][ri]).astype(BF16))
            for b in range(bb):
                for p in range(npair):
                    sl = pl.ds(p * LANES, LANES)
                    lhs.append(jnp.where(eye_pair, v_ref[b, rows, sl][ri], 0.0).astype(BF16))
            red = jnp.dot(jnp.concatenate(lhs, axis=0), wred, preferred_element_type=F32)
            for b in range(bb):
                for p in range(npair):
                    sl = pl.ds(p * LANES, LANES)
                    idx = b * npair + p
                    sa = red[idx * B_HEAD:(idx + 1) * B_HEAD, :]
                    vcol = red[(ntile + idx) * B_HEAD:(ntile + idx + 1) * B_HEAD, :]
                    S = (S_s[b, p] * w_ref[b, rows, sl][ri] + sa * be_ref[b, rows, sl][ri]
                         + vcol * kt_ref[b, rows, sl][ri])
                    S_s[b, p] = S
                    t3_s[idx * B_HEAD:(idx + 1) * B_HEAD, i * LANES:(i + 1) * LANES] = (
                        S * r_ref[b, rows, sl][ri]).astype(BF16)
        wsel = jnp.where(sel_head & (sel_off == base % ty), 1.0, 0.0).astype(BF16)
        yg = jnp.dot(t3_s[...], wsel, preferred_element_type=F32)

        @pl.when(base % ty == 0)
        def _():
            y_s[...] = yg

        @pl.when(base % ty != 0)
        def _():
            y_s[...] += yg

        @pl.when((base + nrow) % ty == 0)
        def _():
            y_ref[0, base // ty] = y_s[...]
        return carry

    if tb == nrow:
        group(0, 0)
    else:
        lax.fori_loop(0, tb // nrow, group, 0)

    @pl.when(pl.program_id(1) == pl.num_programs(1) - 1)
    def _():
        s1_ref[...] = S_s[...]


def _rwkv_scan(al, w, be, kt, r, v, S0p, bb, tb):
    b, t, bw = al.shape
    npair = B_HEADS // 2
    ty = min(B_HEAD, t)
    nb = b // bb
    tok = pl.BlockSpec((bb, tb, bw), lambda i, j: (i, j, 0))
    st = pl.BlockSpec((bb, npair, B_HEAD, LANES), lambda i, j: (i, 0, 0, 0))
    nrow = min(tb, SUBLANES)
    rows = bb * npair * B_HEAD
    yt, S1p = pl.pallas_call(
        functools.partial(_rwkv_body, bb, tb, ty),
        grid=(nb, t // tb),
        in_specs=[tok] * 6 + [st],
        out_specs=[pl.BlockSpec((1, tb // ty, rows, LANES), lambda i, j: (i, j, 0, 0)), st],
        out_shape=[jax.ShapeDtypeStruct((nb, t // ty, rows, LANES), F32), jax.ShapeDtypeStruct(S0p.shape, F32)],
        scratch_shapes=[pltpu.VMEM((bb, npair, B_HEAD, LANES), F32), pltpu.VMEM((rows, nrow * LANES), BF16),
                        pltpu.VMEM((rows, LANES), F32)],
        compiler_params=_params(2),
        name="rwkv_scan",
    )(al, w, be, kt, r, v, S0p)
    y = yt.reshape(nb, t // ty, bb, npair, B_HEAD, 2, B_HEAD)[..., :ty]
    y = y.transpose(0, 2, 1, 6, 3, 5, 4).reshape(b, t, bw)
    return y, S1p


def _even_post_body(d, aw, bw, x_ref, mod_ref, ha_ref, so_ref, y_ref, r_ref, kt_ref, vb_ref, gg_ref,
                    ang_ref, lng_ref, lnb_ref, rk_ref, wo_ref, o_ref):
    ga = _group_matrix(aw, A_DK, 1.0 / A_DK)
    gb_mean = _group_matrix(bw, B_HEAD, 1.0 / B_HEAD)
    gb_sum = _group_matrix(bw, B_HEAD, 1.0)
    ha = ha_ref[0]
    ha = ha * lax.rsqrt(_dot_split(ha * ha, ga, 2) + EPS) * ang_ref[...] * so_ref[0]
    y = y_ref[0]
    yc = y - _dot_split(y, gb_mean, 2)
    var = _dot_split(yc * yc, gb_mean, 2)
    yn = yc * lax.rsqrt(var + B_LN_EPS) * lng_ref[...] + lnb_ref[...]
    bonus = _dot_split(r_ref[0] * kt_ref[0] * rk_ref[...], gb_sum, 2)
    y2 = (yn + bonus * vb_ref[0]) * gg_ref[0]
    out = (jnp.dot(ha.astype(BF16), wo_ref[0:aw, :], preferred_element_type=F32)
           + jnp.dot(y2.astype(BF16), wo_ref[aw:aw + bw, :], preferred_element_type=F32))
    o_ref[0] = x_ref[0] + mod_ref[0][:, 2 * d:3 * d] * out


def _even_post(x3, mod3, ha, so, y, r, kt, vb, gg, ang, lng, lnb, rk, wo, tt):
    g_, ttot, d = x3.shape
    aw = d // 2
    bw = d // 2
    consts = [ang, lng, lnb, rk, wo]
    return pl.pallas_call(
        functools.partial(_even_post_body, d, aw, bw),
        grid=(g_, ttot // tt),
        in_specs=[_tok_spec(tt, d), _mod_spec(mod3, tt)] + [_tok_spec(tt, aw)] * 7 + [_full_spec(c) for c in consts],
        out_specs=_tok_spec(tt, d),
        out_shape=jax.ShapeDtypeStruct(x3.shape, F32),
        compiler_params=_params(2),
        name="even_post",
    )(x3, mod3, ha, so, y, r, kt, vb, gg, *consts)


def _odd_pre_body(j, d, x_ref, mod_ref, ng_ref, w_ref, lbl_ref, q_ref, k_ref, v_ref, lf_ref, sg_ref):
    x = x_ref[0]
    mod = mod_ref[0]
    h = _rmsnorm(x, ng_ref[...]) * (1.0 + mod[:, d:2 * d]) + mod[:, 0:d]
    p = jnp.dot(h.astype(BF16), w_ref[...], preferred_element_type=F32)
    lbl = lbl_ref[...]
    n_odd = lbl.shape[0]
    mx = lbl[0:1, :]
    for i in range(1, n_odd):
        mx = jnp.maximum(mx, lbl[i:i + 1, :])
    ex = [jnp.exp(lbl[i:i + 1, :] - mx) for i in range(n_odd)]
    tot = ex[0]
    for i in range(1, n_odd):
        tot = tot + ex[i]
    lb = jnp.zeros_like(mx)
    for i in range(1, j + 1):
        lb = lb + ex[i] / tot
    f = lb + (1.0 - lb) * _sigmoid(p[:, d:2 * d])
    q_ref[0] = _silu(p[:, 0:d])
    k_ref[0] = 1.0 - f
    v_ref[0] = p[:, 2 * d:3 * d]
    lf_ref[0] = jnp.log(f)
    sg_ref[0] = _silu(p[:, 3 * d:4 * d])


def _odd_pre(x3, mod3, ng, w_in, lbl, j, tt):
    g_, ttot, d = x3.shape
    consts = [ng, w_in, lbl]
    return pl.pallas_call(
        functools.partial(_odd_pre_body, j, d),
        grid=(g_, ttot // tt),
        in_specs=[_tok_spec(tt, d), _mod_spec(mod3, tt)] + [_full_spec(c) for c in consts],
        out_specs=[_tok_spec(tt, d)] * 5,
        out_shape=[jax.ShapeDtypeStruct(x3.shape, F32)] * 5,
        compiler_params=_params(2),
        name="odd_pre",
    )(x3, mod3, *consts)


def _col_bcast(row, terms):
    parts = _split(row, terms)
    r = lax.broadcasted_iota(jnp.int32, (SUBLANES, row.shape[1]), 0)
    lhs = jnp.zeros((SUBLANES, row.shape[1]), F32)
    for i, p in enumerate(parts):
        lhs = jnp.where(r == i, jnp.broadcast_to(p.astype(F32), lhs.shape), lhs)
    ones = jnp.where(lax.broadcasted_iota(jnp.int32, (SUBLANES, LANES), 0) < terms, 1.0, 0.0).astype(BF16)
    return lax.dot_general(lhs.astype(BF16), ones, (((0,), (0,)), ((), ())), preferred_element_type=F32)


def _hgrn_chunk_body(L, sub, nchunk, q_ref, k_ref, v_ref, g_ref, s0_ref, o_ref, s1_ref, S_s):
    @pl.when(pl.program_id(2) == 0)
    def _():
        S_s[...] = s0_ref[0, 0]

    nj = L // sub
    rowi = lax.broadcasted_iota(jnp.int32, (L, C_DK), 0)
    rr = lax.broadcasted_iota(jnp.int32, (L, L), 0)
    cc = lax.broadcasted_iota(jnp.int32, (L, L), 1)
    tril = jnp.where(cc <= rr, 1.0, 0.0).astype(BF16)
    ones = jnp.ones((C_DK, LANES), BF16)

    def chunk(ci, carry):
        base = pl.multiple_of(ci * L, L)
        q = q_ref[0, pl.ds(base, L), :]
        k = k_ref[0, pl.ds(base, L), :]
        v = v_ref[0, pl.ds(base, L), :]
        g = g_ref[0, pl.ds(base, L), :]
        b = _dot_split_left(tril, g, 3)
        S = S_s[...]
        o = _bdot(q * jnp.exp(b), S)
        for dlt in range(sub):
            if dlt == 0:
                pr = q * k
                vs = v
            else:
                ok = (rowi % sub) >= dlt
                e = jnp.exp(jnp.where(ok, b - pltpu.roll(b, dlt, 0), NEG_INF))
                pr = q * pltpu.roll(k, dlt, 0) * e
                vs = pltpu.roll(v, dlt, 0)
            o = o + _bdot(pr, ones) * vs
        if nj > 1:
            amat = None
            for jj in range(nj - 1):
                end = (jj + 1) * sub
                bref = b[end - 1:end, :]
                qj = q * jnp.exp(jnp.where(rowi >= end, b - bref, NEG_INF))
                kj = k * jnp.exp(jnp.where((rowi >= end - sub) & (rowi < end), bref - b, NEG_INF))
                a = _dot_nt(qj, kj)
                amat = a if amat is None else amat + a
            o = o + _bdot(amat, v)
        o_ref[0, pl.ds(base, L), :] = o
        b_last = b[L - 1:L, :]
        S_s[...] = _col_bcast(jnp.exp(b_last), 3) * S + _dot_tn(k * jnp.exp(b_last - b), v)
        return carry

    lax.fori_loop(0, nchunk, chunk, 0)

    @pl.when(pl.program_id(2) == pl.num_programs(2) - 1)
    def _():
        s1_ref[0, 0] = S_s[...]


def _hgrn_chunk(q, k, v, lf, S0, tblk, L, sub):
    b, t, d = q.shape
    tok = pl.BlockSpec((1, tblk, C_DK), lambda bb, h, c: (bb, c, h))
    st = pl.BlockSpec((1, 1, C_DK, C_DK), lambda bb, h, c: (bb, h, 0, 0))
    return pl.pallas_call(
        functools.partial(_hgrn_chunk_body, L, sub, tblk // L),
        grid=(b, C_HEADS, t // tblk),
        in_specs=[tok] * 4 + [st],
        out_specs=[tok, st],
        out_shape=[jax.ShapeDtypeStruct((b, t, d), F32), jax.ShapeDtypeStruct(S0.shape, F32)],
        scratch_shapes=[pltpu.VMEM((C_DK, C_DK), F32)],
        compiler_params=_params(3),
        name="hgrn_chunk",
    )(q, k, v, lf, S0)


def _hgrn_step_body(bb, q_ref, k_ref, v_ref, g_ref, s0_ref, o_ref, s1_ref):
    for b in range(bb):
        for hd in range(C_HEADS):
            sl = slice(hd * C_DK, (hd + 1) * C_DK)
            q = q_ref[b:b + 1, sl]
            k = k_ref[b:b + 1, sl]
            v = v_ref[b:b + 1, sl]
            f = jnp.exp(g_ref[b:b + 1, sl])
            S1 = _col_bcast(f, 3) * s0_ref[b, hd] + _dot_tn(_row8(k), _row8(v))
            o_ref[b:b + 1, sl] = _bdot(_row8(q), S1)[0:1, :]
            s1_ref[b, hd] = S1


def _hgrn_step(q, k, v, lf, S0, bb):
    b, d = q.shape
    row = pl.BlockSpec((bb, d), lambda i: (i, 0))
    st = pl.BlockSpec((bb, C_HEADS, C_DK, C_DK), lambda i: (i, 0, 0, 0))
    return pl.pallas_call(
        functools.partial(_hgrn_step_body, bb),
        grid=(b // bb,),
        in_specs=[row] * 4 + [st],
        out_specs=[row, st],
        out_shape=[jax.ShapeDtypeStruct((b, d), F32), jax.ShapeDtypeStruct(S0.shape, F32)],
        compiler_params=_params(1),
        name="hgrn_step",
    )(q, k, v, lf, S0)


def _odd_post_body(d, x_ref, mod_ref, o_ref, sg_ref, ng_ref, wo_ref, out_ref):
    gm = _group_matrix(d, C_DK, 1.0 / C_DK)
    o = o_ref[0]
    o = o * lax.rsqrt(_dot_split(o * o, gm, 2) + EPS) * ng_ref[...] * sg_ref[0]
    out = jnp.dot(o.astype(BF16), wo_ref[...], preferred_element_type=F32)
    out_ref[0] = x_ref[0] + mod_ref[0][:, 2 * d:3 * d] * out


def _odd_post(x3, mod3, o, sg, ng_tiled, wo, tt):
    g_, ttot, d = x3.shape
    consts = [ng_tiled, wo]
    return pl.pallas_call(
        functools.partial(_odd_post_body, d),
        grid=(g_, ttot // tt),
        in_specs=[_tok_spec(tt, d), _mod_spec(mod3, tt)] + [_tok_spec(tt, d)] * 2 + [_full_spec(c) for c in consts],
        out_specs=_tok_spec(tt, d),
        out_shape=jax.ShapeDtypeStruct(x3.shape, F32),
        compiler_params=_params(2),
        name="odd_post",
    )(x3, mod3, o, sg, *consts)


def _top_sorted(s, count):
    ridx = lax.broadcasted_iota(jnp.int32, (count, s.shape[1]), 0)
    acc = jnp.zeros((count, s.shape[1]), F32)
    cur = s
    for i in range(count):
        m = jnp.max(cur, axis=0, keepdims=True)
        acc = jnp.where(ridx == i, m, acc)
        cur = jnp.where(cur == m, NEG_INF, cur)
    return acc


def _peer_body(tm, te, d, x_ref, mod_ref, ng_ref, wqt_ref, keys_ref, u_ref, vt_ref, o_ref,
               ht_s, s1_s, e1_s, s2_s, e2_s, tau_s, act_s, w_s, acc_s):
    j = pl.program_id(2)
    dk2 = keys_ref.shape[3]
    nsub = te // P_NKEYS

    @pl.when(j == 0)
    def _():
        x = x_ref[0]
        mod = mod_ref[0]
        h = _rmsnorm(x, ng_ref[...]) * (1.0 + mod[:, 4 * d:5 * d]) + mod[:, 3 * d:4 * d]
        ht = h.T.astype(BF16)
        ht_s[...] = ht
        qt = jnp.dot(wqt_ref[...], ht, preferred_element_type=F32)
        r8 = lax.broadcasted_iota(jnp.int32, (SUBLANES, tm), 0)
        for hd in range(P_HEADS):
            s1 = _bdot(keys_ref[hd, 0], qt[(2 * hd) * dk2:(2 * hd + 1) * dk2, :])
            s2 = _bdot(keys_ref[hd, 1], qt[(2 * hd + 1) * dk2:(2 * hd + 2) * dk2, :])
            a = _top_sorted(s1, P_TOPK)
            b = _top_sorted(s2, P_TOPK)
            blocks = [a[0:1, :] + b, a[1:2, :] + b[0:SUBLANES, :]]
            for i in range(3, SUBLANES + 1):
                blocks.append(jnp.where(r8 < P_TOPK // i, a[i - 1:i, :] + b[0:SUBLANES, :], NEG_INF))
            blocks.append(a[SUBLANES:2 * SUBLANES, :] + b[0:1, :])
            cand = jnp.concatenate(blocks, axis=0)
            cur = cand
            tau = None
            for _ in range(P_TOPK):
                tau = jnp.max(cur, axis=0, keepdims=True)
                cur = jnp.where(cur == tau, NEG_INF, cur)
            top = a[0:1, :] + b[0:1, :]
            z = jnp.sum(jnp.where(cand >= tau, jnp.exp(cand - top), 0.0), axis=0, keepdims=True)
            s1_s[hd] = s1
            e1_s[hd] = jnp.exp(s1 - a[0:1, :])
            s2_s[hd] = s2
            e2_s[hd] = jnp.exp(s2 - b[0:1, :]) / z
            tau_s[hd:hd + 1, :] = tau
        acc_s[...] = jnp.zeros_like(acc_s)

    act_s[...] = _gelu_tanh(jnp.dot(u_ref[0], ht_s[...], preferred_element_type=F32))

    assert nsub == SUBLANES
    grp = pl.ds(pl.multiple_of(j * nsub, nsub), nsub)
    for tc in range(tm // LANES):
        ls = slice(tc * LANES, (tc + 1) * LANES)
        s1g = [s1_s[hd, grp, ls] for hd in range(P_HEADS)]
        e1g = [e1_s[hd, grp, ls] for hd in range(P_HEADS)]
        for ii in range(nsub):
            rows = slice(ii * P_NKEYS, (ii + 1) * P_NKEYS)
            gate = jnp.zeros((P_NKEYS, LANES), F32)
            for hd in range(P_HEADS):
                keep = (s1g[hd][ii:ii + 1, :] + s2_s[hd, :, ls]) >= tau_s[hd:hd + 1, ls]
                gate = gate + jnp.where(keep, e2_s[hd, :, ls], 0.0) * e1g[hd][ii:ii + 1, :]
            w_s[rows, ls] = (gate * act_s[rows, ls]).astype(BF16)
    acc_s[...] += jnp.dot(vt_ref[0], w_s[...], preferred_element_type=F32)

    @pl.when(j == pl.num_programs(2) - 1)
    def _():
        o_ref[0] = x_ref[0] + mod_ref[0][:, 5 * d:6 * d] * acc_s[...].T


def _peer(x3, mod3, ng, wqt, keys_bf, u_bf, vt_bf, tm, te):
    g_, ttot, d = x3.shape
    ne = u_bf.shape[0] // te
    nk = P_NKEYS
    if mod3.shape[1] == 1:
        mod_spec = pl.BlockSpec((1, 1, mod3.shape[2]), lambda g, i, j: (g, 0, 0))
    else:
        mod_spec = pl.BlockSpec((1, tm, mod3.shape[2]), lambda g, i, j: (g, i, 0))
    full = lambda arr: pl.BlockSpec(arr.shape, lambda g, i, j: (0,) * arr.ndim)
    u3 = u_bf.reshape(ne, te, d)
    vt3 = vt_bf.reshape(d, ne, te).transpose(1, 0, 2)
    return pl.pallas_call(
        functools.partial(_peer_body, tm, te, d),
        grid=(g_, ttot // tm, ne),
        in_specs=[pl.BlockSpec((1, tm, d), lambda g, i, j: (g, i, 0)), mod_spec, full(ng), full(wqt), full(keys_bf),
                  pl.BlockSpec((1, te, d), lambda g, i, j: (j, 0, 0)),
                  pl.BlockSpec((1, d, te), lambda g, i, j: (j, 0, 0))],
        out_specs=pl.BlockSpec((1, tm, d), lambda g, i, j: (g, i, 0)),
        out_shape=jax.ShapeDtypeStruct(x3.shape, F32),
        scratch_shapes=[pltpu.VMEM((d, tm), BF16),
                        pltpu.VMEM((P_HEADS, nk, tm), F32), pltpu.VMEM((P_HEADS, nk, tm), F32),
                        pltpu.VMEM((P_HEADS, nk, tm), F32), pltpu.VMEM((P_HEADS, nk, tm), F32),
                        pltpu.VMEM((P_HEADS, tm), F32),
                        pltpu.VMEM((te, tm), F32), pltpu.VMEM((te, tm), BF16), pltpu.VMEM((d, tm), F32)],
        compiler_params=_params(3),
        name="peer",
    )(x3, mod3, ng, wqt, keys_bf, u3, vt3)


def _final_norm_body(x_ref, g_ref, o_ref):
    o_ref[0] = _rmsnorm(x_ref[0], g_ref[...])


def _final_norm(x3, g, tt):
    g_, ttot, d = x3.shape
    return pl.pallas_call(
        _final_norm_body,
        grid=(g_, ttot // tt),
        in_specs=[_tok_spec(tt, d), _full_spec(g)],
        out_specs=_tok_spec(tt, d),
        out_shape=jax.ShapeDtypeStruct(x3.shape, F32),
        compiler_params=_params(2),
        name="final_norm",
    )(x3, g)


def _pair_state(S):
    b = S.shape[0]
    return S.reshape(b, B_HEADS // 2, 2, B_HEAD, B_HEAD).transpose(0, 1, 3, 2, 4).reshape(b, B_HEADS // 2, B_HEAD, LANES)


def _unpair_state(Sp):
    b = Sp.shape[0]
    return Sp.reshape(b, B_HEADS // 2, B_HEAD, 2, B_HEAD).transpose(0, 1, 3, 2, 4).reshape(b, B_HEADS, B_HEAD, B_HEAD)


def _trunk(x, mod_all, st, w, seq):
    bsz, t, d = x.shape
    depth = mod_all.shape[0]
    aw = d // 2
    bw = d // 2
    if seq:
        x3 = x
        tt = min(256, t)
        tm = min(512, t)
        mods = [mod_all[l][:, None, :] for l in range(depth)]
    else:
        x3 = x.reshape(1, bsz * t, d)
        tt = bsz * t
        tm = bsz * t
        mods = [mod_all[l][None] for l in range(depth)]
    row = lambda v_: v_.reshape(1, -1)
    new_even, new_odd = [], []
    for l in range(depth):
        j = l // 2
        if l % 2 == 0:
            e = w["even"][j]
            if seq:
                prev = None
            else:
                conv0, shift0 = st[3][j], st[5][j]
                prev = (conv0[:, 0][None], conv0[:, 1][None], conv0[:, 2][None], shift0[:, 0][None])
            (q, k, v, so, g8, g8t, r, dec, kt, vb, al, be, gg, ctail, stail) = _even_pre(
                x3, mods[l], row(w["norm_mix_g"][l]), e["wcat"], e["conv_w"], row(e["conv_b"]), e["gate_b"],
                row(e["mu"]), row(e["w0"]), row(e["a0"]), e["wla"], e["g2"], row(e["k_k"]), row(e["k_a"]), prev, tt)
            if seq:
                C0 = jnp.zeros((bsz, A_HEADS, A_DK, A_DK), F32)
                nm0 = jnp.zeros((bsz, SUBLANES, LANES), F32)
                ha, C1, n1p, m1p = _mlstm_chunk(q, k, v, g8, g8t, C0, nm0, nm0, min(256, t))
                n1 = n1p[:, :A_HEADS, :]
                m1 = m1p[:, :A_HEADS, 0]
                S0p = jnp.zeros((bsz, B_HEADS // 2, B_HEAD, LANES), F32)
                y, S1p = _rwkv_scan(al, dec, be, kt, r, vb, S0p, min(8, bsz), min(128, t))
                conv1 = ctail[:, SUBLANES - (A_CONV - 1):, :]
                shift1 = stail[:, SUBLANES - 1:, :]
            else:
                ha2, C1, n1, m1 = _mlstm_step(q[0], k[0], v[0], g8[0], st[0][j], st[1][j], st[2][j], 8)
                ha = ha2[None]
                tok = lambda a_: a_[0][:, None, :]
                y2, S1p = _rwkv_scan(tok(al), tok(dec), tok(be), tok(kt), tok(r), tok(vb), _pair_state(st[4][j]), 8, 1)
                y = y2.reshape(1, bsz, bw)
                conv1 = jnp.concatenate([st[3][j][:, 1:], ctail[0][:, None, :]], axis=1)
                shift1 = stail[0][:, None, :]
            x3 = _even_post(x3, mods[l], ha, so, y, r, kt, vb, gg, row(e["a_norm_g"]), row(e["ln_g"]),
                            row(e["ln_b"]), row(e["r_k"]), e["w_out"], tt)
            new_even.append((C1, n1, m1, conv1, _unpair_state(S1p), shift1))
        else:
            o_ = w["odd"][j]
            q, k, v, lf, sg = _odd_pre(x3, mods[l], row(w["norm_mix_g"][l]), o_["w_in"], w["lb_logits"], j, tt)
            if seq:
                S0 = jnp.zeros((bsz, C_HEADS, C_DK, C_DK), F32)
                o, S1 = _hgrn_chunk(q, k, v, lf, S0, min(256, t), min(64, t), min(16, t))
            else:
                o2, S1 = _hgrn_step(q[0], k[0], v[0], lf[0], st[6][j], 8)
                o = o2[None]
            x3 = _odd_post(x3, mods[l], o, sg, row(jnp.tile(o_["norm_g"], C_HEADS)), o_["w_out"], tt)
            new_odd.append(S1)
        pw = w["peer"][l]
        x3 = _peer(x3, mods[l], row(w["norm_ffn_g"][l]), pw["wqt"], pw["keys"], pw["u"], pw["vt"], tm, 1024)
    y = _final_norm(x3, row(w["norm_final_g"]), tt).reshape(bsz, t, d)
    ev = [jnp.stack([s[i] for s in new_even]) for i in range(6)]
    return (y, ev[0], ev[1], ev[2], ev[3], ev[4], ev[5], jnp.stack(new_odd))


def _prepare_weights(norm_mix_g, norm_ffn_g, norm_final_g, even_w_in, even_w_out, mlstm_conv_w, mlstm_conv_b,
                     mlstm_gate_b, mlstm_norm_g, rwkv_mu, rwkv_w0, rwkv_w2, rwkv_a0, rwkv_a2, rwkv_g2, rwkv_k_k,
                     rwkv_k_a, rwkv_r_k, rwkv_ln_g, rwkv_ln_b, odd_w_in, odd_w_out, hgrn_lb_logits, hgrn_norm_g,
                     peer_w_q, peer_keys, peer_u, peer_v):
    d = even_w_in.shape[1]
    aw = d // 2
    bw = d // 2
    sw = 3 * bw + B_LORA_W + B_LORA_A + B_LORA_G
    even = []
    for j in range(even_w_in.shape[0]):
        wi = even_w_in[j]
        gates_w = wi[:, 4 * aw:4 * aw + 2 * A_HEADS]
        wcat = jnp.concatenate([wi[:, 0:4 * aw], wi[:, 4 * aw + 2 * A_HEADS:],
                                jnp.pad(gates_w, ((0, 0), (0, LANES - 2 * A_HEADS)))], axis=1).astype(BF16)
        assert wcat.shape[1] == 4 * aw + sw + LANES
        wla = jnp.zeros((B_LORA_W + B_LORA_A, 2 * bw), F32)
        wla = wla.at[:B_LORA_W, :bw].set(rwkv_w2[j]).at[B_LORA_W:, bw:].set(rwkv_a2[j]).astype(BF16)
        even.append(dict(
            wcat=wcat, conv_w=mlstm_conv_w[j], conv_b=mlstm_conv_b[j],
            gate_b=jnp.pad(mlstm_gate_b[j], (0, LANES - 2 * A_HEADS)).reshape(1, LANES),
            mu=rwkv_mu[j], w0=rwkv_w0[j], a0=rwkv_a0[j], wla=wla, g2=rwkv_g2[j].astype(BF16),
            k_k=rwkv_k_k[j], k_a=rwkv_k_a[j], r_k=rwkv_r_k[j], ln_g=rwkv_ln_g[j], ln_b=rwkv_ln_b[j],
            a_norm_g=mlstm_norm_g[j], w_out=even_w_out[j].astype(BF16)))
    odd = [dict(w_in=odd_w_in[j].astype(BF16), w_out=odd_w_out[j].astype(BF16), norm_g=hgrn_norm_g[j])
           for j in range(odd_w_in.shape[0])]
    peer = [dict(wqt=peer_w_q[l].T.astype(BF16), keys=peer_keys[l].astype(BF16), u=peer_u[l].astype(BF16),
                 vt=peer_v[l].T.astype(BF16)) for l in range(peer_w_q.shape[0])]
    return dict(norm_mix_g=norm_mix_g, norm_ffn_g=norm_ffn_g, norm_final_g=norm_final_g, even=even, odd=odd,
                peer=peer, lb_logits=hgrn_lb_logits)


def kernel(x_prompt, x_sample, c_prompt, c_sample, state_mlstm_C, state_mlstm_n, state_mlstm_m, state_mlstm_conv, state_rwkv_S, state_rwkv_shift, state_hgrn_S, norm_mix_g, norm_ffn_g, norm_final_g, ada_w, ada_b, even_w_in, even_w_out, mlstm_conv_w, mlstm_conv_b, mlstm_gate_b, mlstm_norm_g, rwkv_mu, rwkv_w0, rwkv_w2, rwkv_a0, rwkv_a2, rwkv_g2, rwkv_k_k, rwkv_k_a, rwkv_r_k, rwkv_ln_g, rwkv_ln_b, odd_w_in, odd_w_out, hgrn_lb_logits, hgrn_norm_g, peer_w_q, peer_keys, peer_u, peer_v):
    w = _prepare_weights(norm_mix_g, norm_ffn_g, norm_final_g, even_w_in, even_w_out, mlstm_conv_w, mlstm_conv_b,
                         mlstm_gate_b, mlstm_norm_g, rwkv_mu, rwkv_w0, rwkv_w2, rwkv_a0, rwkv_a2, rwkv_g2, rwkv_k_k,
                         rwkv_k_a, rwkv_r_k, rwkv_ln_g, rwkv_ln_b, odd_w_in, odd_w_out, hgrn_lb_logits, hgrn_norm_g,
                         peer_w_q, peer_keys, peer_u, peer_v)
    bp = x_prompt.shape[0]
    mod = _ada(jnp.concatenate([c_prompt, c_sample], axis=0), ada_w, ada_b)
    out_p = _trunk(x_prompt, mod[:, :bp], None, w, True)
    st = (state_mlstm_C, state_mlstm_n, state_mlstm_m, state_mlstm_conv, state_rwkv_S, state_rwkv_shift,
          state_hgrn_S)
    out_s = _trunk(x_sample, mod[:, bp:], st, w, False)
    return (out_p[0], out_s[0]) + tuple(out_p[1:]) + tuple(out_s[1:])
```

```python
import functools
import math

import jax
import jax.numpy as jnp
from jax import lax
from jax.experimental import pallas as pl
from jax.experimental.pallas import tpu as pltpu

F32 = jnp.float32
BF16 = jnp.bfloat16

EPS = 1e-6
A_HEADS = 4
A_DK = 128
A_CONV = 4
B_HEADS = 8
B_HEAD = 64
B_LORA_W = 64
B_LORA_A = 64
B_LORA_G = 128
B_LN_EPS = 64e-5
C_HEADS = 8
C_DK = 128
P_HEADS = 8
P_NKEYS = 128
P_TOPK = 16

LANES = 128
SUBLANES = 8
MXU_WIDTH = 256
VMEM_LIMIT = 56 * 1024 * 1024
NEG_INF = float("-inf")


def _bdot(a, b):
    return jnp.dot(a.astype(BF16), b.astype(BF16), preferred_element_type=F32)


def _dot_nt(a, b):
    return lax.dot_general(a.astype(BF16), b.astype(BF16), (((1,), (1,)), ((), ())), preferred_element_type=F32)


def _dot_tn(a, b):
    return lax.dot_general(a.astype(BF16), b.astype(BF16), (((0,), (0,)), ((), ())), preferred_element_type=F32)


def _split(x, terms):
    parts = []
    rem = x
    for _ in range(terms):
        p = rem.astype(BF16)
        parts.append(p)
        rem = rem - p.astype(F32)
    return parts


def _dot_split(a, b_bf, terms):
    out = None
    for p in _split(a, terms):
        d = jnp.dot(p, b_bf, preferred_element_type=F32)
        out = d if out is None else out + d
    return out


def _dot_split_left(a_bf, b, terms):
    out = None
    for p in _split(b, terms):
        d = jnp.dot(a_bf, p, preferred_element_type=F32)
        out = d if out is None else out + d
    return out


def _sigmoid(x):
    return 1.0 / (1.0 + jnp.exp(-x))


def _silu(x):
    return x * _sigmoid(x)


def _softplus(x):
    return jnp.maximum(x, 0.0) + jnp.log(1.0 + jnp.exp(-jnp.abs(x)))


def _gelu_tanh(x):
    return 0.5 * x * (1.0 + jnp.tanh(math.sqrt(2.0 / math.pi) * (x + 0.044715 * (x * x * x))))


def _rmsnorm(x, g):
    return x * lax.rsqrt(jnp.mean(x * x, axis=-1, keepdims=True) + EPS) * g


def _group_matrix(n, group, value):
    r = lax.broadcasted_iota(jnp.int32, (n, n), 0) // group
    c = lax.broadcasted_iota(jnp.int32, (n, n), 1) // group
    return jnp.where(r == c, value, 0.0).astype(BF16)


def _full_spec(arr):
    nd = arr.ndim
    return pl.BlockSpec(arr.shape, lambda *_: (0,) * nd)


def _params(n_grid):
    return pltpu.CompilerParams(dimension_semantics=("arbitrary",) * n_grid, vmem_limit_bytes=VMEM_LIMIT)


def _tok_spec(tt, width):
    return pl.BlockSpec((1, tt, width), lambda g, i: (g, i, 0))


def _mod_spec(mod3, tt):
    width = mod3.shape[2]
    if mod3.shape[1] == 1:
        return pl.BlockSpec((1, 1, width), lambda g, i: (g, 0, 0))
    return pl.BlockSpec((1, tt, width), lambda g, i: (g, i, 0))


def _ada_body(c_ref, w_ref, b_ref, o_ref):
    c = c_ref[...]
    o_ref[0] = _bdot(_silu(c), w_ref[0]) + b_ref[0]


def _ada(c_all, ada_w, ada_b):
    depth, d, n6 = ada_w.shape
    bc = c_all.shape[0]
    tn = n6 // 4
    return pl.pallas_call(
        _ada_body,
        grid=(depth, n6 // tn),
        in_specs=[pl.BlockSpec((bc, d), lambda l, j: (0, 0)),
                  pl.BlockSpec((1, d, tn), lambda l, j: (l, 0, j)),
                  pl.BlockSpec((1, 1, tn), lambda l, j: (l, 0, j))],
        out_specs=pl.BlockSpec((1, bc, tn), lambda l, j: (l, 0, j)),
        out_shape=jax.ShapeDtypeStruct((depth, bc, n6), F32),
        compiler_params=_params(2),
        name="ada_mod",
    )(c_all, ada_w, ada_b.reshape(depth, 1, n6))


def _even_pre_body(seq, tt, d, aw, bw, *refs):
    n_in = 15 if seq else 19
    ins, outs = refs[:n_in], refs[n_in:]
    (x_ref, mod_ref, ng_ref, w_ref, cw_ref, cb_ref, gb_ref, mu_ref, w0_ref, a0_ref, wla_ref, g2_ref,
     kk_ref, ka_ref, gs_ref) = ins[:15]
    (q_ref, k_ref, v_ref, so_ref, g8_ref, g8t_ref, r_ref, dec_ref, kt_ref, vb_ref, al_ref, be_ref, gg_ref,
     ctail_ref, stail_ref) = outs[:15]
    a2 = 2 * aw
    sw = 3 * bw + B_LORA_W + B_LORA_A + B_LORA_G

    x = x_ref[0]
    mod = mod_ref[0]
    h = _rmsnorm(x, ng_ref[...]) * (1.0 + mod[:, d:2 * d]) + mod[:, 0:d]
    p = jnp.dot(h.astype(BF16), w_ref[...], preferred_element_type=F32)
    u = p[:, 0:a2]
    v = p[:, a2:a2 + aw]
    o = p[:, a2 + aw:a2 + 2 * aw]
    pb = p[:, 4 * aw:4 * aw + sw]
    gt = p[:, 4 * aw + sw:4 * aw + sw + LANES]

    if seq:
        ubuf, pbuf = outs[15], outs[16]

        @pl.when(pl.program_id(1) == 0)
        def _():
            ubuf[0:SUBLANES, :] = jnp.zeros((SUBLANES, a2), F32)
            pbuf[0:SUBLANES, :] = jnp.zeros((SUBLANES, sw), F32)

        ubuf[SUBLANES:SUBLANES + tt, :] = u
        pbuf[SUBLANES:SUBLANES + tt, :] = pb
        u1 = ubuf[SUBLANES - 1:SUBLANES - 1 + tt, :]
        u2 = ubuf[SUBLANES - 2:SUBLANES - 2 + tt, :]
        u3 = ubuf[SUBLANES - 3:SUBLANES - 3 + tt, :]
        pprev = pbuf[SUBLANES - 1:SUBLANES - 1 + tt, :]
        ubuf[0:SUBLANES, :] = u[tt - SUBLANES:tt, :]
        pbuf[0:SUBLANES, :] = pb[tt - SUBLANES:tt, :]
        ctail_ref[0] = u[tt - SUBLANES:tt, :]
        stail_ref[0] = pb[tt - SUBLANES:tt, :]
    else:
        u3, u2, u1, pprev = ins[15][0], ins[16][0], ins[17][0], ins[18][0]
        ctail_ref[0] = u
        stail_ref[0] = pb

    cw = cw_ref[...]
    y = cb_ref[...] + u3 * cw[0:1, :] + u2 * cw[1:2, :] + u1 * cw[2:3, :] + u * cw[3:4, :]
    qk = _silu(y)
    q_ref[0] = qk[:, 0:aw] * (A_DK ** -0.5)
    k_ref[0] = qk[:, aw:a2]
    v_ref[0] = v
    so_ref[0] = _sigmoid(o)
    g = gt + gb_ref[...]
    lane = lax.broadcasted_iota(jnp.int32, g.shape, 1)
    gates = jnp.where(lane < A_HEADS, g, -_softplus(-g))
    g8_ref[0] = gates[:, 0:2 * A_HEADS]
    g8t_ref[0] = gates.T[0:2 * A_HEADS, :]

    xb = pb + mu_ref[...] * (pprev - pb)
    r = xb[:, 0:bw]
    kb = xb[:, bw:2 * bw]
    vb = xb[:, 2 * bw:3 * bw]
    la = xb[:, 3 * bw:3 * bw + LANES]
    gl = xb[:, 3 * bw + LANES:3 * bw + 2 * LANES]
    lane2 = lax.broadcasted_iota(jnp.int32, la.shape, 1)
    la_act = jnp.where(lane2 < B_LORA_W, jnp.tanh(la), la)
    lw = jnp.dot(la_act.astype(BF16), wla_ref[...], preferred_element_type=F32)
    wlog = -_softplus(-(w0_ref[...] + lw[:, 0:bw])) - 0.5
    decay = jnp.exp(-jnp.exp(wlog))
    a = _sigmoid(a0_ref[...] + lw[:, bw:2 * bw])
    gg = jnp.dot(_sigmoid(gl).astype(BF16), g2_ref[...], preferred_element_type=F32)
    kk = kb * kk_ref[...]
    ss = _dot_split(kk * kk, gs_ref[...], 2)
    kkn = kk * lax.rsqrt(jnp.maximum(ss, 1e-24))
    r_ref[0] = r
    dec_ref[0] = decay
    kt_ref[0] = kb * (1.0 + (a - 1.0) * ka_ref[...])
    vb_ref[0] = vb
    al_ref[0] = -kkn
    be_ref[0] = kkn * a
    gg_ref[0] = gg


def _even_pre(x3, mod3, ng, wcat, cw, cb, gb, mu, w0, a0, wla, g2, k_k, k_a, prev, tt):
    g_, ttot, d = x3.shape
    aw = d // 2
    bw = d // 2
    a2 = 2 * aw
    sw = 3 * bw + B_LORA_W + B_LORA_A + B_LORA_G
    seq = prev is None
    gsum = _group_matrix(bw, B_HEAD, 1.0)
    consts = [ng, wcat, cw, cb, gb, mu, w0, a0, wla, g2, k_k, k_a, gsum]
    ins = [x3, mod3] + consts
    in_specs = [_tok_spec(tt, d), _mod_spec(mod3, tt)] + [_full_spec(c) for c in consts]
    if not seq:
        ins += list(prev)
        in_specs += [_tok_spec(tt, a2)] * 3 + [_tok_spec(tt, sw)]
    nt = ttot // tt
    tail_rows = SUBLANES if seq else tt
    tail_tot = g_ * SUBLANES if seq else ttot

    def tail_spec(width):
        if seq:
            return pl.BlockSpec((1, SUBLANES, width), lambda g, i: (g, 0, 0))
        return _tok_spec(tt, width)

    def tail_shape(width):
        if seq:
            return jax.ShapeDtypeStruct((g_, SUBLANES, width), F32)
        return jax.ShapeDtypeStruct((g_, ttot, width), F32)

    tok = lambda w: jax.ShapeDtypeStruct((g_, ttot, w), F32)
    out_shape = [tok(aw), tok(aw), tok(aw), tok(aw), tok(2 * A_HEADS),
                 jax.ShapeDtypeStruct((g_, 2 * A_HEADS, ttot), F32),
                 tok(bw), tok(bw), tok(bw), tok(bw), tok(bw), tok(bw), tok(bw),
                 tail_shape(a2), tail_shape(sw)]
    out_specs = [_tok_spec(tt, aw)] * 4 + [_tok_spec(tt, 2 * A_HEADS),
                                           pl.BlockSpec((1, 2 * A_HEADS, tt), lambda g, i: (g, 0, i))]
    out_specs += [_tok_spec(tt, bw)] * 7 + [tail_spec(a2), tail_spec(sw)]
    scratch = [pltpu.VMEM((tt + SUBLANES, a2), F32), pltpu.VMEM((tt + SUBLANES, sw), F32)] if seq else []
    del tail_rows, tail_tot
    return pl.pallas_call(
        functools.partial(_even_pre_body, seq, tt, d, aw, bw),
        grid=(g_, nt),
        in_specs=in_specs,
        out_specs=out_specs,
        out_shape=out_shape,
        scratch_shapes=scratch,
        compiler_params=_params(2),
        name="even_pre_seq" if seq else "even_pre_step",
    )(*ins)


def _mlstm_chunk_body(L, q_ref, k_ref, v_ref, g8_ref, g8t_ref, c0_ref, n0_ref, m0_ref,
                      h_ref, c1_ref, n1_ref, m1_ref, C_s, n_s, m_s):
    c = pl.program_id(1)

    @pl.when(c == 0)
    def _():
        C_s[...] = c0_ref[0]
        n_s[...] = n0_ref[0]
        m_s[...] = m0_ref[0]

    row = lax.broadcasted_iota(jnp.int32, (L, L), 0)
    col = lax.broadcasted_iota(jnp.int32, (L, L), 1)
    causal = col <= row
    tril = jnp.where(causal, 1.0, 0.0).astype(BF16)
    triu = jnp.where(row <= col, 1.0, 0.0).astype(BF16)
    g8 = g8_ref[0]
    g8t = g8t_ref[0]
    bcols = _dot_split_left(tril, g8, 3)
    brows = _dot_split(g8t, triu, 3)
    for hd in range(A_HEADS):
        sl = slice(hd * A_DK, (hd + 1) * A_DK)
        q = q_ref[0, :, sl]
        k = k_ref[0, :, sl]
        v = v_ref[0, :, sl]
        i_row = g8t[hd:hd + 1, :]
        i_col = g8[:, hd:hd + 1]
        b_row = brows[A_HEADS + hd:A_HEADS + hd + 1, :]
        b_col = bcols[:, A_HEADS + hd:A_HEADS + hd + 1]
        m_prev = m_s[hd:hd + 1, 0:1]
        C = C_s[hd]
        n = n_s[hd:hd + 1, :]
        dmat = jnp.where(causal, b_col - b_row + i_row, NEG_INF)
        inter = b_col + m_prev
        m_t = jnp.maximum(inter, jnp.max(dmat, axis=1, keepdims=True))
        w_intra = jnp.exp(dmat - m_t)
        w_inter = jnp.exp(inter - m_t)
        s = _dot_nt(q, k) * w_intra
        num = _bdot(s, v) + w_inter * _bdot(q, C)
        den = jnp.sum(s, axis=1, keepdims=True) + w_inter * jnp.sum(q * n, axis=1, keepdims=True)
        h_ref[0, :, sl] = num / jnp.maximum(jnp.abs(den), jnp.exp(-m_t))
        m_new = m_t[L - 1:L, :]
        b_last = b_col[L - 1:L, :]
        kw = k * jnp.exp(b_last - b_col + i_col - m_new)
        carry = jnp.exp(b_last + m_prev - m_new)
        C_s[hd] = carry * C + _dot_tn(kw, v)
        n_s[hd:hd + 1, :] = carry * n + jnp.sum(kw, axis=0, keepdims=True)
        m_s[hd:hd + 1, :] = jnp.broadcast_to(m_new, (1, LANES))

    @pl.when(c == pl.num_programs(1) - 1)
    def _():
        c1_ref[0] = C_s[...]
        n1_ref[0] = n_s[...]
        m1_ref[0] = m_s[...]


def _mlstm_chunk(q, k, v, g8, g8t, C0, n0p, m0p, L):
    b, t, aw = q.shape
    st = lambda *s: pl.BlockSpec((1,) + s, lambda bb, c: (bb,) + (0,) * len(s))
    return pl.pallas_call(
        functools.partial(_mlstm_chunk_body, L),
        grid=(b, t // L),
        in_specs=[_tok_spec(L, aw)] * 3 + [_tok_spec(L, 2 * A_HEADS),
                                           pl.BlockSpec((1, 2 * A_HEADS, L), lambda bb, c: (bb, 0, c)),
                                           st(A_HEADS, A_DK, A_DK), st(SUBLANES, LANES), st(SUBLANES, LANES)],
        out_specs=[_tok_spec(L, aw), st(A_HEADS, A_DK, A_DK), st(SUBLANES, LANES), st(SUBLANES, LANES)],
        out_shape=[jax.ShapeDtypeStruct((b, t, aw), F32), jax.ShapeDtypeStruct(C0.shape, F32),
                   jax.ShapeDtypeStruct(n0p.shape, F32), jax.ShapeDtypeStruct(m0p.shape, F32)],
        scratch_shapes=[pltpu.VMEM((A_HEADS, A_DK, A_DK), F32), pltpu.VMEM((SUBLANES, LANES), F32),
                        pltpu.VMEM((SUBLANES, LANES), F32)],
        compiler_params=_params(2),
        name="mlstm_chunk",
    )(q, k, v, g8, g8t, C0, n0p, m0p)


def _row8(x):
    r = lax.broadcasted_iota(jnp.int32, (SUBLANES, x.shape[1]), 0)
    return jnp.where(r == 0, jnp.broadcast_to(x, (SUBLANES, x.shape[1])), 0.0)


def _mlstm_step_body(bb, q_ref, k_ref, v_ref, g8_ref, c0_ref, n0_ref, m0_ref, h_ref, c1_ref, n1_ref, m1_ref):
    g8 = g8_ref[...]
    m0 = m0_ref[...]
    for hd in range(A_HEADS):
        sl = slice(hd * A_DK, (hd + 1) * A_DK)
        li = g8[:, hd:hd + 1]
        lf = g8[:, A_HEADS + hd:A_HEADS + hd + 1]
        mp = m0[:, hd:hd + 1]
        m1 = jnp.maximum(lf + mp, li)
        wi = jnp.exp(li - m1)
        wf = jnp.exp(lf + mp - m1)
        floor = jnp.exp(-m1)
        for b in range(bb):
            q = q_ref[b:b + 1, sl]
            k = k_ref[b:b + 1, sl]
            v = v_ref[b:b + 1, sl]
            wib = wi[b:b + 1, :]
            wfb = wf[b:b + 1, :]
            C1 = wfb * c0_ref[b, hd] + wib * _dot_tn(_row8(k), _row8(v))
            n1 = wfb * n0_ref[b, hd:hd + 1, :] + wib * k
            num = _bdot(_row8(q), C1)[0:1, :]
            den = jnp.sum(q * n1, axis=1, keepdims=True)
            h_ref[b:b + 1, sl] = num / jnp.maximum(jnp.abs(den), floor[b:b + 1, :])
            c1_ref[b, hd] = C1
            n1_ref[b, hd:hd + 1, :] = n1
        m1_ref[:, hd:hd + 1] = m1


def _mlstm_step(q, k, v, g8, C0, n0, m0, bb):
    b, aw = q.shape
    row = lambda w: pl.BlockSpec((bb, w), lambda i: (i, 0))
    return pl.pallas_call(
        functools.partial(_mlstm_step_body, bb),
        grid=(b // bb,),
        in_specs=[row(aw)] * 3 + [row(2 * A_HEADS),
                                  pl.BlockSpec((bb, A_HEADS, A_DK, A_DK), lambda i: (i, 0, 0, 0)),
                                  pl.BlockSpec((bb, A_HEADS, A_DK), lambda i: (i, 0, 0)), row(A_HEADS)],
        out_specs=[row(aw), pl.BlockSpec((bb, A_HEADS, A_DK, A_DK), lambda i: (i, 0, 0, 0)),
                   pl.BlockSpec((bb, A_HEADS, A_DK), lambda i: (i, 0, 0)), row(A_HEADS)],
        out_shape=[jax.ShapeDtypeStruct((b, aw), F32), jax.ShapeDtypeStruct(C0.shape, F32),
                   jax.ShapeDtypeStruct(n0.shape, F32), jax.ShapeDtypeStruct(m0.shape, F32)],
        compiler_params=_params(1),
        name="mlstm_step",
    )(q, k, v, g8, C0, n0, m0)


def _rwkv_body(bb, tb, ty, al_ref, w_ref, be_ref, kt_ref, r_ref, v_ref, s0_ref, y_ref, s1_ref, S_s, t3_s, y_s):
    @pl.when(pl.program_id(1) == 0)
    def _():
        S_s[...] = s0_ref[...]

    npair = B_HEADS // 2
    ntile = bb * npair
    nrow = min(tb, SUBLANES)
    row = lax.broadcasted_iota(jnp.int32, (B_HEAD, LANES), 0)
    lane = lax.broadcasted_iota(jnp.int32, (B_HEAD, LANES), 1)
    eye_pair = row == (lane % B_HEAD)
    wred = _group_matrix(LANES, B_HEAD, 1.0)
    kk = lax.broadcasted_iota(jnp.int32, (nrow * LANES, LANES), 0)
    ll = lax.broadcasted_iota(jnp.int32, (nrow * LANES, LANES), 1)
    sel_head = ((kk % LANES) // B_HEAD) == (ll // B_HEAD)
    sel_off = (ll % B_HEAD) - (kk // LANES)

    def group(t8, carry):
        base = t8 * nrow if isinstance(t8, int) else pl.multiple_of(t8 * nrow, nrow)
        rows = pl.ds(base, nrow)
        for i in range(nrow):
            ri = slice(i, i + 1)
            lhs = []
            for b in range(bb):
                for p in range(npair):
                    sl = pl.ds(p * LANES, LANES)
                    lhs.append((S_s[b, p] * al_ref[b, rows, sl][ri]).astype(BF16))
            for b in range(bb):
                for p in range(npair):
                    sl = pl.ds(p * LANES, LANES)
                    lhs.append(jnp.where(eye_pair, v_ref[b, rows, sl][ri], 0.0).astype(BF16))
            red = jnp.dot(jnp.concatenate(lhs, axis=0), wred, preferred_element_type=F32)
            for b in range(bb):
                for p in range(npair):
                    sl = pl.ds(p * LANES, LANES)
                    idx = b * npair + p
                    sa = red[idx * B_HEAD:(idx + 1) * B_HEAD, :]
                    vcol = red[(ntile + idx) * B_HEAD:(ntile + idx + 1) * B_HEAD, :]
                    S = (S_s[b, p] * w_ref[b, rows, sl][ri] + sa * be_ref[b, rows, sl][ri]
                         + vcol * kt_ref[b, rows, sl][ri])
                    S_s[b, p] = S
                    t3_s[idx * B_HEAD:(idx + 1) * B_HEAD, i * LANES:(i + 1) * LANES] = (
                        S * r_ref[b, rows, sl][ri]).astype(BF16)
        wsel = jnp.where(sel_head & (sel_off == base % ty), 1.0, 0.0).astype(BF16)
        yg = jnp.dot(t3_s[...], wsel, preferred_element_type=F32)

        @pl.when(base % ty == 0)
        def _():
            y_s[...] = yg

        @pl.when(base % ty != 0)
        def _():
            y_s[...] += yg

        @pl.when((base + nrow) % ty == 0)
        def _():
            y_ref[0, base // ty] = y_s[...]
        return carry

    if tb == nrow:
        group(0, 0)
    else:
        lax.fori_loop(0, tb // nrow, group, 0)

    @pl.when(pl.program_id(1) == pl.num_programs(1) - 1)
    def _():
        s1_ref[...] = S_s[...]


def _rwkv_scan(al, w, be, kt, r, v, S0p, bb, tb):
    b, t, bw = al.shape
    npair = B_HEADS // 2
    ty = min(B_HEAD, t)
    nb = b // bb
    tok = pl.BlockSpec((bb, tb, bw), lambda i, j: (i, j, 0))
    st = pl.BlockSpec((bb, npair, B_HEAD, LANES), lambda i, j: (i, 0, 0, 0))
    nrow = min(tb, SUBLANES)
    rows = bb * npair * B_HEAD
    yt, S1p = pl.pallas_call(
        functools.partial(_rwkv_body, bb, tb, ty),
        grid=(nb, t // tb),
        in_specs=[tok] * 6 + [st],
        out_specs=[pl.BlockSpec((1, tb // ty, rows, LANES), lambda i, j: (i, j, 0, 0)), st],
        out_shape=[jax.ShapeDtypeStruct((nb, t // ty, rows, LANES), F32), jax.ShapeDtypeStruct(S0p.shape, F32)],
        scratch_shapes=[pltpu.VMEM((bb, npair, B_HEAD, LANES), F32), pltpu.VMEM((rows, nrow * LANES), BF16),
                        pltpu.VMEM((rows, LANES), F32)],
        compiler_params=_params(2),
        name="rwkv_scan",
    )(al, w, be, kt, r, v, S0p)
    y = yt.reshape(nb, t // ty, bb, npair, B_HEAD, 2, B_HEAD)[..., :ty]
    y = y.transpose(0, 2, 1, 6, 3, 5, 4).reshape(b, t, bw)
    return y, S1p


def _even_post_body(d, aw, bw, x_ref, mod_ref, ha_ref, so_ref, y_ref, r_ref, kt_ref, vb_ref, gg_ref,
                    ang_ref, lng_ref, lnb_ref, rk_ref, wo_ref, o_ref):
    ga = _group_matrix(aw, A_DK, 1.0 / A_DK)
    gb_mean = _group_matrix(bw, B_HEAD, 1.0 / B_HEAD)
    gb_sum = _group_matrix(bw, B_HEAD, 1.0)
    ha = ha_ref[0]
    ha = ha * lax.rsqrt(_dot_split(ha * ha, ga, 2) + EPS) * ang_ref[...] * so_ref[0]
    y = y_ref[0]
    yc = y - _dot_split(y, gb_mean, 2)
    var = _dot_split(yc * yc, gb_mean, 2)
    yn = yc * lax.rsqrt(var + B_LN_EPS) * lng_ref[...] + lnb_ref[...]
    bonus = _dot_split(r_ref[0] * kt_ref[0] * rk_ref[...], gb_sum, 2)
    y2 = (yn + bonus * vb_ref[0]) * gg_ref[0]
    out = (jnp.dot(ha.astype(BF16), wo_ref[0:aw, :], preferred_element_type=F32)
           + jnp.dot(y2.astype(BF16), wo_ref[aw:aw + bw, :], preferred_element_type=F32))
    o_ref[0] = x_ref[0] + mod_ref[0][:, 2 * d:3 * d] * out


def _even_post(x3, mod3, ha, so, y, r, kt, vb, gg, ang, lng, lnb, rk, wo, tt):
    g_, ttot, d = x3.shape
    aw = d // 2
    bw = d // 2
    consts = [ang, lng, lnb, rk, wo]
    return pl.pallas_call(
        functools.partial(_even_post_body, d, aw, bw),
        grid=(g_, ttot // tt),
        in_specs=[_tok_spec(tt, d), _mod_spec(mod3, tt)] + [_tok_spec(tt, aw)] * 7 + [_full_spec(c) for c in consts],
        out_specs=_tok_spec(tt, d),
        out_shape=jax.ShapeDtypeStruct(x3.shape, F32),
        compiler_params=_params(2),
        name="even_post",
    )(x3, mod3, ha, so, y, r, kt, vb, gg, *consts)


def _odd_pre_body(j, d, x_ref, mod_ref, ng_ref, w_ref, lbl_ref, q_ref, k_ref, v_ref, lf_ref, sg_ref):
    x = x_ref[0]
    mod = mod_ref[0]
    h = _rmsnorm(x, ng_ref[...]) * (1.0 + mod[:, d:2 * d]) + mod[:, 0:d]
    p = jnp.dot(h.astype(BF16), w_ref[...], preferred_element_type=F32)
    lbl = lbl_ref[...]
    n_odd = lbl.shape[0]
    mx = lbl[0:1, :]
    for i in range(1, n_odd):
        mx = jnp.maximum(mx, lbl[i:i + 1, :])
    ex = [jnp.exp(lbl[i:i + 1, :] - mx) for i in range(n_odd)]
    tot = ex[0]
    for i in range(1, n_odd):
        tot = tot + ex[i]
    lb = jnp.zeros_like(mx)
    for i in range(1, j + 1):
        lb = lb + ex[i] / tot
    f = lb + (1.0 - lb) * _sigmoid(p[:, d:2 * d])
    q_ref[0] = _silu(p[:, 0:d])
    k_ref[0] = 1.0 - f
    v_ref[0] = p[:, 2 * d:3 * d]
    lf_ref[0] = jnp.log(f)
    sg_ref[0] = _silu(p[:, 3 * d:4 * d])


def _odd_pre(x3, mod3, ng, w_in, lbl, j, tt):
    g_, ttot, d = x3.shape
    consts = [ng, w_in, lbl]
    return pl.pallas_call(
        functools.partial(_odd_pre_body, j, d),
        grid=(g_, ttot // tt),
        in_specs=[_tok_spec(tt, d), _mod_spec(mod3, tt)] + [_full_spec(c) for c in consts],
        out_specs=[_tok_spec(tt, d)] * 5,
        out_shape=[jax.ShapeDtypeStruct(x3.shape, F32)] * 5,
        compiler_params=_params(2),
        name="odd_pre",
    )(x3, mod3, *consts)


def _col_bcast(row, terms):
    parts = _split(row, terms)
    r = lax.broadcasted_iota(jnp.int32, (SUBLANES, row.shape[1]), 0)
    lhs = jnp.zeros((SUBLANES, row.shape[1]), F32)
    for i, p in enumerate(parts):
        lhs = jnp.where(r == i, jnp.broadcast_to(p.astype(F32), lhs.shape), lhs)
    ones = jnp.where(lax.broadcasted_iota(jnp.int32, (SUBLANES, LANES), 0) < terms, 1.0, 0.0).astype(BF16)
    return lax.dot_general(lhs.astype(BF16), ones, (((0,), (0,)), ((), ())), preferred_element_type=F32)


def _hgrn_chunk_body(L, sub, nchunk, hg, q_ref, k_ref, v_ref, g_ref, s0_ref, o_ref, s1_ref, S_s):
    @pl.when(pl.program_id(2) == 0)
    def _():
        S_s[...] = s0_ref[0]

    nj = L // sub
    rowi = lax.broadcasted_iota(jnp.int32, (L, C_DK), 0)
    rr = lax.broadcasted_iota(jnp.int32, (L, L), 0)
    cc = lax.broadcasted_iota(jnp.int32, (L, L), 1)
    tril = jnp.where(cc <= rr, 1.0, 0.0).astype(BF16)
    ones = jnp.ones((C_DK, LANES), BF16)

    def chunk(ci, carry):
        base = pl.multiple_of(ci * L, L)
        rows = pl.ds(base, L)
        b_all = _dot_split_left(tril, g_ref[0, rows, :], 3)
        heads = []
        for hd in range(hg):
            sl = slice(hd * C_DK, (hd + 1) * C_DK)
            heads.append((q_ref[0, rows, sl], k_ref[0, rows, sl], v_ref[0, rows, sl], b_all[:, sl], S_s[hd]))
        outs = [_bdot(q * jnp.exp(b), S) for (q, k, v, b, S) in heads]
        for dlt in range(sub):
            prs, vss = [], []
            for (q, k, v, b, S) in heads:
                if dlt == 0:
                    prs.append((q * k).astype(BF16))
                    vss.append(v)
                else:
                    ok = (rowi % sub) >= dlt
                    e = jnp.exp(jnp.where(ok, b - pltpu.roll(b, dlt, 0), NEG_INF))
                    prs.append((q * pltpu.roll(k, dlt, 0) * e).astype(BF16))
                    vss.append(pltpu.roll(v, dlt, 0))
            rs = jnp.dot(jnp.concatenate(prs, axis=0), ones, preferred_element_type=F32)
            outs = [o + rs[i * L:(i + 1) * L, :] * vs for i, (o, vs) in enumerate(zip(outs, vss))]
        for hd, ((q, k, v, b, S), o) in enumerate(zip(heads, outs)):
            if nj > 1:
                amat = None
                for jj in range(nj - 1):
                    end = (jj + 1) * sub
                    bref = b[end - 1:end, :]
                    qj = q * jnp.exp(jnp.where(rowi >= end, b - bref, NEG_INF))
                    kj = k * jnp.exp(jnp.where((rowi >= end - sub) & (rowi < end), bref - b, NEG_INF))
                    a = _dot_nt(qj, kj)
                    amat = a if amat is None else amat + a
                o = o + _bdot(amat, v)
            o_ref[0, rows, hd * C_DK:(hd + 1) * C_DK] = o
            b_last = b[L - 1:L, :]
            S_s[hd] = _col_bcast(jnp.exp(b_last), 3) * S + _dot_tn(k * jnp.exp(b_last - b), v)
        return carry

    lax.fori_loop(0, nchunk, chunk, 0)

    @pl.when(pl.program_id(2) == pl.num_programs(2) - 1)
    def _():
        s1_ref[0] = S_s[...]


def _hgrn_chunk(q, k, v, lf, S0, tblk, L, sub, hg):
    b, t, d = q.shape
    tok = pl.BlockSpec((1, tblk, hg * C_DK), lambda bb, h, c: (bb, c, h))
    st = pl.BlockSpec((1, hg, C_DK, C_DK), lambda bb, h, c: (bb, h, 0, 0))
    return pl.pallas_call(
        functools.partial(_hgrn_chunk_body, L, sub, tblk // L, hg),
        grid=(b, C_HEADS // hg, t // tblk),
        in_specs=[tok] * 4 + [st],
        out_specs=[tok, st],
        out_shape=[jax.ShapeDtypeStruct((b, t, d), F32), jax.ShapeDtypeStruct(S0.shape, F32)],
        scratch_shapes=[pltpu.VMEM((hg, C_DK, C_DK), F32)],
        compiler_params=_params(3),
        name="hgrn_chunk",
    )(q, k, v, lf, S0)


def _hgrn_step_body(bb, q_ref, k_ref, v_ref, g_ref, s0_ref, o_ref, s1_ref):
    for b in range(bb):
        for hd in range(C_HEADS):
            sl = slice(hd * C_DK, (hd + 1) * C_DK)
            q = q_ref[b:b + 1, sl]
            k = k_ref[b:b + 1, sl]
            v = v_ref[b:b + 1, sl]
            f = jnp.exp(g_ref[b:b + 1, sl])
            S1 = _col_bcast(f, 3) * s0_ref[b, hd] + _dot_tn(_row8(k), _row8(v))
            o_ref[b:b + 1, sl] = _bdot(_row8(q), S1)[0:1, :]
            s1_ref[b, hd] = S1


def _hgrn_step(q, k, v, lf, S0, bb):
    b, d = q.shape
    row = pl.BlockSpec((bb, d), lambda i: (i, 0))
    st = pl.BlockSpec((bb, C_HEADS, C_DK, C_DK), lambda i: (i, 0, 0, 0))
    return pl.pallas_call(
        functools.partial(_hgrn_step_body, bb),
        grid=(b // bb,),
        in_specs=[row] * 4 + [st],
        out_specs=[row, st],
        out_shape=[jax.ShapeDtypeStruct((b, d), F32), jax.ShapeDtypeStruct(S0.shape, F32)],
        compiler_params=_params(1),
        name="hgrn_step",
    )(q, k, v, lf, S0)


def _odd_post_body(d, x_ref, mod_ref, o_ref, sg_ref, ng_ref, wo_ref, out_ref):
    gm = _group_matrix(d, C_DK, 1.0 / C_DK)
    o = o_ref[0]
    o = o * lax.rsqrt(_dot_split(o * o, gm, 2) + EPS) * ng_ref[...] * sg_ref[0]
    out = jnp.dot(o.astype(BF16), wo_ref[...], preferred_element_type=F32)
    out_ref[0] = x_ref[0] + mod_ref[0][:, 2 * d:3 * d] * out


def _odd_post(x3, mod3, o, sg, ng_tiled, wo, tt):
    g_, ttot, d = x3.shape
    consts = [ng_tiled, wo]
    return pl.pallas_call(
        functools.partial(_odd_post_body, d),
        grid=(g_, ttot // tt),
        in_specs=[_tok_spec(tt, d), _mod_spec(mod3, tt)] + [_tok_spec(tt, d)] * 2 + [_full_spec(c) for c in consts],
        out_specs=_tok_spec(tt, d),
        out_shape=jax.ShapeDtypeStruct(x3.shape, F32),
        compiler_params=_params(2),
        name="odd_post",
    )(x3, mod3, o, sg, *consts)


P_NCAND = P_TOPK + 1
P_CAND_ROWS = 3 * SUBLANES


def _top_sorted(s):
    ridx = lax.broadcasted_iota(jnp.int32, (P_CAND_ROWS, s.shape[1]), 0)
    acc = jnp.full((P_CAND_ROWS, s.shape[1]), NEG_INF, F32)
    cur = s
    for i in range(P_NCAND):
        m = jnp.max(cur, axis=0, keepdims=True)
        acc = jnp.where(ridx == i, m, acc)
        cur = jnp.where(cur == m, NEG_INF, cur)
    return acc


def _peer_body(tm, te, ne, d, x_ref, mod_ref, ng_ref, wqt_ref, keys_ref, u_ref, vt_ref, o_ref,
               ht_s, s1_s, e1_s, thr_s, e2_s, gate_s, w_s, acc_s):
    j = pl.program_id(2)
    dk2 = keys_ref.shape[3]
    nsub = te // P_NKEYS

    @pl.when(j == 0)
    def _():
        x = x_ref[0]
        mod = mod_ref[0]
        h = _rmsnorm(x, ng_ref[...]) * (1.0 + mod[:, 4 * d:5 * d]) + mod[:, 3 * d:4 * d]
        ht = h.T.astype(BF16)
        ht_s[...] = ht
        qt = jnp.dot(wqt_ref[...], ht, preferred_element_type=F32)
        r8 = lax.broadcasted_iota(jnp.int32, (SUBLANES, tm), 0)
        for hd in range(P_HEADS):
            s1 = _bdot(keys_ref[hd, 0], qt[(2 * hd) * dk2:(2 * hd + 1) * dk2, :])
            s2 = _bdot(keys_ref[hd, 1], qt[(2 * hd + 1) * dk2:(2 * hd + 2) * dk2, :])
            a = _top_sorted(s1)
            b = _top_sorted(s2)
            blocks = [a[0:1, :] + b]
            for i in range(2, SUBLANES + 1):
                blocks.append(jnp.where(r8 < P_NCAND // i, a[i - 1:i, :] + b[0:SUBLANES, :], NEG_INF))
            blocks.append(a[SUBLANES:P_CAND_ROWS, :] + b[0:1, :])
            cand = jnp.concatenate(blocks, axis=0)
            cur = cand
            best = None
            for _ in range(P_NCAND):
                prev, best = best, jnp.max(cur, axis=0, keepdims=True)
                cur = jnp.where(cur == best, NEG_INF, cur)
            tau = 0.5 * (prev + best)
            top = a[0:1, :] + b[0:1, :]
            z = jnp.sum(jnp.where(cand >= tau, jnp.exp(cand - top), 0.0), axis=0, keepdims=True)
            s1_s[hd] = s1
            e1_s[hd] = jnp.exp(s1 - a[0:1, :])
            thr_s[hd] = tau - s2
            e2_s[hd] = jnp.exp(s2 - b[0:1, :]) / z
        acc_s[...] = jnp.zeros_like(acc_s)
        w_s[1] = jnp.zeros((te, tm), BF16)

    assert nsub == SUBLANES
    jj = jnp.minimum(j, ne - 1)
    grp = pl.ds(pl.multiple_of(jj * nsub, nsub), nsub)

    def gates(tc):
        ls = slice(tc * LANES, (tc + 1) * LANES)
        s1g = [s1_s[hd, grp, ls] for hd in range(P_HEADS)]
        e1g = [e1_s[hd, grp, ls] for hd in range(P_HEADS)]
        for ii in range(nsub):
            rows = slice(ii * P_NKEYS, (ii + 1) * P_NKEYS)
            gate = jnp.zeros((P_NKEYS, LANES), F32)
            for hd in range(P_HEADS):
                keep = s1g[hd][ii:ii + 1, :] >= thr_s[hd, :, ls]
                gate = gate + jnp.where(keep, e2_s[hd, :, ls], 0.0) * e1g[hd][ii:ii + 1, :]
            gate_s[rows, ls] = gate

    piece = min(tm, MXU_WIDTH)
    tc_per_piece = piece // LANES
    for pc in range(tm // piece):
        cs = slice(pc * piece, (pc + 1) * piece)
        acc_s[:, cs] += jnp.dot(vt_ref[0], w_s[(j + 1) % 2, :, cs], preferred_element_type=F32)
        for tc in range(pc * tc_per_piece, pc * tc_per_piece + (tc_per_piece + 1) // 2):
            gates(tc)
        act = jnp.dot(u_ref[0], ht_s[:, cs], preferred_element_type=F32)
        for tc in range(pc * tc_per_piece + (tc_per_piece + 1) // 2, (pc + 1) * tc_per_piece):
            gates(tc)
        w_s[j % 2, :, cs] = (gate_s[:, cs] * _gelu_tanh(act)).astype(BF16)

    @pl.when(j == ne)
    def _():
        o_ref[0] = x_ref[0] + mod_ref[0][:, 5 * d:6 * d] * acc_s[...].T


def _peer(x3, mod3, ng, wqt, keys_bf, u_bf, vt_bf, tm, te):
    g_, ttot, d = x3.shape
    ne = u_bf.shape[0] // te
    nk = P_NKEYS
    if mod3.shape[1] == 1:
        mod_spec = pl.BlockSpec((1, 1, mod3.shape[2]), lambda g, i, j: (g, 0, 0))
    else:
        mod_spec = pl.BlockSpec((1, tm, mod3.shape[2]), lambda g, i, j: (g, i, 0))
    full = lambda arr: pl.BlockSpec(arr.shape, lambda g, i, j: (0,) * arr.ndim)
    u3 = u_bf.reshape(ne, te, d)
    vt3 = vt_bf.reshape(d, ne, te).transpose(1, 0, 2)
    return pl.pallas_call(
        functools.partial(_peer_body, tm, te, ne, d),
        grid=(g_, ttot // tm, ne + 1),
        in_specs=[pl.BlockSpec((1, tm, d), lambda g, i, j: (g, i, 0)), mod_spec, full(ng), full(wqt), full(keys_bf),
                  pl.BlockSpec((1, te, d), lambda g, i, j: (jnp.minimum(j, ne - 1), 0, 0)),
                  pl.BlockSpec((1, d, te), lambda g, i, j: (jnp.maximum(j - 1, 0), 0, 0))],
        out_specs=pl.BlockSpec((1, tm, d), lambda g, i, j: (g, i, 0)),
        out_shape=jax.ShapeDtypeStruct(x3.shape, F32),
        scratch_shapes=[pltpu.VMEM((d, tm), BF16),
                        pltpu.VMEM((P_HEADS, nk, tm), F32), pltpu.VMEM((P_HEADS, nk, tm), F32),
                        pltpu.VMEM((P_HEADS, nk, tm), F32), pltpu.VMEM((P_HEADS, nk, tm), F32),
                        pltpu.VMEM((te, tm), F32), pltpu.VMEM((2, te, tm), BF16), pltpu.VMEM((d, tm), F32)],
        compiler_params=_params(3),
        name="peer",
    )(x3, mod3, ng, wqt, keys_bf, u3, vt3)


def _final_norm_body(x_ref, g_ref, o_ref):
    o_ref[0] = _rmsnorm(x_ref[0], g_ref[...])


def _final_norm(x3, g, tt):
    g_, ttot, d = x3.shape
    return pl.pallas_call(
        _final_norm_body,
        grid=(g_, ttot // tt),
        in_specs=[_tok_spec(tt, d), _full_spec(g)],
        out_specs=_tok_spec(tt, d),
        out_shape=jax.ShapeDtypeStruct(x3.shape, F32),
        compiler_params=_params(2),
        name="final_norm",
    )(x3, g)


def _pair_state(S):
    b = S.shape[0]
    return S.reshape(b, B_HEADS // 2, 2, B_HEAD, B_HEAD).transpose(0, 1, 3, 2, 4).reshape(b, B_HEADS // 2, B_HEAD, LANES)


def _unpair_state(Sp):
    b = Sp.shape[0]
    return Sp.reshape(b, B_HEADS // 2, B_HEAD, 2, B_HEAD).transpose(0, 1, 3, 2, 4).reshape(b, B_HEADS, B_HEAD, B_HEAD)


def _trunk(x, mod_all, st, w, seq):
    bsz, t, d = x.shape
    depth = mod_all.shape[0]
    aw = d // 2
    bw = d // 2
    if seq:
        x3 = x
        tt = min(256, t)
        tm = min(512, t)
        mods = [mod_all[l][:, None, :] for l in range(depth)]
    else:
        x3 = x.reshape(1, bsz * t, d)
        tt = bsz * t
        tm = bsz * t
        mods = [mod_all[l][None] for l in range(depth)]
    row = lambda v_: v_.reshape(1, -1)
    new_even, new_odd = [], []
    for l in range(depth):
        j = l // 2
        if l % 2 == 0:
            e = w["even"][j]
            if seq:
                prev = None
            else:
                conv0, shift0 = st[3][j], st[5][j]
                prev = (conv0[:, 0][None], conv0[:, 1][None], conv0[:, 2][None], shift0[:, 0][None])
            (q, k, v, so, g8, g8t, r, dec, kt, vb, al, be, gg, ctail, stail) = _even_pre(
                x3, mods[l], row(w["norm_mix_g"][l]), e["wcat"], e["conv_w"], row(e["conv_b"]), e["gate_b"],
                row(e["mu"]), row(e["w0"]), row(e["a0"]), e["wla"], e["g2"], row(e["k_k"]), row(e["k_a"]), prev, tt)
            if seq:
                C0 = jnp.zeros((bsz, A_HEADS, A_DK, A_DK), F32)
                nm0 = jnp.zeros((bsz, SUBLANES, LANES), F32)
                ha, C1, n1p, m1p = _mlstm_chunk(q, k, v, g8, g8t, C0, nm0, nm0, min(256, t))
                n1 = n1p[:, :A_HEADS, :]
                m1 = m1p[:, :A_HEADS, 0]
                S0p = jnp.zeros((bsz, B_HEADS // 2, B_HEAD, LANES), F32)
                y, S1p = _rwkv_scan(al, dec, be, kt, r, vb, S0p, min(8, bsz), min(128, t))
                conv1 = ctail[:, SUBLANES - (A_CONV - 1):, :]
                shift1 = stail[:, SUBLANES - 1:, :]
            else:
                ha2, C1, n1, m1 = _mlstm_step(q[0], k[0], v[0], g8[0], st[0][j], st[1][j], st[2][j], 8)
                ha = ha2[None]
                tok = lambda a_: a_[0][:, None, :]
                y2, S1p = _rwkv_scan(tok(al), tok(dec), tok(be), tok(kt), tok(r), tok(vb), _pair_state(st[4][j]), 8, 1)
                y = y2.reshape(1, bsz, bw)
                conv1 = jnp.concatenate([st[3][j][:, 1:], ctail[0][:, None, :]], axis=1)
                shift1 = stail[0][:, None, :]
            x3 = _even_post(x3, mods[l], ha, so, y, r, kt, vb, gg, row(e["a_norm_g"]), row(e["ln_g"]),
                            row(e["ln_b"]), row(e["r_k"]), e["w_out"], tt)
            new_even.append((C1, n1, m1, conv1, _unpair_state(S1p), shift1))
        else:
            o_ = w["odd"][j]
            q, k, v, lf, sg = _odd_pre(x3, mods[l], row(w["norm_mix_g"][l]), o_["w_in"], w["lb_logits"], j, tt)
            if seq:
                S0 = jnp.zeros((bsz, C_HEADS, C_DK, C_DK), F32)
                o, S1 = _hgrn_chunk(q, k, v, lf, S0, min(256, t), min(64, t), min(16, t), 4)
            else:
                o2, S1 = _hgrn_step(q[0], k[0], v[0], lf[0], st[6][j], 8)
                o = o2[None]
            x3 = _odd_post(x3, mods[l], o, sg, row(jnp.tile(o_["norm_g"], C_HEADS)), o_["w_out"], tt)
            new_odd.append(S1)
        pw = w["peer"][l]
        x3 = _peer(x3, mods[l], row(w["norm_ffn_g"][l]), pw["wqt"], pw["keys"], pw["u"], pw["vt"], tm, 1024)
    y = _final_norm(x3, row(w["norm_final_g"]), tt).reshape(bsz, t, d)
    ev = [jnp.stack([s[i] for s in new_even]) for i in range(6)]
    return (y, ev[0], ev[1], ev[2], ev[3], ev[4], ev[5], jnp.stack(new_odd))


def _prepare_weights(norm_mix_g, norm_ffn_g, norm_final_g, even_w_in, even_w_out, mlstm_conv_w, mlstm_conv_b,
                     mlstm_gate_b, mlstm_norm_g, rwkv_mu, rwkv_w0, rwkv_w2, rwkv_a0, rwkv_a2, rwkv_g2, rwkv_k_k,
                     rwkv_k_a, rwkv_r_k, rwkv_ln_g, rwkv_ln_b, odd_w_in, odd_w_out, hgrn_lb_logits, hgrn_norm_g,
                     peer_w_q, peer_keys, peer_u, peer_v):
    d = even_w_in.shape[1]
    aw = d // 2
    bw = d // 2
    sw = 3 * bw + B_LORA_W + B_LORA_A + B_LORA_G
    even = []
    for j in range(even_w_in.shape[0]):
        wi = even_w_in[j]
        gates_w = wi[:, 4 * aw:4 * aw + 2 * A_HEADS]
        wcat = jnp.concatenate([wi[:, 0:4 * aw], wi[:, 4 * aw + 2 * A_HEADS:],
                                jnp.pad(gates_w, ((0, 0), (0, LANES - 2 * A_HEADS)))], axis=1).astype(BF16)
        assert wcat.shape[1] == 4 * aw + sw + LANES
        wla = jnp.zeros((B_LORA_W + B_LORA_A, 2 * bw), F32)
        wla = wla.at[:B_LORA_W, :bw].set(rwkv_w2[j]).at[B_LORA_W:, bw:].set(rwkv_a2[j]).astype(BF16)
        even.append(dict(
            wcat=wcat, conv_w=mlstm_conv_w[j], conv_b=mlstm_conv_b[j],
            gate_b=jnp.pad(mlstm_gate_b[j], (0, LANES - 2 * A_HEADS)).reshape(1, LANES),
            mu=rwkv_mu[j], w0=rwkv_w0[j], a0=rwkv_a0[j], wla=wla, g2=rwkv_g2[j].astype(BF16),
            k_k=rwkv_k_k[j], k_a=rwkv_k_a[j], r_k=rwkv_r_k[j], ln_g=rwkv_ln_g[j], ln_b=rwkv_ln_b[j],
            a_norm_g=mlstm_norm_g[j], w_out=even_w_out[j].astype(BF16)))
    odd = [dict(w_in=odd_w_in[j].astype(BF16), w_out=odd_w_out[j].astype(BF16), norm_g=hgrn_norm_g[j])
           for j in range(odd_w_in.shape[0])]
    peer = [dict(wqt=peer_w_q[l].T.astype(BF16), keys=peer_keys[l].astype(BF16), u=peer_u[l].astype(BF16),
                 vt=peer_v[l].T.astype(BF16)) for l in range(peer_w_q.shape[0])]
    return dict(norm_mix_g=norm_mix_g, norm_ffn_g=norm_ffn_g, norm_final_g=norm_final_g, even=even, odd=odd,
                peer=peer, lb_logits=hgrn_lb_logits)


def kernel(x_prompt, x_sample, c_prompt, c_sample, state_mlstm_C, state_mlstm_n, state_mlstm_m, state_mlstm_conv, state_rwkv_S, state_rwkv_shift, state_hgrn_S, norm_mix_g, norm_ffn_g, norm_final_g, ada_w, ada_b, even_w_in, even_w_out, mlstm_conv_w, mlstm_conv_b, mlstm_gate_b, mlstm_norm_g, rwkv_mu, rwkv_w0, rwkv_w2, rwkv_a0, rwkv_a2, rwkv_g2, rwkv_k_k, rwkv_k_a, rwkv_r_k, rwkv_ln_g, rwkv_ln_b, odd_w_in, odd_w_out, hgrn_lb_logits, hgrn_norm_g, peer_w_q, peer_keys, peer_u, peer_v):
    w = _prepare_weights(norm_mix_g, norm_ffn_g, norm_final_g, even_w_in, even_w_out, mlstm_conv_w, mlstm_conv_b,
                         mlstm_gate_b, mlstm_norm_g, rwkv_mu, rwkv_w0, rwkv_w2, rwkv_a0, rwkv_a2, rwkv_g2, rwkv_k_k,
                         rwkv_k_a, rwkv_r_k, rwkv_ln_g, rwkv_ln_b, odd_w_in, odd_w_out, hgrn_lb_logits, hgrn_norm_g,
                         peer_w_q, peer_keys, peer_u, peer_v)
    bp = x_prompt.shape[0]
    mod = _ada(jnp.concatenate([c_prompt, c_sample], axis=0), ada_w, ada_b)
    out_p = _trunk(x_prompt, mod[:, :bp], None, w, True)
    st = (state_mlstm_C, state_mlstm_n, state_mlstm_m, state_mlstm_conv, state_rwkv_S, state_rwkv_shift,
          state_hgrn_S)
    out_s = _trunk(x_sample, mod[:, bp:], st, w, False)
    return (out_p[0], out_s[0]) + tuple(out_p[1:]) + tuple(out_s[1:])
```

```python
import functools
import math

import jax
import jax.numpy as jnp
from jax import lax
from jax.experimental import pallas as pl
from jax.experimental.pallas import tpu as pltpu

F32 = jnp.float32
BF16 = jnp.bfloat16

EPS = 1e-6
A_HEADS = 4
A_DK = 128
A_CONV = 4
B_HEADS = 8
B_HEAD = 64
B_LORA_W = 64
B_LORA_A = 64
B_LORA_G = 128
B_LN_EPS = 64e-5
C_HEADS = 8
C_DK = 128
P_HEADS = 8
P_NKEYS = 128
P_TOPK = 16

LANES = 128
SUBLANES = 8
MXU_WIDTH = 256
VMEM_LIMIT = 56 * 1024 * 1024
NEG_INF = float("-inf")


def _bdot(a, b):
    return jnp.dot(a.astype(BF16), b.astype(BF16), preferred_element_type=F32)


def _dot_nt(a, b):
    return lax.dot_general(a.astype(BF16), b.astype(BF16), (((1,), (1,)), ((), ())), preferred_element_type=F32)


def _dot_tn(a, b):
    return lax.dot_general(a.astype(BF16), b.astype(BF16), (((0,), (0,)), ((), ())), preferred_element_type=F32)


def _split(x, terms):
    parts = []
    rem = x
    for _ in range(terms):
        p = rem.astype(BF16)
        parts.append(p)
        rem = rem - p.astype(F32)
    return parts


def _dot_split(a, b_bf, terms):
    out = None
    for p in _split(a, terms):
        d = jnp.dot(p, b_bf, preferred_element_type=F32)
        out = d if out is None else out + d
    return out


def _dot_split_left(a_bf, b, terms):
    out = None
    for p in _split(b, terms):
        d = jnp.dot(a_bf, p, preferred_element_type=F32)
        out = d if out is None else out + d
    return out


def _sigmoid(x):
    return 1.0 / (1.0 + jnp.exp(-x))


def _silu(x):
    return x * _sigmoid(x)


def _softplus(x):
    return jnp.maximum(x, 0.0) + jnp.log(1.0 + jnp.exp(-jnp.abs(x)))


def _gelu_tanh(x):
    return 0.5 * x * (1.0 + jnp.tanh(math.sqrt(2.0 / math.pi) * (x + 0.044715 * (x * x * x))))


def _rmsnorm(x, g):
    return x * lax.rsqrt(jnp.mean(x * x, axis=-1, keepdims=True) + EPS) * g


def _group_matrix(n, group, value):
    r = lax.broadcasted_iota(jnp.int32, (n, n), 0) // group
    c = lax.broadcasted_iota(jnp.int32, (n, n), 1) // group
    return jnp.where(r == c, value, 0.0).astype(BF16)


def _full_spec(arr):
    nd = arr.ndim
    return pl.BlockSpec(arr.shape, lambda *_: (0,) * nd)


def _params(n_grid):
    return pltpu.CompilerParams(dimension_semantics=("arbitrary",) * n_grid, vmem_limit_bytes=VMEM_LIMIT)


def _tok_spec(tt, width):
    return pl.BlockSpec((1, tt, width), lambda g, i: (g, i, 0))


def _mod_spec(mod3, tt):
    width = mod3.shape[2]
    if mod3.shape[1] == 1:
        return pl.BlockSpec((1, 1, width), lambda g, i: (g, 0, 0))
    return pl.BlockSpec((1, tt, width), lambda g, i: (g, i, 0))


def _ada_body(c_ref, w_ref, b_ref, o_ref):
    c = c_ref[...]
    o_ref[0] = _bdot(_silu(c), w_ref[0]) + b_ref[0]


def _ada(c_all, ada_w, ada_b):
    depth, d, n6 = ada_w.shape
    bc = c_all.shape[0]
    tn = n6 // 4
    return pl.pallas_call(
        _ada_body,
        grid=(depth, n6 // tn),
        in_specs=[pl.BlockSpec((bc, d), lambda l, j: (0, 0)),
                  pl.BlockSpec((1, d, tn), lambda l, j: (l, 0, j)),
                  pl.BlockSpec((1, 1, tn), lambda l, j: (l, 0, j))],
        out_specs=pl.BlockSpec((1, bc, tn), lambda l, j: (l, 0, j)),
        out_shape=jax.ShapeDtypeStruct((depth, bc, n6), F32),
        compiler_params=_params(2),
        name="ada_mod",
    )(c_all, ada_w, ada_b.reshape(depth, 1, n6))


def _even_pre_body(seq, tt, d, aw, bw, *refs):
    n_in = 15 if seq else 19
    ins, outs = refs[:n_in], refs[n_in:]
    (x_ref, mod_ref, ng_ref, w_ref, cw_ref, cb_ref, gb_ref, mu_ref, w0_ref, a0_ref, wla_ref, g2_ref,
     kk_ref, ka_ref, gs_ref) = ins[:15]
    (q_ref, k_ref, v_ref, so_ref, g8_ref, g8t_ref, r_ref, dec_ref, kt_ref, vb_ref, al_ref, be_ref, gg_ref,
     ctail_ref, stail_ref) = outs[:15]
    a2 = 2 * aw
    sw = 3 * bw + B_LORA_W + B_LORA_A + B_LORA_G

    x = x_ref[0]
    mod = mod_ref[0]
    h = _rmsnorm(x, ng_ref[...]) * (1.0 + mod[:, d:2 * d]) + mod[:, 0:d]
    p = jnp.dot(h.astype(BF16), w_ref[...], preferred_element_type=F32)
    u = p[:, 0:a2]
    v = p[:, a2:a2 + aw]
    o = p[:, a2 + aw:a2 + 2 * aw]
    pb = p[:, 4 * aw:4 * aw + sw]
    gt = p[:, 4 * aw + sw:4 * aw + sw + LANES]

    if seq:
        ubuf, pbuf = outs[15], outs[16]

        @pl.when(pl.program_id(1) == 0)
        def _():
            ubuf[0:SUBLANES, :] = jnp.zeros((SUBLANES, a2), F32)
            pbuf[0:SUBLANES, :] = jnp.zeros((SUBLANES, sw), F32)

        ubuf[SUBLANES:SUBLANES + tt, :] = u
        pbuf[SUBLANES:SUBLANES + tt, :] = pb
        u1 = ubuf[SUBLANES - 1:SUBLANES - 1 + tt, :]
        u2 = ubuf[SUBLANES - 2:SUBLANES - 2 + tt, :]
        u3 = ubuf[SUBLANES - 3:SUBLANES - 3 + tt, :]
        pprev = pbuf[SUBLANES - 1:SUBLANES - 1 + tt, :]
        ubuf[0:SUBLANES, :] = u[tt - SUBLANES:tt, :]
        pbuf[0:SUBLANES, :] = pb[tt - SUBLANES:tt, :]
        ctail_ref[0] = u[tt - SUBLANES:tt, :]
        stail_ref[0] = pb[tt - SUBLANES:tt, :]
    else:
        u3, u2, u1, pprev = ins[15][0], ins[16][0], ins[17][0], ins[18][0]
        ctail_ref[0] = u
        stail_ref[0] = pb

    cw = cw_ref[...]
    y = cb_ref[...] + u3 * cw[0:1, :] + u2 * cw[1:2, :] + u1 * cw[2:3, :] + u * cw[3:4, :]
    qk = _silu(y)
    q_ref[0] = qk[:, 0:aw] * (A_DK ** -0.5)
    k_ref[0] = qk[:, aw:a2]
    v_ref[0] = v
    so_ref[0] = _sigmoid(o)
    g = gt + gb_ref[...]
    lane = lax.broadcasted_iota(jnp.int32, g.shape, 1)
    gates = jnp.where(lane < A_HEADS, g, -_softplus(-g))
    g8_ref[0] = gates[:, 0:2 * A_HEADS]
    g8t_ref[0] = gates.T[0:2 * A_HEADS, :]

    xb = pb + mu_ref[...] * (pprev - pb)
    r = xb[:, 0:bw]
    kb = xb[:, bw:2 * bw]
    vb = xb[:, 2 * bw:3 * bw]
    la = xb[:, 3 * bw:3 * bw + LANES]
    gl = xb[:, 3 * bw + LANES:3 * bw + 2 * LANES]
    lane2 = lax.broadcasted_iota(jnp.int32, la.shape, 1)
    la_act = jnp.where(lane2 < B_LORA_W, jnp.tanh(la), la)
    lw = jnp.dot(la_act.astype(BF16), wla_ref[...], preferred_element_type=F32)
    wlog = -_softplus(-(w0_ref[...] + lw[:, 0:bw])) - 0.5
    decay = jnp.exp(-jnp.exp(wlog))
    a = _sigmoid(a0_ref[...] + lw[:, bw:2 * bw])
    gg = jnp.dot(_sigmoid(gl).astype(BF16), g2_ref[...], preferred_element_type=F32)
    kk = kb * kk_ref[...]
    ss = _dot_split(kk * kk, gs_ref[...], 2)
    kkn = kk * lax.rsqrt(jnp.maximum(ss, 1e-24))
    r_ref[0] = r
    dec_ref[0] = decay
    kt_ref[0] = kb * (1.0 + (a - 1.0) * ka_ref[...])
    vb_ref[0] = vb
    al_ref[0] = -kkn
    be_ref[0] = kkn * a
    gg_ref[0] = gg


def _even_pre(x3, mod3, ng, wcat, cw, cb, gb, mu, w0, a0, wla, g2, k_k, k_a, prev, tt):
    g_, ttot, d = x3.shape
    aw = d // 2
    bw = d // 2
    a2 = 2 * aw
    sw = 3 * bw + B_LORA_W + B_LORA_A + B_LORA_G
    seq = prev is None
    gsum = _group_matrix(bw, B_HEAD, 1.0)
    consts = [ng, wcat, cw, cb, gb, mu, w0, a0, wla, g2, k_k, k_a, gsum]
    ins = [x3, mod3] + consts
    in_specs = [_tok_spec(tt, d), _mod_spec(mod3, tt)] + [_full_spec(c) for c in consts]
    if not seq:
        ins += list(prev)
        in_specs += [_tok_spec(tt, a2)] * 3 + [_tok_spec(tt, sw)]
    nt = ttot // tt
    tail_rows = SUBLANES if seq else tt
    tail_tot = g_ * SUBLANES if seq else ttot

    def tail_spec(width):
        if seq:
            return pl.BlockSpec((1, SUBLANES, width), lambda g, i: (g, 0, 0))
        return _tok_spec(tt, width)

    def tail_shape(width):
        if seq:
            return jax.ShapeDtypeStruct((g_, SUBLANES, width), F32)
        return jax.ShapeDtypeStruct((g_, ttot, width), F32)

    tok = lambda w: jax.ShapeDtypeStruct((g_, ttot, w), F32)
    out_shape = [tok(aw), tok(aw), tok(aw), tok(aw), tok(2 * A_HEADS),
                 jax.ShapeDtypeStruct((g_, 2 * A_HEADS, ttot), F32),
                 tok(bw), tok(bw), tok(bw), tok(bw), tok(bw), tok(bw), tok(bw),
                 tail_shape(a2), tail_shape(sw)]
    out_specs = [_tok_spec(tt, aw)] * 4 + [_tok_spec(tt, 2 * A_HEADS),
                                           pl.BlockSpec((1, 2 * A_HEADS, tt), lambda g, i: (g, 0, i))]
    out_specs += [_tok_spec(tt, bw)] * 7 + [tail_spec(a2), tail_spec(sw)]
    scratch = [pltpu.VMEM((tt + SUBLANES, a2), F32), pltpu.VMEM((tt + SUBLANES, sw), F32)] if seq else []
    del tail_rows, tail_tot
    return pl.pallas_call(
        functools.partial(_even_pre_body, seq, tt, d, aw, bw),
        grid=(g_, nt),
        in_specs=in_specs,
        out_specs=out_specs,
        out_shape=out_shape,
        scratch_shapes=scratch,
        compiler_params=_params(2),
        name="even_pre_seq" if seq else "even_pre_step",
    )(*ins)


def _mlstm_chunk_body(L, q_ref, k_ref, v_ref, g8_ref, g8t_ref, c0_ref, n0_ref, m0_ref,
                      h_ref, c1_ref, n1_ref, m1_ref, C_s, n_s, m_s):
    c = pl.program_id(1)

    @pl.when(c == 0)
    def _():
        C_s[...] = c0_ref[0]
        n_s[...] = n0_ref[0]
        m_s[...] = m0_ref[0]

    row = lax.broadcasted_iota(jnp.int32, (L, L), 0)
    col = lax.broadcasted_iota(jnp.int32, (L, L), 1)
    causal = col <= row
    tril = jnp.where(causal, 1.0, 0.0).astype(BF16)
    triu = jnp.where(row <= col, 1.0, 0.0).astype(BF16)
    g8 = g8_ref[0]
    g8t = g8t_ref[0]
    bcols = _dot_split_left(tril, g8, 3)
    brows = _dot_split(g8t, triu, 3)
    for hd in range(A_HEADS):
        sl = slice(hd * A_DK, (hd + 1) * A_DK)
        q = q_ref[0, :, sl]
        k = k_ref[0, :, sl]
        v = v_ref[0, :, sl]
        i_row = g8t[hd:hd + 1, :]
        i_col = g8[:, hd:hd + 1]
        b_row = brows[A_HEADS + hd:A_HEADS + hd + 1, :]
        b_col = bcols[:, A_HEADS + hd:A_HEADS + hd + 1]
        m_prev = m_s[hd:hd + 1, 0:1]
        C = C_s[hd]
        n = n_s[hd:hd + 1, :]
        dmat = jnp.where(causal, b_col - b_row + i_row, NEG_INF)
        inter = b_col + m_prev
        m_t = jnp.maximum(inter, jnp.max(dmat, axis=1, keepdims=True))
        w_intra = jnp.exp(dmat - m_t)
        w_inter = jnp.exp(inter - m_t)
        s = _dot_nt(q, k) * w_intra
        num = _bdot(s, v) + w_inter * _bdot(q, C)
        den = jnp.sum(s, axis=1, keepdims=True) + w_inter * jnp.sum(q * n, axis=1, keepdims=True)
        h_ref[0, :, sl] = num / jnp.maximum(jnp.abs(den), jnp.exp(-m_t))
        m_new = m_t[L - 1:L, :]
        b_last = b_col[L - 1:L, :]
        kw = k * jnp.exp(b_last - b_col + i_col - m_new)
        carry = jnp.exp(b_last + m_prev - m_new)
        C_s[hd] = carry * C + _dot_tn(kw, v)
        n_s[hd:hd + 1, :] = carry * n + jnp.sum(kw, axis=0, keepdims=True)
        m_s[hd:hd + 1, :] = jnp.broadcast_to(m_new, (1, LANES))

    @pl.when(c == pl.num_programs(1) - 1)
    def _():
        c1_ref[0] = C_s[...]
        n1_ref[0] = n_s[...]
        m1_ref[0] = m_s[...]


def _mlstm_chunk(q, k, v, g8, g8t, C0, n0p, m0p, L):
    b, t, aw = q.shape
    st = lambda *s: pl.BlockSpec((1,) + s, lambda bb, c: (bb,) + (0,) * len(s))
    return pl.pallas_call(
        functools.partial(_mlstm_chunk_body, L),
        grid=(b, t // L),
        in_specs=[_tok_spec(L, aw)] * 3 + [_tok_spec(L, 2 * A_HEADS),
                                           pl.BlockSpec((1, 2 * A_HEADS, L), lambda bb, c: (bb, 0, c)),
                                           st(A_HEADS, A_DK, A_DK), st(SUBLANES, LANES), st(SUBLANES, LANES)],
        out_specs=[_tok_spec(L, aw), st(A_HEADS, A_DK, A_DK), st(SUBLANES, LANES), st(SUBLANES, LANES)],
        out_shape=[jax.ShapeDtypeStruct((b, t, aw), F32), jax.ShapeDtypeStruct(C0.shape, F32),
                   jax.ShapeDtypeStruct(n0p.shape, F32), jax.ShapeDtypeStruct(m0p.shape, F32)],
        scratch_shapes=[pltpu.VMEM((A_HEADS, A_DK, A_DK), F32), pltpu.VMEM((SUBLANES, LANES), F32),
                        pltpu.VMEM((SUBLANES, LANES), F32)],
        compiler_params=_params(2),
        name="mlstm_chunk",
    )(q, k, v, g8, g8t, C0, n0p, m0p)


def _row8(x):
    r = lax.broadcasted_iota(jnp.int32, (SUBLANES, x.shape[1]), 0)
    return jnp.where(r == 0, jnp.broadcast_to(x, (SUBLANES, x.shape[1])), 0.0)


def _mlstm_step_body(bb, q_ref, k_ref, v_ref, g8_ref, c0_ref, n0_ref, m0_ref, h_ref, c1_ref, n1_ref, m1_ref):
    g8 = g8_ref[...]
    m0 = m0_ref[...]
    for hd in range(A_HEADS):
        sl = slice(hd * A_DK, (hd + 1) * A_DK)
        li = g8[:, hd:hd + 1]
        lf = g8[:, A_HEADS + hd:A_HEADS + hd + 1]
        mp = m0[:, hd:hd + 1]
        m1 = jnp.maximum(lf + mp, li)
        wi = jnp.exp(li - m1)
        wf = jnp.exp(lf + mp - m1)
        floor = jnp.exp(-m1)
        for b in range(bb):
            q = q_ref[b:b + 1, sl]
            k = k_ref[b:b + 1, sl]
            v = v_ref[b:b + 1, sl]
            wib = wi[b:b + 1, :]
            wfb = wf[b:b + 1, :]
            C1 = wfb * c0_ref[b, hd] + wib * _dot_tn(_row8(k), _row8(v))
            n1 = wfb * n0_ref[b, hd:hd + 1, :] + wib * k
            num = _bdot(_row8(q), C1)[0:1, :]
            den = jnp.sum(q * n1, axis=1, keepdims=True)
            h_ref[b:b + 1, sl] = num / jnp.maximum(jnp.abs(den), floor[b:b + 1, :])
            c1_ref[b, hd] = C1
            n1_ref[b, hd:hd + 1, :] = n1
        m1_ref[:, hd:hd + 1] = m1


def _mlstm_step(q, k, v, g8, C0, n0, m0, bb):
    b, aw = q.shape
    row = lambda w: pl.BlockSpec((bb, w), lambda i: (i, 0))
    return pl.pallas_call(
        functools.partial(_mlstm_step_body, bb),
        grid=(b // bb,),
        in_specs=[row(aw)] * 3 + [row(2 * A_HEADS),
                                  pl.BlockSpec((bb, A_HEADS, A_DK, A_DK), lambda i: (i, 0, 0, 0)),
                                  pl.BlockSpec((bb, A_HEADS, A_DK), lambda i: (i, 0, 0)), row(A_HEADS)],
        out_specs=[row(aw), pl.BlockSpec((bb, A_HEADS, A_DK, A_DK), lambda i: (i, 0, 0, 0)),
                   pl.BlockSpec((bb, A_HEADS, A_DK), lambda i: (i, 0, 0)), row(A_HEADS)],
        out_shape=[jax.ShapeDtypeStruct((b, aw), F32), jax.ShapeDtypeStruct(C0.shape, F32),
                   jax.ShapeDtypeStruct(n0.shape, F32), jax.ShapeDtypeStruct(m0.shape, F32)],
        compiler_params=_params(1),
        name="mlstm_step",
    )(q, k, v, g8, C0, n0, m0)


def _rwkv_body(bb, tb, ty, al_ref, w_ref, be_ref, kt_ref, r_ref, v_ref, s0_ref, y_ref, s1_ref, S_s, t3_s, y_s):
    @pl.when(pl.program_id(1) == 0)
    def _():
        S_s[...] = s0_ref[...]

    npair = B_HEADS // 2
    ntile = bb * npair
    nrow = min(tb, SUBLANES)
    row = lax.broadcasted_iota(jnp.int32, (B_HEAD, LANES), 0)
    lane = lax.broadcasted_iota(jnp.int32, (B_HEAD, LANES), 1)
    eye_pair = row == (lane % B_HEAD)
    wred = _group_matrix(LANES, B_HEAD, 1.0)
    kk = lax.broadcasted_iota(jnp.int32, (nrow * LANES, LANES), 0)
    ll = lax.broadcasted_iota(jnp.int32, (nrow * LANES, LANES), 1)
    sel_head = ((kk % LANES) // B_HEAD) == (ll // B_HEAD)
    sel_off = (ll % B_HEAD) - (kk // LANES)

    def group(t8, carry):
        base = t8 * nrow if isinstance(t8, int) else pl.multiple_of(t8 * nrow, nrow)
        rows = pl.ds(base, nrow)
        for i in range(nrow):
            ri = slice(i, i + 1)
            lhs = []
            for b in range(bb):
                for p in range(npair):
                    sl = pl.ds(p * LANES, LANES)
                    lhs.append((S_s[b, p] * al_ref[b, rows, sl][ri]).astype(BF16))
            for b in range(bb):
                for p in range(npair):
                    sl = pl.ds(p * LANES, LANES)
                    lhs.append(jnp.where(eye_pair, v_ref[b, rows, sl][ri], 0.0).astype(BF16))
            red = jnp.dot(jnp.concatenate(lhs, axis=0), wred, preferred_element_type=F32)
            for b in range(bb):
                for p in range(npair):
                    sl = pl.ds(p * LANES, LANES)
                    idx = b * npair + p
                    sa = red[idx * B_HEAD:(idx + 1) * B_HEAD, :]
                    vcol = red[(ntile + idx) * B_HEAD:(ntile + idx + 1) * B_HEAD, :]
                    S = (S_s[b, p] * w_ref[b, rows, sl][ri] + sa * be_ref[b, rows, sl][ri]
                         + vcol * kt_ref[b, rows, sl][ri])
                    S_s[b, p] = S
                    t3_s[idx * B_HEAD:(idx + 1) * B_HEAD, i * LANES:(i + 1) * LANES] = (
                        S * r_ref[b, rows, sl][ri]).astype(BF16)
        wsel = jnp.where(sel_head & (sel_off == base % ty), 1.0, 0.0).astype(BF16)
        yg = jnp.dot(t3_s[...], wsel, preferred_element_type=F32)

        @pl.when(base % ty == 0)
        def _():
            y_s[...] = yg

        @pl.when(base % ty != 0)
        def _():
            y_s[...] += yg

        @pl.when((base + nrow) % ty == 0)
        def _():
            y_ref[0, base // ty] = y_s[...]
        return carry

    if tb == nrow:
        group(0, 0)
    else:
        lax.fori_loop(0, tb // nrow, group, 0)

    @pl.when(pl.program_id(1) == pl.num_programs(1) - 1)
    def _():
        s1_ref[...] = S_s[...]


def _rwkv_scan(al, w, be, kt, r, v, S0p, bb, tb):
    b, t, bw = al.shape
    npair = B_HEADS // 2
    ty = min(B_HEAD, t)
    nb = b // bb
    tok = pl.BlockSpec((bb, tb, bw), lambda i, j: (i, j, 0))
    st = pl.BlockSpec((bb, npair, B_HEAD, LANES), lambda i, j: (i, 0, 0, 0))
    nrow = min(tb, SUBLANES)
    rows = bb * npair * B_HEAD
    yt, S1p = pl.pallas_call(
        functools.partial(_rwkv_body, bb, tb, ty),
        grid=(nb, t // tb),
        in_specs=[tok] * 6 + [st],
        out_specs=[pl.BlockSpec((1, tb // ty, rows, LANES), lambda i, j: (i, j, 0, 0)), st],
        out_shape=[jax.ShapeDtypeStruct((nb, t // ty, rows, LANES), F32), jax.ShapeDtypeStruct(S0p.shape, F32)],
        scratch_shapes=[pltpu.VMEM((bb, npair, B_HEAD, LANES), F32), pltpu.VMEM((rows, nrow * LANES), BF16),
                        pltpu.VMEM((rows, LANES), F32)],
        compiler_params=_params(2),
        name="rwkv_scan",
    )(al, w, be, kt, r, v, S0p)
    y = yt.reshape(nb, t // ty, bb, npair, B_HEAD, 2, B_HEAD)[..., :ty]
    y = y.transpose(0, 2, 1, 6, 3, 5, 4).reshape(b, t, bw)
    return y, S1p


def _even_post_body(d, aw, bw, x_ref, mod_ref, ha_ref, so_ref, y_ref, r_ref, kt_ref, vb_ref, gg_ref,
                    ang_ref, lng_ref, lnb_ref, rk_ref, wo_ref, o_ref):
    ga = _group_matrix(aw, A_DK, 1.0 / A_DK)
    gb_mean = _group_matrix(bw, B_HEAD, 1.0 / B_HEAD)
    gb_sum = _group_matrix(bw, B_HEAD, 1.0)
    ha = ha_ref[0]
    ha = ha * lax.rsqrt(_dot_split(ha * ha, ga, 2) + EPS) * ang_ref[...] * so_ref[0]
    y = y_ref[0]
    yc = y - _dot_split(y, gb_mean, 2)
    var = _dot_split(yc * yc, gb_mean, 2)
    yn = yc * lax.rsqrt(var + B_LN_EPS) * lng_ref[...] + lnb_ref[...]
    bonus = _dot_split(r_ref[0] * kt_ref[0] * rk_ref[...], gb_sum, 2)
    y2 = (yn + bonus * vb_ref[0]) * gg_ref[0]
    out = (jnp.dot(ha.astype(BF16), wo_ref[0:aw, :], preferred_element_type=F32)
           + jnp.dot(y2.astype(BF16), wo_ref[aw:aw + bw, :], preferred_element_type=F32))
    o_ref[0] = x_ref[0] + mod_ref[0][:, 2 * d:3 * d] * out


def _even_post(x3, mod3, ha, so, y, r, kt, vb, gg, ang, lng, lnb, rk, wo, tt):
    g_, ttot, d = x3.shape
    aw = d // 2
    bw = d // 2
    consts = [ang, lng, lnb, rk, wo]
    return pl.pallas_call(
        functools.partial(_even_post_body, d, aw, bw),
        grid=(g_, ttot // tt),
        in_specs=[_tok_spec(tt, d), _mod_spec(mod3, tt)] + [_tok_spec(tt, aw)] * 7 + [_full_spec(c) for c in consts],
        out_specs=_tok_spec(tt, d),
        out_shape=jax.ShapeDtypeStruct(x3.shape, F32),
        compiler_params=_params(2),
        name="even_post",
    )(x3, mod3, ha, so, y, r, kt, vb, gg, *consts)


def _odd_pre_body(j, d, x_ref, mod_ref, ng_ref, w_ref, lbl_ref, q_ref, k_ref, v_ref, lf_ref, sg_ref):
    x = x_ref[0]
    mod = mod_ref[0]
    h = _rmsnorm(x, ng_ref[...]) * (1.0 + mod[:, d:2 * d]) + mod[:, 0:d]
    p = jnp.dot(h.astype(BF16), w_ref[...], preferred_element_type=F32)
    lbl = lbl_ref[...]
    n_odd = lbl.shape[0]
    mx = lbl[0:1, :]
    for i in range(1, n_odd):
        mx = jnp.maximum(mx, lbl[i:i + 1, :])
    ex = [jnp.exp(lbl[i:i + 1, :] - mx) for i in range(n_odd)]
    tot = ex[0]
    for i in range(1, n_odd):
        tot = tot + ex[i]
    lb = jnp.zeros_like(mx)
    for i in range(1, j + 1):
        lb = lb + ex[i] / tot
    f = lb + (1.0 - lb) * _sigmoid(p[:, d:2 * d])
    q_ref[0] = _silu(p[:, 0:d])
    k_ref[0] = 1.0 - f
    v_ref[0] = p[:, 2 * d:3 * d]
    lf_ref[0] = jnp.log(f)
    sg_ref[0] = _silu(p[:, 3 * d:4 * d])


def _odd_pre(x3, mod3, ng, w_in, lbl, j, tt):
    g_, ttot, d = x3.shape
    consts = [ng, w_in, lbl]
    return pl.pallas_call(
        functools.partial(_odd_pre_body, j, d),
        grid=(g_, ttot // tt),
        in_specs=[_tok_spec(tt, d), _mod_spec(mod3, tt)] + [_full_spec(c) for c in consts],
        out_specs=[_tok_spec(tt, d)] * 5,
        out_shape=[jax.ShapeDtypeStruct(x3.shape, F32)] * 5,
        compiler_params=_params(2),
        name="odd_pre",
    )(x3, mod3, *consts)


def _col_bcast(row, terms):
    parts = _split(row, terms)
    r = lax.broadcasted_iota(jnp.int32, (SUBLANES, row.shape[1]), 0)
    lhs = jnp.zeros((SUBLANES, row.shape[1]), F32)
    for i, p in enumerate(parts):
        lhs = jnp.where(r == i, jnp.broadcast_to(p.astype(F32), lhs.shape), lhs)
    ones = jnp.where(lax.broadcasted_iota(jnp.int32, (SUBLANES, LANES), 0) < terms, 1.0, 0.0).astype(BF16)
    return lax.dot_general(lhs.astype(BF16), ones, (((0,), (0,)), ((), ())), preferred_element_type=F32)


def _hgrn_chunk_body(L, sub, nchunk, hg, q_ref, k_ref, v_ref, g_ref, s0_ref, o_ref, s1_ref, S_s):
    @pl.when(pl.program_id(2) == 0)
    def _():
        S_s[...] = s0_ref[0]

    nj = L // sub
    rowi = lax.broadcasted_iota(jnp.int32, (L, C_DK), 0)
    rr = lax.broadcasted_iota(jnp.int32, (L, L), 0)
    cc = lax.broadcasted_iota(jnp.int32, (L, L), 1)
    tril = jnp.where(cc <= rr, 1.0, 0.0).astype(BF16)
    ones = jnp.ones((C_DK, LANES), BF16)

    def chunk(ci, carry):
        base = pl.multiple_of(ci * L, L)
        rows = pl.ds(base, L)
        b_all = _dot_split_left(tril, g_ref[0, rows, :], 3)
        heads = []
        for hd in range(hg):
            sl = slice(hd * C_DK, (hd + 1) * C_DK)
            heads.append((q_ref[0, rows, sl], k_ref[0, rows, sl], v_ref[0, rows, sl], b_all[:, sl], S_s[hd]))
        outs = [_bdot(q * jnp.exp(b), S) for (q, k, v, b, S) in heads]
        for dlt in range(sub):
            prs, vss = [], []
            for (q, k, v, b, S) in heads:
                if dlt == 0:
                    prs.append((q * k).astype(BF16))
                    vss.append(v)
                else:
                    ok = (rowi % sub) >= dlt
                    e = jnp.exp(jnp.where(ok, b - pltpu.roll(b, dlt, 0), NEG_INF))
                    prs.append((q * pltpu.roll(k, dlt, 0) * e).astype(BF16))
                    vss.append(pltpu.roll(v, dlt, 0))
            rs = jnp.dot(jnp.concatenate(prs, axis=0), ones, preferred_element_type=F32)
            outs = [o + rs[i * L:(i + 1) * L, :] * vs for i, (o, vs) in enumerate(zip(outs, vss))]
        for hd, ((q, k, v, b, S), o) in enumerate(zip(heads, outs)):
            if nj > 1:
                amat = None
                for jj in range(nj - 1):
                    end = (jj + 1) * sub
                    bref = b[end - 1:end, :]
                    qj = q * jnp.exp(jnp.where(rowi >= end, b - bref, NEG_INF))
                    kj = k * jnp.exp(jnp.where((rowi >= end - sub) & (rowi < end), bref - b, NEG_INF))
                    a = _dot_nt(qj, kj)
                    amat = a if amat is None else amat + a
                o = o + _bdot(amat, v)
            o_ref[0, rows, hd * C_DK:(hd + 1) * C_DK] = o
            b_last = b[L - 1:L, :]
            S_s[hd] = _col_bcast(jnp.exp(b_last), 3) * S + _dot_tn(k * jnp.exp(b_last - b), v)
        return carry

    lax.fori_loop(0, nchunk, chunk, 0)

    @pl.when(pl.program_id(2) == pl.num_programs(2) - 1)
    def _():
        s1_ref[0] = S_s[...]


def _hgrn_chunk(q, k, v, lf, S0, tblk, L, sub, hg):
    b, t, d = q.shape
    tok = pl.BlockSpec((1, tblk, hg * C_DK), lambda bb, h, c: (bb, c, h))
    st = pl.BlockSpec((1, hg, C_DK, C_DK), lambda bb, h, c: (bb, h, 0, 0))
    return pl.pallas_call(
        functools.partial(_hgrn_chunk_body, L, sub, tblk // L, hg),
        grid=(b, C_HEADS // hg, t // tblk),
        in_specs=[tok] * 4 + [st],
        out_specs=[tok, st],
        out_shape=[jax.ShapeDtypeStruct((b, t, d), F32), jax.ShapeDtypeStruct(S0.shape, F32)],
        scratch_shapes=[pltpu.VMEM((hg, C_DK, C_DK), F32)],
        compiler_params=_params(3),
        name="hgrn_chunk",
    )(q, k, v, lf, S0)


def _hgrn_step_body(bb, q_ref, k_ref, v_ref, g_ref, s0_ref, o_ref, s1_ref):
    for b in range(bb):
        for hd in range(C_HEADS):
            sl = slice(hd * C_DK, (hd + 1) * C_DK)
            q = q_ref[b:b + 1, sl]
            k = k_ref[b:b + 1, sl]
            v = v_ref[b:b + 1, sl]
            f = jnp.exp(g_ref[b:b + 1, sl])
            S1 = _col_bcast(f, 3) * s0_ref[b, hd] + _dot_tn(_row8(k), _row8(v))
            o_ref[b:b + 1, sl] = _bdot(_row8(q), S1)[0:1, :]
            s1_ref[b, hd] = S1


def _hgrn_step(q, k, v, lf, S0, bb):
    b, d = q.shape
    row = pl.BlockSpec((bb, d), lambda i: (i, 0))
    st = pl.BlockSpec((bb, C_HEADS, C_DK, C_DK), lambda i: (i, 0, 0, 0))
    return pl.pallas_call(
        functools.partial(_hgrn_step_body, bb),
        grid=(b // bb,),
        in_specs=[row] * 4 + [st],
        out_specs=[row, st],
        out_shape=[jax.ShapeDtypeStruct((b, d), F32), jax.ShapeDtypeStruct(S0.shape, F32)],
        compiler_params=_params(1),
        name="hgrn_step",
    )(q, k, v, lf, S0)


def _odd_post_body(d, x_ref, mod_ref, o_ref, sg_ref, ng_ref, wo_ref, out_ref):
    gm = _group_matrix(d, C_DK, 1.0 / C_DK)
    o = o_ref[0]
    o = o * lax.rsqrt(_dot_split(o * o, gm, 2) + EPS) * ng_ref[...] * sg_ref[0]
    out = jnp.dot(o.astype(BF16), wo_ref[...], preferred_element_type=F32)
    out_ref[0] = x_ref[0] + mod_ref[0][:, 2 * d:3 * d] * out


def _odd_post(x3, mod3, o, sg, ng_tiled, wo, tt):
    g_, ttot, d = x3.shape
    consts = [ng_tiled, wo]
    return pl.pallas_call(
        functools.partial(_odd_post_body, d),
        grid=(g_, ttot // tt),
        in_specs=[_tok_spec(tt, d), _mod_spec(mod3, tt)] + [_tok_spec(tt, d)] * 2 + [_full_spec(c) for c in consts],
        out_specs=_tok_spec(tt, d),
        out_shape=jax.ShapeDtypeStruct(x3.shape, F32),
        compiler_params=_params(2),
        name="odd_post",
    )(x3, mod3, o, sg, *consts)


P_NCAND = P_TOPK + 1
P_CAND_ROWS = 3 * SUBLANES


P_RANK_OUT = 99.0


def _top_sorted(s, want_rank):
    ridx = lax.broadcasted_iota(jnp.int32, (P_CAND_ROWS, s.shape[1]), 0)
    acc = jnp.full((P_CAND_ROWS, s.shape[1]), NEG_INF, F32)
    rank = jnp.full(s.shape, P_RANK_OUT, F32) if want_rank else None
    cur = s
    for i in range(P_NCAND):
        m = jnp.max(cur, axis=0, keepdims=True)
        acc = jnp.where(ridx == i, m, acc)
        hit = cur == m
        if want_rank:
            rank = jnp.where(hit, float(i), rank)
        cur = jnp.where(hit, NEG_INF, cur)
    return acc, rank


def _peer_body(tm, te, ne, d, x_ref, mod_ref, ng_ref, wqt_ref, keys_ref, u_ref, v_ref, o_ref,
               ht_s, n1_s, e1_s, r2_s, e2_s, gate_s, w_s, acc_s):
    j = pl.program_id(2)
    dk2 = keys_ref.shape[3]
    nsub = te // P_NKEYS

    @pl.when(j == 0)
    def _():
        x = x_ref[0]
        mod = mod_ref[0]
        h = _rmsnorm(x, ng_ref[...]) * (1.0 + mod[:, 4 * d:5 * d]) + mod[:, 3 * d:4 * d]
        ht = h.T.astype(BF16)
        ht_s[...] = ht
        qt = jnp.dot(wqt_ref[...], ht, preferred_element_type=F32)
        r8 = lax.broadcasted_iota(jnp.int32, (SUBLANES, LANES), 0)
        for hd in range(P_HEADS):
            s1_all = _bdot(keys_ref[hd, 0], qt[(2 * hd) * dk2:(2 * hd + 1) * dk2, :])
            s2_all = _bdot(keys_ref[hd, 1], qt[(2 * hd + 1) * dk2:(2 * hd + 2) * dk2, :])
            for tc in range(tm // LANES):
                ls = slice(tc * LANES, (tc + 1) * LANES)
                s1 = s1_all[:, ls]
                s2 = s2_all[:, ls]
                a, _ = _top_sorted(s1, False)
                b, rank2 = _top_sorted(s2, True)
                blocks = [a[0:1, :] + b]
                for i in range(2, SUBLANES + 1):
                    blocks.append(jnp.where(r8 < P_NCAND // i, a[i - 1:i, :] + b[0:SUBLANES, :], NEG_INF))
                blocks.append(a[SUBLANES:P_CAND_ROWS, :] + b[0:1, :])
                cand = jnp.concatenate(blocks, axis=0)
                cur = cand
                best = None
                for _ in range(P_NCAND):
                    prev, best = best, jnp.max(cur, axis=0, keepdims=True)
                    cur = jnp.where(cur == best, NEG_INF, cur)
                tau = 0.5 * (prev + best)
                top = a[0:1, :] + b[0:1, :]
                z = jnp.sum(jnp.where(cand >= tau, jnp.exp(cand - top), 0.0), axis=0, keepdims=True)
                n1 = jnp.zeros_like(s1)
                for i in range(P_NCAND):
                    n1 = jnp.where(s1 >= tau - b[i:i + 1, :], float(i + 1), n1)
                n1_s[hd, :, ls] = n1
                e1_s[hd, :, ls] = jnp.exp(s1 - a[0:1, :])
                r2_s[hd, :, ls] = rank2.astype(BF16)
                e2_s[hd, :, ls] = (jnp.exp(s2 - b[0:1, :]) / z).astype(BF16)
        acc_s[...] = jnp.zeros_like(acc_s)
        w_s[1] = jnp.zeros((te, tm), BF16)

    assert nsub == SUBLANES
    jj = jnp.minimum(j, ne - 1)
    grp = pl.ds(pl.multiple_of(jj * nsub, nsub), nsub)

    def gates(tc):
        ls = slice(tc * LANES, (tc + 1) * LANES)
        n1g = [n1_s[hd, grp, ls] for hd in range(P_HEADS)]
        e1g = [e1_s[hd, grp, ls] for hd in range(P_HEADS)]
        zero = jnp.zeros((P_NKEYS, LANES), BF16)
        for ii in range(nsub):
            rows = slice(ii * P_NKEYS, (ii + 1) * P_NKEYS)
            gate = zero
            for hd in range(P_HEADS):
                keep = r2_s[hd, :, ls] < n1g[hd][ii:ii + 1, :].astype(BF16)
                gate = gate + jnp.where(keep, e2_s[hd, :, ls], zero) * e1g[hd][ii:ii + 1, :].astype(BF16)
            gate_s[rows, ls] = gate

    piece = min(tm, MXU_WIDTH)
    tc_per_piece = piece // LANES
    for pc in range(tm // piece):
        cs = slice(pc * piece, (pc + 1) * piece)
        acc_s[cs, :] += lax.dot_general(w_s[(j + 1) % 2, :, cs], v_ref[0], (((0,), (0,)), ((), ())),
                                        preferred_element_type=F32)
        for tc in range(pc * tc_per_piece, pc * tc_per_piece + (tc_per_piece + 1) // 2):
            gates(tc)
        act = jnp.dot(u_ref[0], ht_s[:, cs], preferred_element_type=F32)
        for tc in range(pc * tc_per_piece + (tc_per_piece + 1) // 2, (pc + 1) * tc_per_piece):
            gates(tc)
        w_s[j % 2, :, cs] = gate_s[:, cs] * _gelu_tanh(act).astype(BF16)

    @pl.when(j == ne)
    def _():
        o_ref[0] = x_ref[0] + mod_ref[0][:, 5 * d:6 * d] * acc_s[...]


def _peer(x3, mod3, ng, wqt, keys_bf, u_bf, v_bf, tm, te):
    g_, ttot, d = x3.shape
    ne = u_bf.shape[0] // te
    nk = P_NKEYS
    if mod3.shape[1] == 1:
        mod_spec = pl.BlockSpec((1, 1, mod3.shape[2]), lambda g, i, j: (g, 0, 0))
    else:
        mod_spec = pl.BlockSpec((1, tm, mod3.shape[2]), lambda g, i, j: (g, i, 0))
    full = lambda arr: pl.BlockSpec(arr.shape, lambda g, i, j: (0,) * arr.ndim)
    u3 = u_bf.reshape(ne, te, d)
    v3 = v_bf.reshape(ne, te, d)
    return pl.pallas_call(
        functools.partial(_peer_body, tm, te, ne, d),
        grid=(g_, ttot // tm, ne + 1),
        in_specs=[pl.BlockSpec((1, tm, d), lambda g, i, j: (g, i, 0)), mod_spec, full(ng), full(wqt), full(keys_bf),
                  pl.BlockSpec((1, te, d), lambda g, i, j: (jnp.minimum(j, ne - 1), 0, 0)),
                  pl.BlockSpec((1, te, d), lambda g, i, j: (jnp.maximum(j - 1, 0), 0, 0))],
        out_specs=pl.BlockSpec((1, tm, d), lambda g, i, j: (g, i, 0)),
        out_shape=jax.ShapeDtypeStruct(x3.shape, F32),
        scratch_shapes=[pltpu.VMEM((d, tm), BF16),
                        pltpu.VMEM((P_HEADS, nk, tm), F32), pltpu.VMEM((P_HEADS, nk, tm), F32),
                        pltpu.VMEM((P_HEADS, nk, tm), BF16), pltpu.VMEM((P_HEADS, nk, tm), BF16),
                        pltpu.VMEM((te, tm), BF16), pltpu.VMEM((2, te, tm), BF16), pltpu.VMEM((tm, d), F32)],
        compiler_params=_params(3),
        name="peer",
    )(x3, mod3, ng, wqt, keys_bf, u3, v3)


def _final_norm_body(x_ref, g_ref, o_ref):
    o_ref[0] = _rmsnorm(x_ref[0], g_ref[...])


def _final_norm(x3, g, tt):
    g_, ttot, d = x3.shape
    return pl.pallas_call(
        _final_norm_body,
        grid=(g_, ttot // tt),
        in_specs=[_tok_spec(tt, d), _full_spec(g)],
        out_specs=_tok_spec(tt, d),
        out_shape=jax.ShapeDtypeStruct(x3.shape, F32),
        compiler_params=_params(2),
        name="final_norm",
    )(x3, g)


def _pair_state(S):
    b = S.shape[0]
    return S.reshape(b, B_HEADS // 2, 2, B_HEAD, B_HEAD).transpose(0, 1, 3, 2, 4).reshape(b, B_HEADS // 2, B_HEAD, LANES)


def _unpair_state(Sp):
    b = Sp.shape[0]
    return Sp.reshape(b, B_HEADS // 2, B_HEAD, 2, B_HEAD).transpose(0, 1, 3, 2, 4).reshape(b, B_HEADS, B_HEAD, B_HEAD)


def _trunk(x, mod_all, st, w, seq):
    bsz, t, d = x.shape
    depth = mod_all.shape[0]
    aw = d // 2
    bw = d // 2
    if seq:
        x3 = x
        tt = min(256, t)
        tm = min(512, t)
        mods = [mod_all[l][:, None, :] for l in range(depth)]
    else:
        x3 = x.reshape(1, bsz * t, d)
        tt = bsz * t
        tm = bsz * t
        mods = [mod_all[l][None] for l in range(depth)]
    row = lambda v_: v_.reshape(1, -1)
    new_even, new_odd = [], []
    for l in range(depth):
        j = l // 2
        if l % 2 == 0:
            e = w["even"][j]
            if seq:
                prev = None
            else:
                conv0, shift0 = st[3][j], st[5][j]
                prev = (conv0[:, 0][None], conv0[:, 1][None], conv0[:, 2][None], shift0[:, 0][None])
            (q, k, v, so, g8, g8t, r, dec, kt, vb, al, be, gg, ctail, stail) = _even_pre(
                x3, mods[l], row(w["norm_mix_g"][l]), e["wcat"], e["conv_w"], row(e["conv_b"]), e["gate_b"],
                row(e["mu"]), row(e["w0"]), row(e["a0"]), e["wla"], e["g2"], row(e["k_k"]), row(e["k_a"]), prev, tt)
            if seq:
                C0 = jnp.zeros((bsz, A_HEADS, A_DK, A_DK), F32)
                nm0 = jnp.zeros((bsz, SUBLANES, LANES), F32)
                ha, C1, n1p, m1p = _mlstm_chunk(q, k, v, g8, g8t, C0, nm0, nm0, min(256, t))
                n1 = n1p[:, :A_HEADS, :]
                m1 = m1p[:, :A_HEADS, 0]
                S0p = jnp.zeros((bsz, B_HEADS // 2, B_HEAD, LANES), F32)
                y, S1p = _rwkv_scan(al, dec, be, kt, r, vb, S0p, min(8, bsz), min(128, t))
                conv1 = ctail[:, SUBLANES - (A_CONV - 1):, :]
                shift1 = stail[:, SUBLANES - 1:, :]
            else:
                ha2, C1, n1, m1 = _mlstm_step(q[0], k[0], v[0], g8[0], st[0][j], st[1][j], st[2][j], 8)
                ha = ha2[None]
                tok = lambda a_: a_[0][:, None, :]
                y2, S1p = _rwkv_scan(tok(al), tok(dec), tok(be), tok(kt), tok(r), tok(vb), _pair_state(st[4][j]), 8, 1)
                y = y2.reshape(1, bsz, bw)
                conv1 = jnp.concatenate([st[3][j][:, 1:], ctail[0][:, None, :]], axis=1)
                shift1 = stail[0][:, None, :]
            x3 = _even_post(x3, mods[l], ha, so, y, r, kt, vb, gg, row(e["a_norm_g"]), row(e["ln_g"]),
                            row(e["ln_b"]), row(e["r_k"]), e["w_out"], tt)
            new_even.append((C1, n1, m1, conv1, _unpair_state(S1p), shift1))
        else:
            o_ = w["odd"][j]
            q, k, v, lf, sg = _odd_pre(x3, mods[l], row(w["norm_mix_g"][l]), o_["w_in"], w["lb_logits"], j, tt)
            if seq:
                S0 = jnp.zeros((bsz, C_HEADS, C_DK, C_DK), F32)
                o, S1 = _hgrn_chunk(q, k, v, lf, S0, min(256, t), min(64, t), min(16, t), 4)
            else:
                o2, S1 = _hgrn_step(q[0], k[0], v[0], lf[0], st[6][j], 8)
                o = o2[None]
            x3 = _odd_post(x3, mods[l], o, sg, row(jnp.tile(o_["norm_g"], C_HEADS)), o_["w_out"], tt)
            new_odd.append(S1)
        pw = w["peer"][l]
        x3 = _peer(x3, mods[l], row(w["norm_ffn_g"][l]), pw["wqt"], pw["keys"], pw["u"], pw["v"], tm, 1024)
    y = _final_norm(x3, row(w["norm_final_g"]), tt).reshape(bsz, t, d)
    ev = [jnp.stack([s[i] for s in new_even]) for i in range(6)]
    return (y, ev[0], ev[1], ev[2], ev[3], ev[4], ev[5], jnp.stack(new_odd))


def _prepare_weights(norm_mix_g, norm_ffn_g, norm_final_g, even_w_in, even_w_out, mlstm_conv_w, mlstm_conv_b,
                     mlstm_gate_b, mlstm_norm_g, rwkv_mu, rwkv_w0, rwkv_w2, rwkv_a0, rwkv_a2, rwkv_g2, rwkv_k_k,
                     rwkv_k_a, rwkv_r_k, rwkv_ln_g, rwkv_ln_b, odd_w_in, odd_w_out, hgrn_lb_logits, hgrn_norm_g,
                     peer_w_q, peer_keys, peer_u, peer_v):
    d = even_w_in.shape[1]
    aw = d // 2
    bw = d // 2
    sw = 3 * bw + B_LORA_W + B_LORA_A + B_LORA_G
    even = []
    for j in range(even_w_in.shape[0]):
        wi = even_w_in[j]
        gates_w = wi[:, 4 * aw:4 * aw + 2 * A_HEADS]
        wcat = jnp.concatenate([wi[:, 0:4 * aw], wi[:, 4 * aw + 2 * A_HEADS:],
                                jnp.pad(gates_w, ((0, 0), (0, LANES - 2 * A_HEADS)))], axis=1).astype(BF16)
        assert wcat.shape[1] == 4 * aw + sw + LANES
        wla = jnp.zeros((B_LORA_W + B_LORA_A, 2 * bw), F32)
        wla = wla.at[:B_LORA_W, :bw].set(rwkv_w2[j]).at[B_LORA_W:, bw:].set(rwkv_a2[j]).astype(BF16)
        even.append(dict(
            wcat=wcat, conv_w=mlstm_conv_w[j], conv_b=mlstm_conv_b[j],
            gate_b=jnp.pad(mlstm_gate_b[j], (0, LANES - 2 * A_HEADS)).reshape(1, LANES),
            mu=rwkv_mu[j], w0=rwkv_w0[j], a0=rwkv_a0[j], wla=wla, g2=rwkv_g2[j].astype(BF16),
            k_k=rwkv_k_k[j], k_a=rwkv_k_a[j], r_k=rwkv_r_k[j], ln_g=rwkv_ln_g[j], ln_b=rwkv_ln_b[j],
            a_norm_g=mlstm_norm_g[j], w_out=even_w_out[j].astype(BF16)))
    odd = [dict(w_in=odd_w_in[j].astype(BF16), w_out=odd_w_out[j].astype(BF16), norm_g=hgrn_norm_g[j])
           for j in range(odd_w_in.shape[0])]
    peer = [dict(wqt=peer_w_q[l].T.astype(BF16), keys=peer_keys[l].astype(BF16), u=peer_u[l].astype(BF16),
                 v=peer_v[l].astype(BF16)) for l in range(peer_w_q.shape[0])]
    return dict(norm_mix_g=norm_mix_g, norm_ffn_g=norm_ffn_g, norm_final_g=norm_final_g, even=even, odd=odd,
                peer=peer, lb_logits=hgrn_lb_logits)


def kernel(x_prompt, x_sample, c_prompt, c_sample, state_mlstm_C, state_mlstm_n, state_mlstm_m, state_mlstm_conv, state_rwkv_S, state_rwkv_shift, state_hgrn_S, norm_mix_g, norm_ffn_g, norm_final_g, ada_w, ada_b, even_w_in, even_w_out, mlstm_conv_w, mlstm_conv_b, mlstm_gate_b, mlstm_norm_g, rwkv_mu, rwkv_w0, rwkv_w2, rwkv_a0, rwkv_a2, rwkv_g2, rwkv_k_k, rwkv_k_a, rwkv_r_k, rwkv_ln_g, rwkv_ln_b, odd_w_in, odd_w_out, hgrn_lb_logits, hgrn_norm_g, peer_w_q, peer_keys, peer_u, peer_v):
    w = _prepare_weights(norm_mix_g, norm_ffn_g, norm_final_g, even_w_in, even_w_out, mlstm_conv_w, mlstm_conv_b,
                         mlstm_gate_b, mlstm_norm_g, rwkv_mu, rwkv_w0, rwkv_w2, rwkv_a0, rwkv_a2, rwkv_g2, rwkv_k_k,
                         rwkv_k_a, rwkv_r_k, rwkv_ln_g, rwkv_ln_b, odd_w_in, odd_w_out, hgrn_lb_logits, hgrn_norm_g,
                         peer_w_q, peer_keys, peer_u, peer_v)
    bp = x_prompt.shape[0]
    mod = _ada(jnp.concatenate([c_prompt, c_sample], axis=0), ada_w, ada_b)
    out_p = _trunk(x_prompt, mod[:, :bp], None, w, True)
    st = (state_mlstm_C, state_mlstm_n, state_mlstm_m, state_mlstm_conv, state_rwkv_S, state_rwkv_shift,
          state_hgrn_S)
    out_s = _trunk(x_sample, mod[:, bp:], st, w, False)
    return (out_p[0], out_s[0]) + tuple(out_p[1:]) + tuple(out_s[1:])
```

```python
import functools
import math

import jax
import jax.numpy as jnp
from jax import lax
from jax.experimental import pallas as pl
from jax.experimental.pallas import tpu as pltpu

F32 = jnp.float32
BF16 = jnp.bfloat16

EPS = 1e-6
A_HEADS = 4
A_DK = 128
A_CONV = 4
B_HEADS = 8
B_HEAD = 64
B_LORA_W = 64
B_LORA_A = 64
B_LORA_G = 128
B_LN_EPS = 64e-5
C_HEADS = 8
C_DK = 128
P_HEADS = 8
P_NKEYS = 128
P_TOPK = 16

LANES = 128
SUBLANES = 8
MXU_WIDTH = 256
VMEM_LIMIT = 56 * 1024 * 1024
NEG_INF = float("-inf")


def _bdot(a, b):
    return jnp.dot(a.astype(BF16), b.astype(BF16), preferred_element_type=F32)


def _dot_nt(a, b):
    return lax.dot_general(a.astype(BF16), b.astype(BF16), (((1,), (1,)), ((), ())), preferred_element_type=F32)


def _dot_tn(a, b):
    return lax.dot_general(a.astype(BF16), b.astype(BF16), (((0,), (0,)), ((), ())), preferred_element_type=F32)


def _split(x, terms):
    parts = []
    rem = x
    for _ in range(terms):
        p = rem.astype(BF16)
        parts.append(p)
        rem = rem - p.astype(F32)
    return parts


def _dot_split(a, b_bf, terms):
    out = None
    for p in _split(a, terms):
        d = jnp.dot(p, b_bf, preferred_element_type=F32)
        out = d if out is None else out + d
    return out


def _dot_split_left(a_bf, b, terms):
    out = None
    for p in _split(b, terms):
        d = jnp.dot(a_bf, p, preferred_element_type=F32)
        out = d if out is None else out + d
    return out


def _sigmoid(x):
    return 1.0 / (1.0 + jnp.exp(-x))


def _silu(x):
    return x * _sigmoid(x)


def _softplus(x):
    return jnp.maximum(x, 0.0) + jnp.log(1.0 + jnp.exp(-jnp.abs(x)))


def _gelu_tanh(x):
    return 0.5 * x * (1.0 + jnp.tanh(math.sqrt(2.0 / math.pi) * (x + 0.044715 * (x * x * x))))


def _rmsnorm(x, g):
    return x * lax.rsqrt(jnp.mean(x * x, axis=-1, keepdims=True) + EPS) * g


def _group_matrix(n, group, value):
    r = lax.broadcasted_iota(jnp.int32, (n, n), 0) // group
    c = lax.broadcasted_iota(jnp.int32, (n, n), 1) // group
    return jnp.where(r == c, value, 0.0).astype(BF16)


def _full_spec(arr):
    nd = arr.ndim
    return pl.BlockSpec(arr.shape, lambda *_: (0,) * nd)


def _params(n_grid):
    return pltpu.CompilerParams(dimension_semantics=("arbitrary",) * n_grid, vmem_limit_bytes=VMEM_LIMIT)


def _tok_spec(tt, width):
    return pl.BlockSpec((1, tt, width), lambda g, i: (g, i, 0))


def _mod_spec(mod3, tt):
    width = mod3.shape[2]
    if mod3.shape[1] == 1:
        return pl.BlockSpec((1, 1, width), lambda g, i: (g, 0, 0))
    return pl.BlockSpec((1, tt, width), lambda g, i: (g, i, 0))


def _ada_body(c_ref, w_ref, b_ref, o_ref):
    c = c_ref[...]
    o_ref[0] = _bdot(_silu(c), w_ref[0]) + b_ref[0]


def _ada(c_all, ada_w, ada_b):
    depth, d, n6 = ada_w.shape
    bc = c_all.shape[0]
    tn = n6 // 4
    return pl.pallas_call(
        _ada_body,
        grid=(depth, n6 // tn),
        in_specs=[pl.BlockSpec((bc, d), lambda l, j: (0, 0)),
                  pl.BlockSpec((1, d, tn), lambda l, j: (l, 0, j)),
                  pl.BlockSpec((1, 1, tn), lambda l, j: (l, 0, j))],
        out_specs=pl.BlockSpec((1, bc, tn), lambda l, j: (l, 0, j)),
        out_shape=jax.ShapeDtypeStruct((depth, bc, n6), F32),
        compiler_params=_params(2),
        name="ada_mod",
    )(c_all, ada_w, ada_b.reshape(depth, 1, n6))


def _even_pre_body(seq, tt, d, aw, bw, *refs):
    n_in = 15 if seq else 19
    ins, outs = refs[:n_in], refs[n_in:]
    (x_ref, mod_ref, ng_ref, w_ref, cw_ref, cb_ref, gb_ref, mu_ref, w0_ref, a0_ref, wla_ref, g2_ref,
     kk_ref, ka_ref, gs_ref) = ins[:15]
    (q_ref, k_ref, v_ref, so_ref, g8_ref, g8t_ref, r_ref, dec_ref, kt_ref, vb_ref, al_ref, be_ref, gg_ref,
     ctail_ref, stail_ref) = outs[:15]
    a2 = 2 * aw
    sw = 3 * bw + B_LORA_W + B_LORA_A + B_LORA_G

    x = x_ref[0]
    mod = mod_ref[0]
    h = _rmsnorm(x, ng_ref[...]) * (1.0 + mod[:, d:2 * d]) + mod[:, 0:d]
    p = jnp.dot(h.astype(BF16), w_ref[...], preferred_element_type=F32)
    u = p[:, 0:a2]
    v = p[:, a2:a2 + aw]
    o = p[:, a2 + aw:a2 + 2 * aw]
    pb = p[:, 4 * aw:4 * aw + sw]
    gt = p[:, 4 * aw + sw:4 * aw + sw + LANES]

    if seq:
        ubuf, pbuf = outs[15], outs[16]

        @pl.when(pl.program_id(1) == 0)
        def _():
            ubuf[0:SUBLANES, :] = jnp.zeros((SUBLANES, a2), F32)
            pbuf[0:SUBLANES, :] = jnp.zeros((SUBLANES, sw), F32)

        ubuf[SUBLANES:SUBLANES + tt, :] = u
        pbuf[SUBLANES:SUBLANES + tt, :] = pb
        u1 = ubuf[SUBLANES - 1:SUBLANES - 1 + tt, :]
        u2 = ubuf[SUBLANES - 2:SUBLANES - 2 + tt, :]
        u3 = ubuf[SUBLANES - 3:SUBLANES - 3 + tt, :]
        pprev = pbuf[SUBLANES - 1:SUBLANES - 1 + tt, :]
        ubuf[0:SUBLANES, :] = u[tt - SUBLANES:tt, :]
        pbuf[0:SUBLANES, :] = pb[tt - SUBLANES:tt, :]
        ctail_ref[0] = u[tt - SUBLANES:tt, :]
        stail_ref[0] = pb[tt - SUBLANES:tt, :]
    else:
        u3, u2, u1, pprev = ins[15][0], ins[16][0], ins[17][0], ins[18][0]
        ctail_ref[0] = u
        stail_ref[0] = pb

    cw = cw_ref[...]
    y = cb_ref[...] + u3 * cw[0:1, :] + u2 * cw[1:2, :] + u1 * cw[2:3, :] + u * cw[3:4, :]
    qk = _silu(y)
    q_ref[0] = qk[:, 0:aw] * (A_DK ** -0.5)
    k_ref[0] = qk[:, aw:a2]
    v_ref[0] = v
    so_ref[0] = _sigmoid(o)
    g = gt + gb_ref[...]
    lane = lax.broadcasted_iota(jnp.int32, g.shape, 1)
    gates = jnp.where(lane < A_HEADS, g, -_softplus(-g))
    g8_ref[0] = gates[:, 0:2 * A_HEADS]
    g8t_ref[0] = gates.T[0:2 * A_HEADS, :]

    xb = pb + mu_ref[...] * (pprev - pb)
    r = xb[:, 0:bw]
    kb = xb[:, bw:2 * bw]
    vb = xb[:, 2 * bw:3 * bw]
    la = xb[:, 3 * bw:3 * bw + LANES]
    gl = xb[:, 3 * bw + LANES:3 * bw + 2 * LANES]
    lane2 = lax.broadcasted_iota(jnp.int32, la.shape, 1)
    la_act = jnp.where(lane2 < B_LORA_W, jnp.tanh(la), la)
    lw = jnp.dot(la_act.astype(BF16), wla_ref[...], preferred_element_type=F32)
    wlog = -_softplus(-(w0_ref[...] + lw[:, 0:bw])) - 0.5
    decay = jnp.exp(-jnp.exp(wlog))
    a = _sigmoid(a0_ref[...] + lw[:, bw:2 * bw])
    gg = jnp.dot(_sigmoid(gl).astype(BF16), g2_ref[...], preferred_element_type=F32)
    kk = kb * kk_ref[...]
    ss = _dot_split(kk * kk, gs_ref[...], 2)
    kkn = kk * lax.rsqrt(jnp.maximum(ss, 1e-24))
    r_ref[0] = r
    dec_ref[0] = decay
    kt_ref[0] = kb * (1.0 + (a - 1.0) * ka_ref[...])
    vb_ref[0] = vb
    al_ref[0] = -kkn
    be_ref[0] = kkn * a
    gg_ref[0] = gg


def _even_pre(x3, mod3, ng, wcat, cw, cb, gb, mu, w0, a0, wla, g2, k_k, k_a, prev, tt):
    g_, ttot, d = x3.shape
    aw = d // 2
    bw = d // 2
    a2 = 2 * aw
    sw = 3 * bw + B_LORA_W + B_LORA_A + B_LORA_G
    seq = prev is None
    gsum = _group_matrix(bw, B_HEAD, 1.0)
    consts = [ng, wcat, cw, cb, gb, mu, w0, a0, wla, g2, k_k, k_a, gsum]
    ins = [x3, mod3] + consts
    in_specs = [_tok_spec(tt, d), _mod_spec(mod3, tt)] + [_full_spec(c) for c in consts]
    if not seq:
        ins += list(prev)
        in_specs += [_tok_spec(tt, a2)] * 3 + [_tok_spec(tt, sw)]
    nt = ttot // tt
    tail_rows = SUBLANES if seq else tt
    tail_tot = g_ * SUBLANES if seq else ttot

    def tail_spec(width):
        if seq:
            return pl.BlockSpec((1, SUBLANES, width), lambda g, i: (g, 0, 0))
        return _tok_spec(tt, width)

    def tail_shape(width):
        if seq:
            return jax.ShapeDtypeStruct((g_, SUBLANES, width), F32)
        return jax.ShapeDtypeStruct((g_, ttot, width), F32)

    tok = lambda w: jax.ShapeDtypeStruct((g_, ttot, w), F32)
    out_shape = [tok(aw), tok(aw), tok(aw), tok(aw), tok(2 * A_HEADS),
                 jax.ShapeDtypeStruct((g_, 2 * A_HEADS, ttot), F32),
                 tok(bw), tok(bw), tok(bw), tok(bw), tok(bw), tok(bw), tok(bw),
                 tail_shape(a2), tail_shape(sw)]
    out_specs = [_tok_spec(tt, aw)] * 4 + [_tok_spec(tt, 2 * A_HEADS),
                                           pl.BlockSpec((1, 2 * A_HEADS, tt), lambda g, i: (g, 0, i))]
    out_specs += [_tok_spec(tt, bw)] * 7 + [tail_spec(a2), tail_spec(sw)]
    scratch = [pltpu.VMEM((tt + SUBLANES, a2), F32), pltpu.VMEM((tt + SUBLANES, sw), F32)] if seq else []
    del tail_rows, tail_tot
    return pl.pallas_call(
        functools.partial(_even_pre_body, seq, tt, d, aw, bw),
        grid=(g_, nt),
        in_specs=in_specs,
        out_specs=out_specs,
        out_shape=out_shape,
        scratch_shapes=scratch,
        compiler_params=_params(2),
        name="even_pre_seq" if seq else "even_pre_step",
    )(*ins)


def _mlstm_chunk_body(L, q_ref, k_ref, v_ref, g8_ref, g8t_ref, c0_ref, n0_ref, m0_ref,
                      h_ref, c1_ref, n1_ref, m1_ref, C_s, n_s, m_s):
    c = pl.program_id(1)

    @pl.when(c == 0)
    def _():
        C_s[...] = c0_ref[0]
        n_s[...] = n0_ref[0]
        m_s[...] = m0_ref[0]

    row = lax.broadcasted_iota(jnp.int32, (L, L), 0)
    col = lax.broadcasted_iota(jnp.int32, (L, L), 1)
    causal = col <= row
    tril = jnp.where(causal, 1.0, 0.0).astype(BF16)
    triu = jnp.where(row <= col, 1.0, 0.0).astype(BF16)
    g8 = g8_ref[0]
    g8t = g8t_ref[0]
    bcols = _dot_split_left(tril, g8, 3)
    brows = _dot_split(g8t, triu, 3)
    for hd in range(A_HEADS):
        sl = slice(hd * A_DK, (hd + 1) * A_DK)
        q = q_ref[0, :, sl]
        k = k_ref[0, :, sl]
        v = v_ref[0, :, sl]
        i_row = g8t[hd:hd + 1, :]
        i_col = g8[:, hd:hd + 1]
        b_row = brows[A_HEADS + hd:A_HEADS + hd + 1, :]
        b_col = bcols[:, A_HEADS + hd:A_HEADS + hd + 1]
        m_prev = m_s[hd:hd + 1, 0:1]
        C = C_s[hd]
        n = n_s[hd:hd + 1, :]
        dmat = jnp.where(causal, b_col - b_row + i_row, NEG_INF)
        inter = b_col + m_prev
        m_t = jnp.maximum(inter, jnp.max(dmat, axis=1, keepdims=True))
        w_intra = jnp.exp(dmat - m_t)
        w_inter = jnp.exp(inter - m_t)
        s = _dot_nt(q, k) * w_intra
        num = _bdot(s, v) + w_inter * _bdot(q, C)
        den = jnp.sum(s, axis=1, keepdims=True) + w_inter * jnp.sum(q * n, axis=1, keepdims=True)
        h_ref[0, :, sl] = num / jnp.maximum(jnp.abs(den), jnp.exp(-m_t))
        m_new = m_t[L - 1:L, :]
        b_last = b_col[L - 1:L, :]
        kw = k * jnp.exp(b_last - b_col + i_col - m_new)
        carry = jnp.exp(b_last + m_prev - m_new)
        C_s[hd] = carry * C + _dot_tn(kw, v)
        n_s[hd:hd + 1, :] = carry * n + jnp.sum(kw, axis=0, keepdims=True)
        m_s[hd:hd + 1, :] = jnp.broadcast_to(m_new, (1, LANES))

    @pl.when(c == pl.num_programs(1) - 1)
    def _():
        c1_ref[0] = C_s[...]
        n1_ref[0] = n_s[...]
        m1_ref[0] = m_s[...]


def _mlstm_chunk(q, k, v, g8, g8t, C0, n0p, m0p, L):
    b, t, aw = q.shape
    st = lambda *s: pl.BlockSpec((1,) + s, lambda bb, c: (bb,) + (0,) * len(s))
    return pl.pallas_call(
        functools.partial(_mlstm_chunk_body, L),
        grid=(b, t // L),
        in_specs=[_tok_spec(L, aw)] * 3 + [_tok_spec(L, 2 * A_HEADS),
                                           pl.BlockSpec((1, 2 * A_HEADS, L), lambda bb, c: (bb, 0, c)),
                                           st(A_HEADS, A_DK, A_DK), st(SUBLANES, LANES), st(SUBLANES, LANES)],
        out_specs=[_tok_spec(L, aw), st(A_HEADS, A_DK, A_DK), st(SUBLANES, LANES), st(SUBLANES, LANES)],
        out_shape=[jax.ShapeDtypeStruct((b, t, aw), F32), jax.ShapeDtypeStruct(C0.shape, F32),
                   jax.ShapeDtypeStruct(n0p.shape, F32), jax.ShapeDtypeStruct(m0p.shape, F32)],
        scratch_shapes=[pltpu.VMEM((A_HEADS, A_DK, A_DK), F32), pltpu.VMEM((SUBLANES, LANES), F32),
                        pltpu.VMEM((SUBLANES, LANES), F32)],
        compiler_params=_params(2),
        name="mlstm_chunk",
    )(q, k, v, g8, g8t, C0, n0p, m0p)


def _row8(x):
    r = lax.broadcasted_iota(jnp.int32, (SUBLANES, x.shape[1]), 0)
    return jnp.where(r == 0, jnp.broadcast_to(x, (SUBLANES, x.shape[1])), 0.0)


def _mlstm_step_body(bb, q_ref, k_ref, v_ref, g8_ref, n0_ref, m0_ref, c0_ref, *rest):
    h_ref, c1_ref, n1_ref, m1_ref = rest[-4:]
    g8 = g8_ref[...]
    m0 = m0_ref[...]
    for hd in range(A_HEADS):
        sl = slice(hd * A_DK, (hd + 1) * A_DK)
        li = g8[:, hd:hd + 1]
        lf = g8[:, A_HEADS + hd:A_HEADS + hd + 1]
        mp = m0[:, hd:hd + 1]
        m1 = jnp.maximum(lf + mp, li)
        wi = jnp.exp(li - m1)
        wf = jnp.exp(lf + mp - m1)
        floor = jnp.exp(-m1)
        for b in range(bb):
            q = q_ref[b:b + 1, sl]
            k = k_ref[b:b + 1, sl]
            v = v_ref[b:b + 1, sl]
            wib = wi[b:b + 1, :]
            wfb = wf[b:b + 1, :]
            C1 = wfb * c0_ref[0, b, hd] + wib * _dot_tn(_row8(k), _row8(v))
            n1 = wfb * n0_ref[b, hd:hd + 1, :] + wib * k
            num = _bdot(_row8(q), C1)[0:1, :]
            den = jnp.sum(q * n1, axis=1, keepdims=True)
            h_ref[b:b + 1, sl] = num / jnp.maximum(jnp.abs(den), floor[b:b + 1, :])
            c1_ref[0, b, hd] = C1
            n1_ref[b, hd:hd + 1, :] = n1
        m1_ref[:, hd:hd + 1] = m1


def _rwkv_body(bb, tb, ty, al_ref, w_ref, be_ref, kt_ref, r_ref, v_ref, s0_ref, y_ref, s1_ref, S_s, t3_s, y_s):
    @pl.when(pl.program_id(1) == 0)
    def _():
        S_s[...] = s0_ref[...]

    npair = B_HEADS // 2
    ntile = bb * npair
    nrow = min(tb, SUBLANES)
    row = lax.broadcasted_iota(jnp.int32, (B_HEAD, LANES), 0)
    lane = lax.broadcasted_iota(jnp.int32, (B_HEAD, LANES), 1)
    eye_pair = row == (lane % B_HEAD)
    wred = _group_matrix(LANES, B_HEAD, 1.0)
    kk = lax.broadcasted_iota(jnp.int32, (nrow * LANES, LANES), 0)
    ll = lax.broadcasted_iota(jnp.int32, (nrow * LANES, LANES), 1)
    sel_head = ((kk % LANES) // B_HEAD) == (ll // B_HEAD)
    sel_off = (ll % B_HEAD) - (kk // LANES)

    def group(t8, carry):
        base = t8 * nrow if isinstance(t8, int) else pl.multiple_of(t8 * nrow, nrow)
        rows = pl.ds(base, nrow)
        for i in range(nrow):
            ri = slice(i, i + 1)
            lhs = []
            for b in range(bb):
                for p in range(npair):
                    sl = pl.ds(p * LANES, LANES)
                    lhs.append((S_s[b, p] * al_ref[b, rows, sl][ri]).astype(BF16))
            for b in range(bb):
                for p in range(npair):
                    sl = pl.ds(p * LANES, LANES)
                    lhs.append(jnp.where(eye_pair, v_ref[b, rows, sl][ri], 0.0).astype(BF16))
            red = jnp.dot(jnp.concatenate(lhs, axis=0), wred, preferred_element_type=F32)
            for b in range(bb):
                for p in range(npair):
                    sl = pl.ds(p * LANES, LANES)
                    idx = b * npair + p
                    sa = red[idx * B_HEAD:(idx + 1) * B_HEAD, :]
                    vcol = red[(ntile + idx) * B_HEAD:(ntile + idx + 1) * B_HEAD, :]
                    S = (S_s[b, p] * w_ref[b, rows, sl][ri] + sa * be_ref[b, rows, sl][ri]
                         + vcol * kt_ref[b, rows, sl][ri])
                    S_s[b, p] = S
                    t3_s[idx * B_HEAD:(idx + 1) * B_HEAD, i * LANES:(i + 1) * LANES] = (
                        S * r_ref[b, rows, sl][ri]).astype(BF16)
        wsel = jnp.where(sel_head & (sel_off == base % ty), 1.0, 0.0).astype(BF16)
        yg = jnp.dot(t3_s[...], wsel, preferred_element_type=F32)

        @pl.when(base % ty == 0)
        def _():
            y_s[...] = yg

        @pl.when(base % ty != 0)
        def _():
            y_s[...] += yg

        @pl.when((base + nrow) % ty == 0)
        def _():
            y_ref[0, base // ty] = y_s[...]
        return carry

    if tb == nrow:
        group(0, 0)
    else:
        lax.fori_loop(0, tb // nrow, group, 0)

    @pl.when(pl.program_id(1) == pl.num_programs(1) - 1)
    def _():
        s1_ref[...] = S_s[...]


def _rwkv_scan(al, w, be, kt, r, v, S0p, bb, tb):
    b, t, bw = al.shape
    npair = B_HEADS // 2
    ty = min(B_HEAD, t)
    nb = b // bb
    tok = pl.BlockSpec((bb, tb, bw), lambda i, j: (i, j, 0))
    st = pl.BlockSpec((bb, npair, B_HEAD, LANES), lambda i, j: (i, 0, 0, 0))
    nrow = min(tb, SUBLANES)
    rows = bb * npair * B_HEAD
    yt, S1p = pl.pallas_call(
        functools.partial(_rwkv_body, bb, tb, ty),
        grid=(nb, t // tb),
        in_specs=[tok] * 6 + [st],
        out_specs=[pl.BlockSpec((1, tb // ty, rows, LANES), lambda i, j: (i, j, 0, 0)), st],
        out_shape=[jax.ShapeDtypeStruct((nb, t // ty, rows, LANES), F32), jax.ShapeDtypeStruct(S0p.shape, F32)],
        scratch_shapes=[pltpu.VMEM((bb, npair, B_HEAD, LANES), F32), pltpu.VMEM((rows, nrow * LANES), BF16),
                        pltpu.VMEM((rows, LANES), F32)],
        compiler_params=_params(2),
        name="rwkv_scan",
    )(al, w, be, kt, r, v, S0p)
    y = jnp.stack([yt[..., 0:ty], yt[..., B_HEAD:B_HEAD + ty]], axis=3)
    y = y.reshape(nb, t // ty, bb, npair, B_HEAD, 2, ty).transpose(0, 2, 1, 6, 3, 5, 4).reshape(b, t, bw)
    return y, S1p


def _even_post_body(d, aw, bw, x_ref, mod_ref, ha_ref, so_ref, y_ref, r_ref, kt_ref, vb_ref, gg_ref,
                    ang_ref, lng_ref, lnb_ref, rk_ref, wo_ref, o_ref):
    ga = _group_matrix(aw, A_DK, 1.0 / A_DK)
    gb_mean = _group_matrix(bw, B_HEAD, 1.0 / B_HEAD)
    gb_sum = _group_matrix(bw, B_HEAD, 1.0)
    ha = ha_ref[0]
    ha = ha * lax.rsqrt(_dot_split(ha * ha, ga, 2) + EPS) * ang_ref[...] * so_ref[0]
    y = y_ref[0]
    yc = y - _dot_split(y, gb_mean, 2)
    var = _dot_split(yc * yc, gb_mean, 2)
    yn = yc * lax.rsqrt(var + B_LN_EPS) * lng_ref[...] + lnb_ref[...]
    bonus = _dot_split(r_ref[0] * kt_ref[0] * rk_ref[...], gb_sum, 2)
    y2 = (yn + bonus * vb_ref[0]) * gg_ref[0]
    out = (jnp.dot(ha.astype(BF16), wo_ref[0:aw, :], preferred_element_type=F32)
           + jnp.dot(y2.astype(BF16), wo_ref[aw:aw + bw, :], preferred_element_type=F32))
    o_ref[0] = x_ref[0] + mod_ref[0][:, 2 * d:3 * d] * out


def _even_post(x3, mod3, ha, so, y, r, kt, vb, gg, ang, lng, lnb, rk, wo, tt):
    g_, ttot, d = x3.shape
    aw = d // 2
    bw = d // 2
    consts = [ang, lng, lnb, rk, wo]
    return pl.pallas_call(
        functools.partial(_even_post_body, d, aw, bw),
        grid=(g_, ttot // tt),
        in_specs=[_tok_spec(tt, d), _mod_spec(mod3, tt)] + [_tok_spec(tt, aw)] * 7 + [_full_spec(c) for c in consts],
        out_specs=_tok_spec(tt, d),
        out_shape=jax.ShapeDtypeStruct(x3.shape, F32),
        compiler_params=_params(2),
        name="even_post",
    )(x3, mod3, ha, so, y, r, kt, vb, gg, *consts)


def _odd_pre_body(j, d, x_ref, mod_ref, ng_ref, w_ref, lbl_ref, q_ref, k_ref, v_ref, lf_ref, sg_ref):
    x = x_ref[0]
    mod = mod_ref[0]
    h = _rmsnorm(x, ng_ref[...]) * (1.0 + mod[:, d:2 * d]) + mod[:, 0:d]
    p = jnp.dot(h.astype(BF16), w_ref[...], preferred_element_type=F32)
    lbl = lbl_ref[...]
    n_odd = lbl.shape[0]
    mx = lbl[0:1, :]
    for i in range(1, n_odd):
        mx = jnp.maximum(mx, lbl[i:i + 1, :])
    ex = [jnp.exp(lbl[i:i + 1, :] - mx) for i in range(n_odd)]
    tot = ex[0]
    for i in range(1, n_odd):
        tot = tot + ex[i]
    lb = jnp.zeros_like(mx)
    for i in range(1, j + 1):
        lb = lb + ex[i] / tot
    f = lb + (1.0 - lb) * _sigmoid(p[:, d:2 * d])
    q_ref[0] = _silu(p[:, 0:d])
    k_ref[0] = 1.0 - f
    v_ref[0] = p[:, 2 * d:3 * d]
    lf_ref[0] = jnp.log(f)
    sg_ref[0] = _silu(p[:, 3 * d:4 * d])


def _odd_pre(x3, mod3, ng, w_in, lbl, j, tt):
    g_, ttot, d = x3.shape
    consts = [ng, w_in, lbl]
    return pl.pallas_call(
        functools.partial(_odd_pre_body, j, d),
        grid=(g_, ttot // tt),
        in_specs=[_tok_spec(tt, d), _mod_spec(mod3, tt)] + [_full_spec(c) for c in consts],
        out_specs=[_tok_spec(tt, d)] * 5,
        out_shape=[jax.ShapeDtypeStruct(x3.shape, F32)] * 5,
        compiler_params=_params(2),
        name="odd_pre",
    )(x3, mod3, *consts)


def _col_bcast(row, terms):
    parts = _split(row, terms)
    r = lax.broadcasted_iota(jnp.int32, (SUBLANES, row.shape[1]), 0)
    lhs = jnp.zeros((SUBLANES, row.shape[1]), F32)
    for i, p in enumerate(parts):
        lhs = jnp.where(r == i, jnp.broadcast_to(p.astype(F32), lhs.shape), lhs)
    ones = jnp.where(lax.broadcasted_iota(jnp.int32, (SUBLANES, LANES), 0) < terms, 1.0, 0.0).astype(BF16)
    return lax.dot_general(lhs.astype(BF16), ones, (((0,), (0,)), ((), ())), preferred_element_type=F32)


def _hgrn_chunk_body(L, sub, nchunk, hg, q_ref, k_ref, v_ref, g_ref, s0_ref, o_ref, s1_ref, S_s):
    @pl.when(pl.program_id(2) == 0)
    def _():
        S_s[...] = s0_ref[0]

    nj = L // sub
    rowi = lax.broadcasted_iota(jnp.int32, (L, C_DK), 0)
    rr = lax.broadcasted_iota(jnp.int32, (L, L), 0)
    cc = lax.broadcasted_iota(jnp.int32, (L, L), 1)
    tril = jnp.where(cc <= rr, 1.0, 0.0).astype(BF16)
    ones = jnp.ones((C_DK, LANES), BF16)

    def chunk(ci, carry):
        base = pl.multiple_of(ci * L, L)
        rows = pl.ds(base, L)
        b_all = _dot_split_left(tril, g_ref[0, rows, :], 3)
        heads = []
        for hd in range(hg):
            sl = slice(hd * C_DK, (hd + 1) * C_DK)
            heads.append((q_ref[0, rows, sl], k_ref[0, rows, sl], v_ref[0, rows, sl], b_all[:, sl], S_s[hd]))
        outs = [_bdot(q * jnp.exp(b), S) for (q, k, v, b, S) in heads]
        for dlt in range(sub):
            prs, vss = [], []
            for (q, k, v, b, S) in heads:
                if dlt == 0:
                    prs.append((q * k).astype(BF16))
                    vss.append(v)
                else:
                    ok = (rowi % sub) >= dlt
                    e = jnp.exp(jnp.where(ok, b - pltpu.roll(b, dlt, 0), NEG_INF))
                    prs.append((q * pltpu.roll(k, dlt, 0) * e).astype(BF16))
                    vss.append(pltpu.roll(v, dlt, 0))
            rs = jnp.dot(jnp.concatenate(prs, axis=0), ones, preferred_element_type=F32)
            outs = [o + rs[i * L:(i + 1) * L, :] * vs for i, (o, vs) in enumerate(zip(outs, vss))]
        for hd, ((q, k, v, b, S), o) in enumerate(zip(heads, outs)):
            if nj > 1:
                amat = None
                for jj in range(nj - 1):
                    end = (jj + 1) * sub
                    bref = b[end - 1:end, :]
                    qj = q * jnp.exp(jnp.where(rowi >= end, b - bref, NEG_INF))
                    kj = k * jnp.exp(jnp.where((rowi >= end - sub) & (rowi < end), bref - b, NEG_INF))
                    a = _dot_nt(qj, kj)
                    amat = a if amat is None else amat + a
                o = o + _bdot(amat, v)
            o_ref[0, rows, hd * C_DK:(hd + 1) * C_DK] = o
            b_last = b[L - 1:L, :]
            S_s[hd] = _col_bcast(jnp.exp(b_last), 3) * S + _dot_tn(k * jnp.exp(b_last - b), v)
        return carry

    lax.fori_loop(0, nchunk, chunk, 0)

    @pl.when(pl.program_id(2) == pl.num_programs(2) - 1)
    def _():
        s1_ref[0] = S_s[...]


def _hgrn_chunk(q, k, v, lf, S0, tblk, L, sub, hg):
    b, t, d = q.shape
    tok = pl.BlockSpec((1, tblk, hg * C_DK), lambda bb, h, c: (bb, c, h))
    st = pl.BlockSpec((1, hg, C_DK, C_DK), lambda bb, h, c: (bb, h, 0, 0))
    return pl.pallas_call(
        functools.partial(_hgrn_chunk_body, L, sub, tblk // L, hg),
        grid=(b, C_HEADS // hg, t // tblk),
        in_specs=[tok] * 4 + [st],
        out_specs=[tok, st],
        out_shape=[jax.ShapeDtypeStruct((b, t, d), F32), jax.ShapeDtypeStruct(S0.shape, F32)],
        scratch_shapes=[pltpu.VMEM((hg, C_DK, C_DK), F32)],
        compiler_params=_params(3),
        name="hgrn_chunk",
    )(q, k, v, lf, S0)


def _hgrn_step_body(bb, q_ref, k_ref, v_ref, g_ref, s0_ref, *rest):
    o_ref, s1_ref = rest[-2], rest[-1]
    for b in range(bb):
        for hd in range(C_HEADS):
            sl = slice(hd * C_DK, (hd + 1) * C_DK)
            q = q_ref[b:b + 1, sl]
            k = k_ref[b:b + 1, sl]
            v = v_ref[b:b + 1, sl]
            f = jnp.exp(g_ref[b:b + 1, sl])
            S1 = _col_bcast(f, 3) * s0_ref[0, b, hd] + _dot_tn(_row8(k), _row8(v))
            o_ref[b:b + 1, sl] = _bdot(_row8(q), S1)[0:1, :]
            s1_ref[0, b, hd] = S1


def _layer_state_call(body, name, rows_in, row_out_shape, row_spec, state_all, layer, state_acc, bb, extra_out=()):
    b = state_all.shape[1]
    blk = (1, bb) + state_all.shape[2:]
    zeros = (0,) * (state_all.ndim - 2)
    st = pl.BlockSpec(blk, lambda i: (layer, i) + zeros)
    ins = list(rows_in) + [state_all]
    in_specs = [row_spec(r) for r in rows_in] + [st]
    aliases = {}
    if state_acc is not None:
        ins.append(state_acc)
        in_specs.append(pl.BlockSpec(memory_space=pl.ANY))
        aliases = {len(ins) - 1: 1}
    return pl.pallas_call(
        body,
        grid=(b // bb,),
        in_specs=in_specs,
        out_specs=[row_spec(row_out_shape), st] + [row_spec(e) for e in extra_out],
        out_shape=[jax.ShapeDtypeStruct(row_out_shape.shape, F32), jax.ShapeDtypeStruct(state_all.shape, F32)]
        + [jax.ShapeDtypeStruct(e.shape, F32) for e in extra_out],
        input_output_aliases=aliases,
        compiler_params=_params(1),
        name=name,
    )(*ins)


def _row_spec_for(bb):
    def spec(arr):
        blk = (bb,) + arr.shape[1:]
        zeros = (0,) * (arr.ndim - 1)
        return pl.BlockSpec(blk, lambda i: (i,) + zeros)
    return spec


def _mlstm_step(q, k, v, g8, n0, m0, C_all, layer, C_acc, bb):
    return _layer_state_call(functools.partial(_mlstm_step_body, bb), "mlstm_step", [q, k, v, g8, n0, m0], q,
                             _row_spec_for(bb), C_all, layer, C_acc, bb, extra_out=(n0, m0))


def _hgrn_step(q, k, v, lf, S_all, layer, S_acc, bb):
    return _layer_state_call(functools.partial(_hgrn_step_body, bb), "hgrn_step", [q, k, v, lf], q,
                             _row_spec_for(bb), S_all, layer, S_acc, bb)


def _odd_post_body(d, x_ref, mod_ref, o_ref, sg_ref, ng_ref, wo_ref, out_ref):
    gm = _group_matrix(d, C_DK, 1.0 / C_DK)
    o = o_ref[0]
    o = o * lax.rsqrt(_dot_split(o * o, gm, 2) + EPS) * ng_ref[...] * sg_ref[0]
    out = jnp.dot(o.astype(BF16), wo_ref[...], preferred_element_type=F32)
    out_ref[0] = x_ref[0] + mod_ref[0][:, 2 * d:3 * d] * out


def _odd_post(x3, mod3, o, sg, ng_tiled, wo, tt):
    g_, ttot, d = x3.shape
    consts = [ng_tiled, wo]
    return pl.pallas_call(
        functools.partial(_odd_post_body, d),
        grid=(g_, ttot // tt),
        in_specs=[_tok_spec(tt, d), _mod_spec(mod3, tt)] + [_tok_spec(tt, d)] * 2 + [_full_spec(c) for c in consts],
        out_specs=_tok_spec(tt, d),
        out_shape=jax.ShapeDtypeStruct(x3.shape, F32),
        compiler_params=_params(2),
        name="odd_post",
    )(x3, mod3, o, sg, *consts)


P_NCAND = P_TOPK + 1
P_CAND_ROWS = 3 * SUBLANES


P_RANK_OUT = 99.0


def _top_sorted(s, want_rank):
    ridx = lax.broadcasted_iota(jnp.int32, (P_CAND_ROWS, s.shape[1]), 0)
    acc = jnp.full((P_CAND_ROWS, s.shape[1]), NEG_INF, F32)
    rank = jnp.full(s.shape, P_RANK_OUT, F32) if want_rank else None
    cur = s
    for i in range(P_NCAND):
        m = jnp.max(cur, axis=0, keepdims=True)
        acc = jnp.where(ridx == i, m, acc)
        hit = cur == m
        if want_rank:
            rank = jnp.where(hit, float(i), rank)
        cur = jnp.where(hit, NEG_INF, cur)
    return acc, rank


def _peer_body(tm, te, ne, d, x_ref, mod_ref, ng_ref, wqt_ref, keys_ref, u_ref, vt_ref, o_ref,
               ht_s, n1_s, e1_s, r2_s, e2_s, gate_s, w_s, acc_s):
    j = pl.program_id(2)
    dk2 = keys_ref.shape[4]
    nsub = te // P_NKEYS

    @pl.when(j == 0)
    def _():
        x = x_ref[0]
        mod = mod_ref[0]
        h = _rmsnorm(x, ng_ref[...]) * (1.0 + mod[:, 4 * d:5 * d]) + mod[:, 3 * d:4 * d]
        ht = h.T.astype(BF16)
        ht_s[...] = ht
        qt = jnp.dot(wqt_ref[0], ht, preferred_element_type=F32)
        r8 = lax.broadcasted_iota(jnp.int32, (SUBLANES, LANES), 0)
        for hd in range(P_HEADS):
            s1_all = _bdot(keys_ref[0, hd, 0], qt[(2 * hd) * dk2:(2 * hd + 1) * dk2, :])
            s2_all = _bdot(keys_ref[0, hd, 1], qt[(2 * hd + 1) * dk2:(2 * hd + 2) * dk2, :])
            for tc in range(tm // LANES):
                ls = slice(tc * LANES, (tc + 1) * LANES)
                s1 = s1_all[:, ls]
                s2 = s2_all[:, ls]
                a, _ = _top_sorted(s1, False)
                b, rank2 = _top_sorted(s2, True)
                blocks = [a[0:1, :] + b]
                for i in range(2, SUBLANES + 1):
                    blocks.append(jnp.where(r8 < P_NCAND // i, a[i - 1:i, :] + b[0:SUBLANES, :], NEG_INF))
                blocks.append(a[SUBLANES:P_CAND_ROWS, :] + b[0:1, :])
                cand = jnp.concatenate(blocks, axis=0)
                cur = cand
                best = None
                for _ in range(P_NCAND):
                    prev, best = best, jnp.max(cur, axis=0, keepdims=True)
                    cur = jnp.where(cur == best, NEG_INF, cur)
                tau = 0.5 * (prev + best)
                top = a[0:1, :] + b[0:1, :]
                z = jnp.sum(jnp.where(cand >= tau, jnp.exp(cand - top), 0.0), axis=0, keepdims=True)
                n1 = jnp.zeros_like(s1)
                for i in range(P_NCAND):
                    n1 = jnp.where(s1 >= tau - b[i:i + 1, :], float(i + 1), n1)
                n1_s[hd, :, ls] = n1
                e1_s[hd, :, ls] = jnp.exp(s1 - a[0:1, :])
                r2_s[hd, :, ls] = rank2.astype(BF16)
                e2_s[hd, :, ls] = (jnp.exp(s2 - b[0:1, :]) / z).astype(BF16)
        acc_s[...] = jnp.zeros_like(acc_s)
        w_s[1] = jnp.zeros((te, tm), BF16)

    assert nsub == SUBLANES
    jj = jnp.minimum(j, ne - 1)
    grp = pl.ds(pl.multiple_of(jj * nsub, nsub), nsub)

    def gates(tc):
        ls = slice(tc * LANES, (tc + 1) * LANES)
        n1g = [n1_s[hd, grp, ls] for hd in range(P_HEADS)]
        e1g = [e1_s[hd, grp, ls] for hd in range(P_HEADS)]
        zero = jnp.zeros((P_NKEYS, LANES), BF16)
        for ii in range(nsub):
            rows = slice(ii * P_NKEYS, (ii + 1) * P_NKEYS)
            gate = zero
            for hd in range(P_HEADS):
                keep = r2_s[hd, :, ls] < n1g[hd][ii:ii + 1, :].astype(BF16)
                gate = gate + jnp.where(keep, e2_s[hd, :, ls], zero) * e1g[hd][ii:ii + 1, :].astype(BF16)
            gate_s[rows, ls] = gate

    piece = min(tm, MXU_WIDTH)
    tc_per_piece = piece // LANES
    for pc in range(tm // piece):
        cs = slice(pc * piece, (pc + 1) * piece)
        acc_s[:, cs] += jnp.dot(vt_ref[0], w_s[(j + 1) % 2, :, cs], preferred_element_type=F32)
        for tc in range(pc * tc_per_piece, pc * tc_per_piece + (tc_per_piece + 1) // 2):
            gates(tc)
        act = jnp.dot(u_ref[0], ht_s[:, cs], preferred_element_type=F32)
        for tc in range(pc * tc_per_piece + (tc_per_piece + 1) // 2, (pc + 1) * tc_per_piece):
            gates(tc)
        w_s[j % 2, :, cs] = gate_s[:, cs] * _gelu_tanh(act).astype(BF16)

    @pl.when(j == ne)
    def _():
        o_ref[0] = x_ref[0] + mod_ref[0][:, 5 * d:6 * d] * acc_s[...].T


def _peer(x3, mod3, ng, wqt_all, keys_all, u_all, vt_all, layer, tm, te):
    g_, ttot, d = x3.shape
    ne = u_all.shape[1] // te
    nk = P_NKEYS
    if mod3.shape[1] == 1:
        mod_spec = pl.BlockSpec((1, 1, mod3.shape[2]), lambda g, i, j: (g, 0, 0))
    else:
        mod_spec = pl.BlockSpec((1, tm, mod3.shape[2]), lambda g, i, j: (g, i, 0))
    return pl.pallas_call(
        functools.partial(_peer_body, tm, te, ne, d),
        grid=(g_, ttot // tm, ne + 1),
        in_specs=[pl.BlockSpec((1, tm, d), lambda g, i, j: (g, i, 0)), mod_spec,
                  pl.BlockSpec(ng.shape, lambda g, i, j: (0, 0)),
                  pl.BlockSpec((1,) + wqt_all.shape[1:], lambda g, i, j: (layer, 0, 0)),
                  pl.BlockSpec((1,) + keys_all.shape[1:], lambda g, i, j: (layer, 0, 0, 0, 0)),
                  pl.BlockSpec((1, te, d), lambda g, i, j: (layer, jnp.minimum(j, ne - 1), 0)),
                  pl.BlockSpec((1, d, te), lambda g, i, j: (layer, 0, jnp.maximum(j - 1, 0)))],
        out_specs=pl.BlockSpec((1, tm, d), lambda g, i, j: (g, i, 0)),
        out_shape=jax.ShapeDtypeStruct(x3.shape, F32),
        scratch_shapes=[pltpu.VMEM((d, tm), BF16),
                        pltpu.VMEM((P_HEADS, nk, tm), F32), pltpu.VMEM((P_HEADS, nk, tm), F32),
                        pltpu.VMEM((P_HEADS, nk, tm), BF16), pltpu.VMEM((P_HEADS, nk, tm), BF16),
                        pltpu.VMEM((te, tm), BF16), pltpu.VMEM((2, te, tm), BF16), pltpu.VMEM((d, tm), F32)],
        compiler_params=_params(3),
        name="peer",
    )(x3, mod3, ng, wqt_all, keys_all, u_all, vt_all)


def _final_norm_body(x_ref, g_ref, o_ref):
    o_ref[0] = _rmsnorm(x_ref[0], g_ref[...])


def _final_norm(x3, g, tt):
    g_, ttot, d = x3.shape
    return pl.pallas_call(
        _final_norm_body,
        grid=(g_, ttot // tt),
        in_specs=[_tok_spec(tt, d), _full_spec(g)],
        out_specs=_tok_spec(tt, d),
        out_shape=jax.ShapeDtypeStruct(x3.shape, F32),
        compiler_params=_params(2),
        name="final_norm",
    )(x3, g)


def _pair_state(S):
    b = S.shape[0]
    return S.reshape(b, B_HEADS // 2, 2, B_HEAD, B_HEAD).transpose(0, 1, 3, 2, 4).reshape(b, B_HEADS // 2, B_HEAD, LANES)


def _unpair_state(Sp):
    b = Sp.shape[0]
    return Sp.reshape(b, B_HEADS // 2, B_HEAD, 2, B_HEAD).transpose(0, 1, 3, 2, 4).reshape(b, B_HEADS, B_HEAD, B_HEAD)


def _trunk(x, mod_all, st, w, seq):
    bsz, t, d = x.shape
    depth = mod_all.shape[0]
    aw = d // 2
    bw = d // 2
    if seq:
        x3 = x
        tt = min(256, t)
        tm = min(512, t)
        mods = [mod_all[l][:, None, :] for l in range(depth)]
    else:
        x3 = x.reshape(1, bsz * t, d)
        tt = bsz * t
        tm = bsz * t
        mods = [mod_all[l][None] for l in range(depth)]
    row = lambda v_: v_.reshape(1, -1)
    new_even, new_odd = [], []
    C_acc = S_acc = None
    for l in range(depth):
        j = l // 2
        if l % 2 == 0:
            e = w["even"][j]
            if seq:
                prev = None
            else:
                conv0, shift0 = st[3][j], st[5][j]
                prev = (conv0[:, 0][None], conv0[:, 1][None], conv0[:, 2][None], shift0[:, 0][None])
            (q, k, v, so, g8, g8t, r, dec, kt, vb, al, be, gg, ctail, stail) = _even_pre(
                x3, mods[l], row(w["norm_mix_g"][l]), e["wcat"], e["conv_w"], row(e["conv_b"]), e["gate_b"],
                row(e["mu"]), row(e["w0"]), row(e["a0"]), e["wla"], e["g2"], row(e["k_k"]), row(e["k_a"]), prev, tt)
            if seq:
                C0 = jnp.zeros((bsz, A_HEADS, A_DK, A_DK), F32)
                nm0 = jnp.zeros((bsz, SUBLANES, LANES), F32)
                ha, C1, n1p, m1p = _mlstm_chunk(q, k, v, g8, g8t, C0, nm0, nm0, min(256, t))
                n1 = n1p[:, :A_HEADS, :]
                m1 = m1p[:, :A_HEADS, 0]
                S0p = jnp.zeros((bsz, B_HEADS // 2, B_HEAD, LANES), F32)
                y, S1p = _rwkv_scan(al, dec, be, kt, r, vb, S0p, min(8, bsz), min(128, t))
                conv1 = ctail[:, SUBLANES - (A_CONV - 1):, :]
                shift1 = stail[:, SUBLANES - 1:, :]
            else:
                ha2, C_acc, n1, m1 = _mlstm_step(q[0], k[0], v[0], g8[0], st[1][j], st[2][j], st[0], j, C_acc, 8)
                C1 = None
                ha = ha2[None]
                tok = lambda a_: a_[0][:, None, :]
                y2, S1p = _rwkv_scan(tok(al), tok(dec), tok(be), tok(kt), tok(r), tok(vb), _pair_state(st[4][j]), 8, 1)
                y = y2.reshape(1, bsz, bw)
                conv1 = jnp.concatenate([st[3][j][:, 1:], ctail[0][:, None, :]], axis=1)
                shift1 = stail[0][:, None, :]
            x3 = _even_post(x3, mods[l], ha, so, y, r, kt, vb, gg, row(e["a_norm_g"]), row(e["ln_g"]),
                            row(e["ln_b"]), row(e["r_k"]), e["w_out"], tt)
            new_even.append((C1, n1, m1, conv1, _unpair_state(S1p), shift1))
        else:
            o_ = w["odd"][j]
            q, k, v, lf, sg = _odd_pre(x3, mods[l], row(w["norm_mix_g"][l]), o_["w_in"], w["lb_logits"], j, tt)
            if seq:
                S0 = jnp.zeros((bsz, C_HEADS, C_DK, C_DK), F32)
                o, S1 = _hgrn_chunk(q, k, v, lf, S0, min(256, t), min(64, t), min(16, t), 4)
            else:
                o2, S_acc = _hgrn_step(q[0], k[0], v[0], lf[0], st[6], j, S_acc, 8)
                S1 = None
                o = o2[None]
            x3 = _odd_post(x3, mods[l], o, sg, row(jnp.tile(o_["norm_g"], C_HEADS)), o_["w_out"], tt)
            new_odd.append(S1)
        pw = w["peer"]
        x3 = _peer(x3, mods[l], row(w["norm_ffn_g"][l]), pw["wqt"], pw["keys"], pw["u"], pw["vt"], l, tm, 1024)
    y = _final_norm(x3, row(w["norm_final_g"]), tt).reshape(bsz, t, d)
    ev = [None if new_even[0][i] is None else jnp.stack([s[i] for s in new_even]) for i in range(6)]
    if seq:
        return (y, ev[0], ev[1], ev[2], ev[3], ev[4], ev[5], jnp.stack(new_odd))
    return (y, C_acc, ev[1], ev[2], ev[3], ev[4], ev[5], S_acc)


def _prepare_weights(norm_mix_g, norm_ffn_g, norm_final_g, even_w_in, even_w_out, mlstm_conv_w, mlstm_conv_b,
                     mlstm_gate_b, mlstm_norm_g, rwkv_mu, rwkv_w0, rwkv_w2, rwkv_a0, rwkv_a2, rwkv_g2, rwkv_k_k,
                     rwkv_k_a, rwkv_r_k, rwkv_ln_g, rwkv_ln_b, odd_w_in, odd_w_out, hgrn_lb_logits, hgrn_norm_g,
                     peer_w_q, peer_keys, peer_u, peer_v):
    d = even_w_in.shape[1]
    aw = d // 2
    bw = d // 2
    sw = 3 * bw + B_LORA_W + B_LORA_A + B_LORA_G
    even = []
    for j in range(even_w_in.shape[0]):
        wi = even_w_in[j]
        gates_w = wi[:, 4 * aw:4 * aw + 2 * A_HEADS]
        wcat = jnp.concatenate([wi[:, 0:4 * aw], wi[:, 4 * aw + 2 * A_HEADS:],
                                jnp.pad(gates_w, ((0, 0), (0, LANES - 2 * A_HEADS)))], axis=1).astype(BF16)
        assert wcat.shape[1] == 4 * aw + sw + LANES
        wla = jnp.zeros((B_LORA_W + B_LORA_A, 2 * bw), F32)
        wla = wla.at[:B_LORA_W, :bw].set(rwkv_w2[j]).at[B_LORA_W:, bw:].set(rwkv_a2[j]).astype(BF16)
        even.append(dict(
            wcat=wcat, conv_w=mlstm_conv_w[j], conv_b=mlstm_conv_b[j],
            gate_b=jnp.pad(mlstm_gate_b[j], (0, LANES - 2 * A_HEADS)).reshape(1, LANES),
            mu=rwkv_mu[j], w0=rwkv_w0[j], a0=rwkv_a0[j], wla=wla, g2=rwkv_g2[j].astype(BF16),
            k_k=rwkv_k_k[j], k_a=rwkv_k_a[j], r_k=rwkv_r_k[j], ln_g=rwkv_ln_g[j], ln_b=rwkv_ln_b[j],
            a_norm_g=mlstm_norm_g[j], w_out=even_w_out[j].astype(BF16)))
    odd = [dict(w_in=odd_w_in[j].astype(BF16), w_out=odd_w_out[j].astype(BF16), norm_g=hgrn_norm_g[j])
           for j in range(odd_w_in.shape[0])]
    peer = dict(wqt=jnp.swapaxes(peer_w_q, 1, 2).astype(BF16), keys=peer_keys.astype(BF16), u=peer_u.astype(BF16),
                vt=jnp.swapaxes(peer_v, 1, 2).astype(BF16))
    return dict(norm_mix_g=norm_mix_g, norm_ffn_g=norm_ffn_g, norm_final_g=norm_final_g, even=even, odd=odd,
                peer=peer, lb_logits=hgrn_lb_logits)


def kernel(x_prompt, x_sample, c_prompt, c_sample, state_mlstm_C, state_mlstm_n, state_mlstm_m, state_mlstm_conv, state_rwkv_S, state_rwkv_shift, state_hgrn_S, norm_mix_g, norm_ffn_g, norm_final_g, ada_w, ada_b, even_w_in, even_w_out, mlstm_conv_w, mlstm_conv_b, mlstm_gate_b, mlstm_norm_g, rwkv_mu, rwkv_w0, rwkv_w2, rwkv_a0, rwkv_a2, rwkv_g2, rwkv_k_k, rwkv_k_a, rwkv_r_k, rwkv_ln_g, rwkv_ln_b, odd_w_in, odd_w_out, hgrn_lb_logits, hgrn_norm_g, peer_w_q, peer_keys, peer_u, peer_v):
    w = _prepare_weights(norm_mix_g, norm_ffn_g, norm_final_g, even_w_in, even_w_out, mlstm_conv_w, mlstm_conv_b,
                         mlstm_gate_b, mlstm_norm_g, rwkv_mu, rwkv_w0, rwkv_w2, rwkv_a0, rwkv_a2, rwkv_g2, rwkv_k_k,
                         rwkv_k_a, rwkv_r_k, rwkv_ln_g, rwkv_ln_b, odd_w_in, odd_w_out, hgrn_lb_logits, hgrn_norm_g,
                         peer_w_q, peer_keys, peer_u, peer_v)
    bp = x_prompt.shape[0]
    mod = _ada(jnp.concatenate([c_prompt, c_sample], axis=0), ada_w, ada_b)
    out_p = _trunk(x_prompt, mod[:, :bp], None, w, True)
    st = (state_mlstm_C, state_mlstm_n, state_mlstm_m, state_mlstm_conv, state_rwkv_S, state_rwkv_shift,
          state_hgrn_S)
    out_s = _trunk(x_sample, mod[:, bp:], st, w, False)
    return (out_p[0], out_s[0]) + tuple(out_p[1:]) + tuple(out_s[1:])
```

```python
import functools
import math

import jax
import jax.numpy as jnp
from jax import lax
from jax.experimental import pallas as pl
from jax.experimental.pallas import tpu as pltpu

F32 = jnp.float32
BF16 = jnp.bfloat16

EPS = 1e-6
A_HEADS = 4
A_DK = 128
A_CONV = 4
B_HEADS = 8
B_HEAD = 64
B_LORA_W = 64
B_LORA_A = 64
B_LORA_G = 128
B_LN_EPS = 64e-5
C_HEADS = 8
C_DK = 128
P_HEADS = 8
P_NKEYS = 128
P_TOPK = 16

LANES = 128
SUBLANES = 8
MXU_WIDTH = 256
VMEM_LIMIT = 56 * 1024 * 1024
NEG_INF = float("-inf")


def _bdot(a, b):
    return jnp.dot(a.astype(BF16), b.astype(BF16), preferred_element_type=F32)


def _dot_nt(a, b):
    return lax.dot_general(a.astype(BF16), b.astype(BF16), (((1,), (1,)), ((), ())), preferred_element_type=F32)


def _dot_tn(a, b):
    return lax.dot_general(a.astype(BF16), b.astype(BF16), (((0,), (0,)), ((), ())), preferred_element_type=F32)


def _split(x, terms):
    parts = []
    rem = x
    for _ in range(terms):
        p = rem.astype(BF16)
        parts.append(p)
        rem = rem - p.astype(F32)
    return parts


def _dot_split(a, b_bf, terms):
    out = None
    for p in _split(a, terms):
        d = jnp.dot(p, b_bf, preferred_element_type=F32)
        out = d if out is None else out + d
    return out


def _dot_split_left(a_bf, b, terms):
    out = None
    for p in _split(b, terms):
        d = jnp.dot(a_bf, p, preferred_element_type=F32)
        out = d if out is None else out + d
    return out


def _sigmoid(x):
    return 1.0 / (1.0 + jnp.exp(-x))


def _silu(x):
    return x * _sigmoid(x)


def _softplus(x):
    return jnp.maximum(x, 0.0) + jnp.log(1.0 + jnp.exp(-jnp.abs(x)))


def _gelu_tanh(x):
    return 0.5 * x * (1.0 + jnp.tanh(math.sqrt(2.0 / math.pi) * (x + 0.044715 * (x * x * x))))


def _rmsnorm(x, g):
    return x * lax.rsqrt(jnp.mean(x * x, axis=-1, keepdims=True) + EPS) * g


def _group_matrix(n, group, value):
    r = lax.broadcasted_iota(jnp.int32, (n, n), 0) // group
    c = lax.broadcasted_iota(jnp.int32, (n, n), 1) // group
    return jnp.where(r == c, value, 0.0).astype(BF16)


def _full_spec(arr):
    nd = arr.ndim
    return pl.BlockSpec(arr.shape, lambda *_: (0,) * nd)


def _params(n_grid):
    return pltpu.CompilerParams(dimension_semantics=("arbitrary",) * n_grid, vmem_limit_bytes=VMEM_LIMIT)


def _tok_spec(tt, width):
    return pl.BlockSpec((1, tt, width), lambda g, i: (g, i, 0))


def _mod_spec(mod3, tt):
    width = mod3.shape[2]
    if mod3.shape[1] == 1:
        return pl.BlockSpec((1, 1, width), lambda g, i: (g, 0, 0))
    return pl.BlockSpec((1, tt, width), lambda g, i: (g, i, 0))


def _ada_body(c_ref, w_ref, b_ref, o_ref):
    c = c_ref[...]
    o_ref[0] = _bdot(_silu(c), w_ref[0]) + b_ref[0]


def _ada(c_all, ada_w, ada_b):
    depth, d, n6 = ada_w.shape
    bc = c_all.shape[0]
    tn = n6 // 4
    return pl.pallas_call(
        _ada_body,
        grid=(depth, n6 // tn),
        in_specs=[pl.BlockSpec((bc, d), lambda l, j: (0, 0)),
                  pl.BlockSpec((1, d, tn), lambda l, j: (l, 0, j)),
                  pl.BlockSpec((1, 1, tn), lambda l, j: (l, 0, j))],
        out_specs=pl.BlockSpec((1, bc, tn), lambda l, j: (l, 0, j)),
        out_shape=jax.ShapeDtypeStruct((depth, bc, n6), F32),
        compiler_params=_params(2),
        name="ada_mod",
    )(c_all, ada_w, ada_b.reshape(depth, 1, n6))


def _even_pre_body(seq, tt, d, aw, bw, *refs):
    n_in = 15 if seq else 19
    ins, outs = refs[:n_in], refs[n_in:]
    (x_ref, mod_ref, ng_ref, w_ref, cw_ref, cb_ref, gb_ref, mu_ref, w0_ref, a0_ref, wla_ref, g2_ref,
     kk_ref, ka_ref, gs_ref) = ins[:15]
    (q_ref, k_ref, v_ref, so_ref, g8_ref, g8t_ref, r_ref, dec_ref, kt_ref, vb_ref, al_ref, be_ref, gg_ref,
     ctail_ref, stail_ref) = outs[:15]
    a2 = 2 * aw
    sw = 3 * bw + B_LORA_W + B_LORA_A + B_LORA_G

    x = x_ref[0]
    mod = mod_ref[0]
    h = _rmsnorm(x, ng_ref[...]) * (1.0 + mod[:, d:2 * d]) + mod[:, 0:d]
    p = jnp.dot(h.astype(BF16), w_ref[...], preferred_element_type=F32)
    u = p[:, 0:a2]
    v = p[:, a2:a2 + aw]
    o = p[:, a2 + aw:a2 + 2 * aw]
    pb = p[:, 4 * aw:4 * aw + sw]
    gt = p[:, 4 * aw + sw:4 * aw + sw + LANES]

    if seq:
        ubuf, pbuf = outs[15], outs[16]

        @pl.when(pl.program_id(1) == 0)
        def _():
            ubuf[0:SUBLANES, :] = jnp.zeros((SUBLANES, a2), F32)
            pbuf[0:SUBLANES, :] = jnp.zeros((SUBLANES, sw), F32)

        ubuf[SUBLANES:SUBLANES + tt, :] = u
        pbuf[SUBLANES:SUBLANES + tt, :] = pb
        u1 = ubuf[SUBLANES - 1:SUBLANES - 1 + tt, :]
        u2 = ubuf[SUBLANES - 2:SUBLANES - 2 + tt, :]
        u3 = ubuf[SUBLANES - 3:SUBLANES - 3 + tt, :]
        pprev = pbuf[SUBLANES - 1:SUBLANES - 1 + tt, :]
        ubuf[0:SUBLANES, :] = u[tt - SUBLANES:tt, :]
        pbuf[0:SUBLANES, :] = pb[tt - SUBLANES:tt, :]
        ctail_ref[0] = u[tt - SUBLANES:tt, :]
        stail_ref[0] = pb[tt - SUBLANES:tt, :]
    else:
        u3, u2, u1, pprev = ins[15][0], ins[16][0], ins[17][0], ins[18][0]
        ctail_ref[0] = u
        stail_ref[0] = pb

    cw = cw_ref[...]
    y = cb_ref[...] + u3 * cw[0:1, :] + u2 * cw[1:2, :] + u1 * cw[2:3, :] + u * cw[3:4, :]
    qk = _silu(y)
    q_ref[0] = qk[:, 0:aw] * (A_DK ** -0.5)
    k_ref[0] = qk[:, aw:a2]
    v_ref[0] = v
    so_ref[0] = _sigmoid(o)
    g = gt + gb_ref[...]
    lane = lax.broadcasted_iota(jnp.int32, g.shape, 1)
    gates = jnp.where(lane < A_HEADS, g, -_softplus(-g))
    g8_ref[0] = gates[:, 0:2 * A_HEADS]
    g8t_ref[0] = gates.T[0:2 * A_HEADS, :]

    xb = pb + mu_ref[...] * (pprev - pb)
    r = xb[:, 0:bw]
    kb = xb[:, bw:2 * bw]
    vb = xb[:, 2 * bw:3 * bw]
    la = xb[:, 3 * bw:3 * bw + LANES]
    gl = xb[:, 3 * bw + LANES:3 * bw + 2 * LANES]
    lane2 = lax.broadcasted_iota(jnp.int32, la.shape, 1)
    la_act = jnp.where(lane2 < B_LORA_W, jnp.tanh(la), la)
    lw = jnp.dot(la_act.astype(BF16), wla_ref[...], preferred_element_type=F32)
    wlog = -_softplus(-(w0_ref[...] + lw[:, 0:bw])) - 0.5
    decay = jnp.exp(-jnp.exp(wlog))
    a = _sigmoid(a0_ref[...] + lw[:, bw:2 * bw])
    gg = jnp.dot(_sigmoid(gl).astype(BF16), g2_ref[...], preferred_element_type=F32)
    kk = kb * kk_ref[...]
    ss = _dot_split(kk * kk, gs_ref[...], 2)
    kkn = kk * lax.rsqrt(jnp.maximum(ss, 1e-24))
    r_ref[0] = r
    dec_ref[0] = decay
    kt_ref[0] = kb * (1.0 + (a - 1.0) * ka_ref[...])
    vb_ref[0] = vb
    al_ref[0] = -kkn
    be_ref[0] = kkn * a
    gg_ref[0] = gg


def _even_pre(x3, mod3, ng, wcat, cw, cb, gb, mu, w0, a0, wla, g2, k_k, k_a, prev, tt):
    g_, ttot, d = x3.shape
    aw = d // 2
    bw = d // 2
    a2 = 2 * aw
    sw = 3 * bw + B_LORA_W + B_LORA_A + B_LORA_G
    seq = prev is None
    gsum = _group_matrix(bw, B_HEAD, 1.0)
    consts = [ng, wcat, cw, cb, gb, mu, w0, a0, wla, g2, k_k, k_a, gsum]
    ins = [x3, mod3] + consts
    in_specs = [_tok_spec(tt, d), _mod_spec(mod3, tt)] + [_full_spec(c) for c in consts]
    if not seq:
        ins += list(prev)
        in_specs += [_tok_spec(tt, a2)] * 3 + [_tok_spec(tt, sw)]
    nt = ttot // tt
    tail_rows = SUBLANES if seq else tt
    tail_tot = g_ * SUBLANES if seq else ttot

    def tail_spec(width):
        if seq:
            return pl.BlockSpec((1, SUBLANES, width), lambda g, i: (g, 0, 0))
        return _tok_spec(tt, width)

    def tail_shape(width):
        if seq:
            return jax.ShapeDtypeStruct((g_, SUBLANES, width), F32)
        return jax.ShapeDtypeStruct((g_, ttot, width), F32)

    tok = lambda w: jax.ShapeDtypeStruct((g_, ttot, w), F32)
    out_shape = [tok(aw), tok(aw), tok(aw), tok(aw), tok(2 * A_HEADS),
                 jax.ShapeDtypeStruct((g_, 2 * A_HEADS, ttot), F32),
                 tok(bw), tok(bw), tok(bw), tok(bw), tok(bw), tok(bw), tok(bw),
                 tail_shape(a2), tail_shape(sw)]
    out_specs = [_tok_spec(tt, aw)] * 4 + [_tok_spec(tt, 2 * A_HEADS),
                                           pl.BlockSpec((1, 2 * A_HEADS, tt), lambda g, i: (g, 0, i))]
    out_specs += [_tok_spec(tt, bw)] * 7 + [tail_spec(a2), tail_spec(sw)]
    scratch = [pltpu.VMEM((tt + SUBLANES, a2), F32), pltpu.VMEM((tt + SUBLANES, sw), F32)] if seq else []
    del tail_rows, tail_tot
    return pl.pallas_call(
        functools.partial(_even_pre_body, seq, tt, d, aw, bw),
        grid=(g_, nt),
        in_specs=in_specs,
        out_specs=out_specs,
        out_shape=out_shape,
        scratch_shapes=scratch,
        compiler_params=_params(2),
        name="even_pre_seq" if seq else "even_pre_step",
    )(*ins)


def _mlstm_chunk_body(L, q_ref, k_ref, v_ref, g8_ref, g8t_ref, c0_ref, n0_ref, m0_ref,
                      h_ref, c1_ref, n1_ref, m1_ref, C_s, n_s, m_s):
    c = pl.program_id(1)

    @pl.when(c == 0)
    def _():
        C_s[...] = c0_ref[0]
        n_s[...] = n0_ref[0]
        m_s[...] = m0_ref[0]

    row = lax.broadcasted_iota(jnp.int32, (L, L), 0)
    col = lax.broadcasted_iota(jnp.int32, (L, L), 1)
    causal = col <= row
    tril = jnp.where(causal, 1.0, 0.0).astype(BF16)
    triu = jnp.where(row <= col, 1.0, 0.0).astype(BF16)
    g8 = g8_ref[0]
    g8t = g8t_ref[0]
    bcols = _dot_split_left(tril, g8, 3)
    brows = _dot_split(g8t, triu, 3)
    for hd in range(A_HEADS):
        sl = slice(hd * A_DK, (hd + 1) * A_DK)
        q = q_ref[0, :, sl]
        k = k_ref[0, :, sl]
        v = v_ref[0, :, sl]
        i_row = g8t[hd:hd + 1, :]
        i_col = g8[:, hd:hd + 1]
        b_row = brows[A_HEADS + hd:A_HEADS + hd + 1, :]
        b_col = bcols[:, A_HEADS + hd:A_HEADS + hd + 1]
        m_prev = m_s[hd:hd + 1, 0:1]
        C = C_s[hd]
        n = n_s[hd:hd + 1, :]
        dmat = jnp.where(causal, b_col - b_row + i_row, NEG_INF)
        inter = b_col + m_prev
        m_t = jnp.maximum(inter, jnp.max(dmat, axis=1, keepdims=True))
        w_intra = jnp.exp(dmat - m_t)
        w_inter = jnp.exp(inter - m_t)
        s = _dot_nt(q, k) * w_intra
        num = _bdot(s, v) + w_inter * _bdot(q, C)
        den = jnp.sum(s, axis=1, keepdims=True) + w_inter * jnp.sum(q * n, axis=1, keepdims=True)
        h_ref[0, :, sl] = num / jnp.maximum(jnp.abs(den), jnp.exp(-m_t))
        m_new = m_t[L - 1:L, :]
        b_last = b_col[L - 1:L, :]
        kw = k * jnp.exp(b_last - b_col + i_col - m_new)
        carry = jnp.exp(b_last + m_prev - m_new)
        C_s[hd] = carry * C + _dot_tn(kw, v)
        n_s[hd:hd + 1, :] = carry * n + jnp.sum(kw, axis=0, keepdims=True)
        m_s[hd:hd + 1, :] = jnp.broadcast_to(m_new, (1, LANES))

    @pl.when(c == pl.num_programs(1) - 1)
    def _():
        c1_ref[0] = C_s[...]
        n1_ref[0] = n_s[...]
        m1_ref[0] = m_s[...]


def _mlstm_chunk(q, k, v, g8, g8t, C0, n0p, m0p, L):
    b, t, aw = q.shape
    st = lambda *s: pl.BlockSpec((1,) + s, lambda bb, c: (bb,) + (0,) * len(s))
    return pl.pallas_call(
        functools.partial(_mlstm_chunk_body, L),
        grid=(b, t // L),
        in_specs=[_tok_spec(L, aw)] * 3 + [_tok_spec(L, 2 * A_HEADS),
                                           pl.BlockSpec((1, 2 * A_HEADS, L), lambda bb, c: (bb, 0, c)),
                                           st(A_HEADS, A_DK, A_DK), st(SUBLANES, LANES), st(SUBLANES, LANES)],
        out_specs=[_tok_spec(L, aw), st(A_HEADS, A_DK, A_DK), st(SUBLANES, LANES), st(SUBLANES, LANES)],
        out_shape=[jax.ShapeDtypeStruct((b, t, aw), F32), jax.ShapeDtypeStruct(C0.shape, F32),
                   jax.ShapeDtypeStruct(n0p.shape, F32), jax.ShapeDtypeStruct(m0p.shape, F32)],
        scratch_shapes=[pltpu.VMEM((A_HEADS, A_DK, A_DK), F32), pltpu.VMEM((SUBLANES, LANES), F32),
                        pltpu.VMEM((SUBLANES, LANES), F32)],
        compiler_params=_params(2),
        name="mlstm_chunk",
    )(q, k, v, g8, g8t, C0, n0p, m0p)


def _row8(x):
    r = lax.broadcasted_iota(jnp.int32, (SUBLANES, x.shape[1]), 0)
    return jnp.where(r == 0, jnp.broadcast_to(x, (SUBLANES, x.shape[1])), 0.0)


def _mlstm_step_body(bb, q_ref, k_ref, v_ref, g8_ref, n0_ref, m0_ref, c0_ref, *rest):
    h_ref, c1_ref, n1_ref, m1_ref = rest[-4:]
    g8 = g8_ref[...]
    m0 = m0_ref[...]
    for hd in range(A_HEADS):
        sl = slice(hd * A_DK, (hd + 1) * A_DK)
        li = g8[:, hd:hd + 1]
        lf = g8[:, A_HEADS + hd:A_HEADS + hd + 1]
        mp = m0[:, hd:hd + 1]
        m1 = jnp.maximum(lf + mp, li)
        wi = jnp.exp(li - m1)
        wf = jnp.exp(lf + mp - m1)
        floor = jnp.exp(-m1)
        for b in range(bb):
            q = q_ref[b:b + 1, sl]
            k = k_ref[b:b + 1, sl]
            v = v_ref[b:b + 1, sl]
            wib = wi[b:b + 1, :]
            wfb = wf[b:b + 1, :]
            C1 = wfb * c0_ref[0, b, hd] + wib * _dot_tn(_row8(k), _row8(v))
            n1 = wfb * n0_ref[b, hd:hd + 1, :] + wib * k
            num = _bdot(_row8(q), C1)[0:1, :]
            den = jnp.sum(q * n1, axis=1, keepdims=True)
            h_ref[b:b + 1, sl] = num / jnp.maximum(jnp.abs(den), floor[b:b + 1, :])
            c1_ref[0, b, hd] = C1
            n1_ref[b, hd:hd + 1, :] = n1
        m1_ref[:, hd:hd + 1] = m1


def _rwkv_body(bb, tb, ty, al_ref, w_ref, be_ref, kt_ref, r_ref, v_ref, s0_ref, y_ref, s1_ref, S_s, t3_s, y_s):
    @pl.when(pl.program_id(1) == 0)
    def _():
        S_s[...] = s0_ref[...]

    npair = B_HEADS // 2
    ntile = bb * npair
    nrow = min(tb, SUBLANES)
    row = lax.broadcasted_iota(jnp.int32, (B_HEAD, LANES), 0)
    lane = lax.broadcasted_iota(jnp.int32, (B_HEAD, LANES), 1)
    eye_pair = row == (lane % B_HEAD)
    wred = _group_matrix(2 * LANES, B_HEAD, 1.0)
    kk = lax.broadcasted_iota(jnp.int32, (nrow * LANES, LANES), 0)
    ll = lax.broadcasted_iota(jnp.int32, (nrow * LANES, LANES), 1)
    sel_head = ((kk % LANES) // B_HEAD) == (ll // B_HEAD)
    sel_off = (ll % B_HEAD) - (kk // LANES)

    def group(t8, carry):
        base = t8 * nrow if isinstance(t8, int) else pl.multiple_of(t8 * nrow, nrow)
        rows = pl.ds(base, nrow)
        for i in range(nrow):
            ri = slice(i, i + 1)
            lhs = []
            for b in range(bb):
                for p in range(npair):
                    sl = pl.ds(p * LANES, LANES)
                    lhs.append(jnp.concatenate(
                        [(S_s[b, p] * al_ref[b, rows, sl][ri]).astype(BF16),
                         jnp.where(eye_pair, v_ref[b, rows, sl][ri], 0.0).astype(BF16)], axis=1))
            red = jnp.dot(jnp.concatenate(lhs, axis=0), wred, preferred_element_type=F32)
            for b in range(bb):
                for p in range(npair):
                    sl = pl.ds(p * LANES, LANES)
                    idx = b * npair + p
                    sa = red[idx * B_HEAD:(idx + 1) * B_HEAD, 0:LANES]
                    vcol = red[idx * B_HEAD:(idx + 1) * B_HEAD, LANES:2 * LANES]
                    S = (S_s[b, p] * w_ref[b, rows, sl][ri] + sa * be_ref[b, rows, sl][ri]
                         + vcol * kt_ref[b, rows, sl][ri])
                    S_s[b, p] = S
                    t3_s[idx * B_HEAD:(idx + 1) * B_HEAD, i * LANES:(i + 1) * LANES] = (
                        S * r_ref[b, rows, sl][ri]).astype(BF16)
        wsel = jnp.where(sel_head & (sel_off == base % ty), 1.0, 0.0).astype(BF16)
        yg = jnp.dot(t3_s[...], wsel, preferred_element_type=F32)

        @pl.when(base % ty == 0)
        def _():
            y_s[...] = yg

        @pl.when(base % ty != 0)
        def _():
            y_s[...] += yg

        @pl.when((base + nrow) % ty == 0)
        def _():
            y_ref[0, base // ty] = y_s[...]
        return carry

    if tb == nrow:
        group(0, 0)
    else:
        lax.fori_loop(0, tb // nrow, group, 0)

    @pl.when(pl.program_id(1) == pl.num_programs(1) - 1)
    def _():
        s1_ref[...] = S_s[...]


def _rwkv_scan(al, w, be, kt, r, v, S0p, bb, tb):
    b, t, bw = al.shape
    npair = B_HEADS // 2
    ty = min(B_HEAD, t)
    nb = b // bb
    tok = pl.BlockSpec((bb, tb, bw), lambda i, j: (i, j, 0))
    st = pl.BlockSpec((bb, npair, B_HEAD, LANES), lambda i, j: (i, 0, 0, 0))
    nrow = min(tb, SUBLANES)
    rows = bb * npair * B_HEAD
    yt, S1p = pl.pallas_call(
        functools.partial(_rwkv_body, bb, tb, ty),
        grid=(nb, t // tb),
        in_specs=[tok] * 6 + [st],
        out_specs=[pl.BlockSpec((1, tb // ty, rows, LANES), lambda i, j: (i, j, 0, 0)), st],
        out_shape=[jax.ShapeDtypeStruct((nb, t // ty, rows, LANES), F32), jax.ShapeDtypeStruct(S0p.shape, F32)],
        scratch_shapes=[pltpu.VMEM((bb, npair, B_HEAD, LANES), F32), pltpu.VMEM((rows, nrow * LANES), BF16),
                        pltpu.VMEM((rows, LANES), F32)],
        compiler_params=_params(2),
        name="rwkv_scan",
    )(al, w, be, kt, r, v, S0p)
    y = jnp.stack([yt[..., 0:ty], yt[..., B_HEAD:B_HEAD + ty]], axis=3)
    y = y.reshape(nb, t // ty, bb, npair, B_HEAD, 2, ty).transpose(0, 2, 1, 6, 3, 5, 4).reshape(b, t, bw)
    return y, S1p


def _even_post_body(d, aw, bw, x_ref, mod_ref, ha_ref, so_ref, y_ref, r_ref, kt_ref, vb_ref, gg_ref,
                    ang_ref, lng_ref, lnb_ref, rk_ref, wo_ref, o_ref):
    ga = _group_matrix(aw, A_DK, 1.0 / A_DK)
    gb_mean = _group_matrix(bw, B_HEAD, 1.0 / B_HEAD)
    gb_sum = _group_matrix(bw, B_HEAD, 1.0)
    ha = ha_ref[0]
    ha = ha * lax.rsqrt(_dot_split(ha * ha, ga, 2) + EPS) * ang_ref[...] * so_ref[0]
    y = y_ref[0]
    yc = y - _dot_split(y, gb_mean, 2)
    var = _dot_split(yc * yc, gb_mean, 2)
    yn = yc * lax.rsqrt(var + B_LN_EPS) * lng_ref[...] + lnb_ref[...]
    bonus = _dot_split(r_ref[0] * kt_ref[0] * rk_ref[...], gb_sum, 2)
    y2 = (yn + bonus * vb_ref[0]) * gg_ref[0]
    out = (jnp.dot(ha.astype(BF16), wo_ref[0:aw, :], preferred_element_type=F32)
           + jnp.dot(y2.astype(BF16), wo_ref[aw:aw + bw, :], preferred_element_type=F32))
    o_ref[0] = x_ref[0] + mod_ref[0][:, 2 * d:3 * d] * out


def _even_post(x3, mod3, ha, so, y, r, kt, vb, gg, ang, lng, lnb, rk, wo, tt):
    g_, ttot, d = x3.shape
    aw = d // 2
    bw = d // 2
    consts = [ang, lng, lnb, rk, wo]
    return pl.pallas_call(
        functools.partial(_even_post_body, d, aw, bw),
        grid=(g_, ttot // tt),
        in_specs=[_tok_spec(tt, d), _mod_spec(mod3, tt)] + [_tok_spec(tt, aw)] * 7 + [_full_spec(c) for c in consts],
        out_specs=_tok_spec(tt, d),
        out_shape=jax.ShapeDtypeStruct(x3.shape, F32),
        compiler_params=_params(2),
        name="even_post",
    )(x3, mod3, ha, so, y, r, kt, vb, gg, *consts)


def _odd_pre_body(j, d, x_ref, mod_ref, ng_ref, w_ref, lbl_ref, q_ref, k_ref, v_ref, lf_ref, sg_ref):
    x = x_ref[0]
    mod = mod_ref[0]
    h = _rmsnorm(x, ng_ref[...]) * (1.0 + mod[:, d:2 * d]) + mod[:, 0:d]
    p = jnp.dot(h.astype(BF16), w_ref[...], preferred_element_type=F32)
    lbl = lbl_ref[...]
    n_odd = lbl.shape[0]
    mx = lbl[0:1, :]
    for i in range(1, n_odd):
        mx = jnp.maximum(mx, lbl[i:i + 1, :])
    ex = [jnp.exp(lbl[i:i + 1, :] - mx) for i in range(n_odd)]
    tot = ex[0]
    for i in range(1, n_odd):
        tot = tot + ex[i]
    lb = jnp.zeros_like(mx)
    for i in range(1, j + 1):
        lb = lb + ex[i] / tot
    f = lb + (1.0 - lb) * _sigmoid(p[:, d:2 * d])
    q_ref[0] = _silu(p[:, 0:d])
    k_ref[0] = 1.0 - f
    v_ref[0] = p[:, 2 * d:3 * d]
    lf_ref[0] = jnp.log(f)
    sg_ref[0] = _silu(p[:, 3 * d:4 * d])


def _odd_pre(x3, mod3, ng, w_in, lbl, j, tt):
    g_, ttot, d = x3.shape
    consts = [ng, w_in, lbl]
    return pl.pallas_call(
        functools.partial(_odd_pre_body, j, d),
        grid=(g_, ttot // tt),
        in_specs=[_tok_spec(tt, d), _mod_spec(mod3, tt)] + [_full_spec(c) for c in consts],
        out_specs=[_tok_spec(tt, d)] * 5,
        out_shape=[jax.ShapeDtypeStruct(x3.shape, F32)] * 5,
        compiler_params=_params(2),
        name="odd_pre",
    )(x3, mod3, *consts)


def _col_bcast(row, terms):
    parts = _split(row, terms)
    r = lax.broadcasted_iota(jnp.int32, (SUBLANES, row.shape[1]), 0)
    lhs = jnp.zeros((SUBLANES, row.shape[1]), F32)
    for i, p in enumerate(parts):
        lhs = jnp.where(r == i, jnp.broadcast_to(p.astype(F32), lhs.shape), lhs)
    ones = jnp.where(lax.broadcasted_iota(jnp.int32, (SUBLANES, LANES), 0) < terms, 1.0, 0.0).astype(BF16)
    return lax.dot_general(lhs.astype(BF16), ones, (((0,), (0,)), ((), ())), preferred_element_type=F32)


def _hgrn_chunk_body(L, sub, nchunk, hg, q_ref, k_ref, v_ref, g_ref, s0_ref, o_ref, s1_ref, S_s):
    @pl.when(pl.program_id(2) == 0)
    def _():
        S_s[...] = s0_ref[0]

    nj = L // sub
    rowi = lax.broadcasted_iota(jnp.int32, (L, C_DK), 0)
    rr = lax.broadcasted_iota(jnp.int32, (L, L), 0)
    cc = lax.broadcasted_iota(jnp.int32, (L, L), 1)
    tril = jnp.where(cc <= rr, 1.0, 0.0).astype(BF16)
    ones = jnp.ones((C_DK, LANES), BF16)

    def chunk(ci, carry):
        base = pl.multiple_of(ci * L, L)
        rows = pl.ds(base, L)
        b_all = _dot_split_left(tril, g_ref[0, rows, :], 3)
        heads = []
        for hd in range(hg):
            sl = slice(hd * C_DK, (hd + 1) * C_DK)
            heads.append((q_ref[0, rows, sl], k_ref[0, rows, sl], v_ref[0, rows, sl], b_all[:, sl], S_s[hd]))
        outs = [_bdot(q * jnp.exp(b), S) for (q, k, v, b, S) in heads]
        for dlt in range(sub):
            prs, vss = [], []
            for (q, k, v, b, S) in heads:
                if dlt == 0:
                    prs.append((q * k).astype(BF16))
                    vss.append(v)
                else:
                    ok = (rowi % sub) >= dlt
                    e = jnp.exp(jnp.where(ok, b - pltpu.roll(b, dlt, 0), NEG_INF))
                    prs.append((q * pltpu.roll(k, dlt, 0) * e).astype(BF16))
                    vss.append(pltpu.roll(v, dlt, 0))
            rs = jnp.dot(jnp.concatenate(prs, axis=0), ones, preferred_element_type=F32)
            outs = [o + rs[i * L:(i + 1) * L, :] * vs for i, (o, vs) in enumerate(zip(outs, vss))]
        for hd, ((q, k, v, b, S), o) in enumerate(zip(heads, outs)):
            if nj > 1:
                amat = None
                for jj in range(nj - 1):
                    end = (jj + 1) * sub
                    bref = b[end - 1:end, :]
                    qj = q * jnp.exp(jnp.where(rowi >= end, b - bref, NEG_INF))
                    kj = k * jnp.exp(jnp.where((rowi >= end - sub) & (rowi < end), bref - b, NEG_INF))
                    a = _dot_nt(qj, kj)
                    amat = a if amat is None else amat + a
                o = o + _bdot(amat, v)
            o_ref[0, rows, hd * C_DK:(hd + 1) * C_DK] = o
            b_last = b[L - 1:L, :]
            S_s[hd] = _col_bcast(jnp.exp(b_last), 3) * S + _dot_tn(k * jnp.exp(b_last - b), v)
        return carry

    lax.fori_loop(0, nchunk, chunk, 0)

    @pl.when(pl.program_id(2) == pl.num_programs(2) - 1)
    def _():
        s1_ref[0] = S_s[...]


def _hgrn_chunk(q, k, v, lf, S0, tblk, L, sub, hg):
    b, t, d = q.shape
    tok = pl.BlockSpec((1, tblk, hg * C_DK), lambda bb, h, c: (bb, c, h))
    st = pl.BlockSpec((1, hg, C_DK, C_DK), lambda bb, h, c: (bb, h, 0, 0))
    return pl.pallas_call(
        functools.partial(_hgrn_chunk_body, L, sub, tblk // L, hg),
        grid=(b, C_HEADS // hg, t // tblk),
        in_specs=[tok] * 4 + [st],
        out_specs=[tok, st],
        out_shape=[jax.ShapeDtypeStruct((b, t, d), F32), jax.ShapeDtypeStruct(S0.shape, F32)],
        scratch_shapes=[pltpu.VMEM((hg, C_DK, C_DK), F32)],
        compiler_params=_params(3),
        name="hgrn_chunk",
    )(q, k, v, lf, S0)


def _hgrn_step_body(bb, q_ref, k_ref, v_ref, g_ref, s0_ref, *rest):
    o_ref, s1_ref = rest[-2], rest[-1]
    for b in range(bb):
        for hd in range(C_HEADS):
            sl = slice(hd * C_DK, (hd + 1) * C_DK)
            q = q_ref[b:b + 1, sl]
            k = k_ref[b:b + 1, sl]
            v = v_ref[b:b + 1, sl]
            f = jnp.exp(g_ref[b:b + 1, sl])
            S1 = _col_bcast(f, 3) * s0_ref[0, b, hd] + _dot_tn(_row8(k), _row8(v))
            o_ref[b:b + 1, sl] = _bdot(_row8(q), S1)[0:1, :]
            s1_ref[0, b, hd] = S1


def _layer_state_call(body, name, rows_in, row_out_shape, row_spec, state_all, layer, state_acc, bb, extra_out=()):
    b = state_all.shape[1]
    blk = (1, bb) + state_all.shape[2:]
    zeros = (0,) * (state_all.ndim - 2)
    st = pl.BlockSpec(blk, lambda i: (layer, i) + zeros)
    ins = list(rows_in) + [state_all]
    in_specs = [row_spec(r) for r in rows_in] + [st]
    aliases = {}
    if state_acc is not None:
        ins.append(state_acc)
        in_specs.append(pl.BlockSpec(memory_space=pl.ANY))
        aliases = {len(ins) - 1: 1}
    return pl.pallas_call(
        body,
        grid=(b // bb,),
        in_specs=in_specs,
        out_specs=[row_spec(row_out_shape), st] + [row_spec(e) for e in extra_out],
        out_shape=[jax.ShapeDtypeStruct(row_out_shape.shape, F32), jax.ShapeDtypeStruct(state_all.shape, F32)]
        + [jax.ShapeDtypeStruct(e.shape, F32) for e in extra_out],
        input_output_aliases=aliases,
        compiler_params=_params(1),
        name=name,
    )(*ins)


def _row_spec_for(bb):
    def spec(arr):
        blk = (bb,) + arr.shape[1:]
        zeros = (0,) * (arr.ndim - 1)
        return pl.BlockSpec(blk, lambda i: (i,) + zeros)
    return spec


def _mlstm_step(q, k, v, g8, n0, m0, C_all, layer, C_acc, bb):
    return _layer_state_call(functools.partial(_mlstm_step_body, bb), "mlstm_step", [q, k, v, g8, n0, m0], q,
                             _row_spec_for(bb), C_all, layer, C_acc, bb, extra_out=(n0, m0))


def _hgrn_step(q, k, v, lf, S_all, layer, S_acc, bb):
    return _layer_state_call(functools.partial(_hgrn_step_body, bb), "hgrn_step", [q, k, v, lf], q,
                             _row_spec_for(bb), S_all, layer, S_acc, bb)


def _odd_post_body(d, x_ref, mod_ref, o_ref, sg_ref, ng_ref, wo_ref, out_ref):
    gm = _group_matrix(d, C_DK, 1.0 / C_DK)
    o = o_ref[0]
    o = o * lax.rsqrt(_dot_split(o * o, gm, 2) + EPS) * ng_ref[...] * sg_ref[0]
    out = jnp.dot(o.astype(BF16), wo_ref[...], preferred_element_type=F32)
    out_ref[0] = x_ref[0] + mod_ref[0][:, 2 * d:3 * d] * out


def _odd_post(x3, mod3, o, sg, ng_tiled, wo, tt):
    g_, ttot, d = x3.shape
    consts = [ng_tiled, wo]
    return pl.pallas_call(
        functools.partial(_odd_post_body, d),
        grid=(g_, ttot // tt),
        in_specs=[_tok_spec(tt, d), _mod_spec(mod3, tt)] + [_tok_spec(tt, d)] * 2 + [_full_spec(c) for c in consts],
        out_specs=_tok_spec(tt, d),
        out_shape=jax.ShapeDtypeStruct(x3.shape, F32),
        compiler_params=_params(2),
        name="odd_post",
    )(x3, mod3, o, sg, *consts)


P_NCAND = P_TOPK + 1
P_CAND_ROWS = 3 * SUBLANES


def _top_sorted(s):
    ridx = lax.broadcasted_iota(jnp.int32, (P_CAND_ROWS, s.shape[1]), 0)
    acc = jnp.full((P_CAND_ROWS, s.shape[1]), NEG_INF, F32)
    cur = s
    for i in range(P_NCAND):
        m = jnp.max(cur, axis=0, keepdims=True)
        acc = jnp.where(ridx == i, m, acc)
        cur = jnp.where(cur == m, NEG_INF, cur)
    return acc


def _peer_body(tm, te, ne, d, x_ref, mod_ref, ng_ref, wqt_ref, keys_ref, u_ref, vt_ref, o_ref,
               ht_s, s1_s, e1_s, thr_s, e2_s, gate_s, w_s, acc_s):
    j = pl.program_id(2)
    dk2 = keys_ref.shape[4]
    nsub = te // P_NKEYS

    @pl.when(j == 0)
    def _():
        x = x_ref[0]
        mod = mod_ref[0]
        h = _rmsnorm(x, ng_ref[...]) * (1.0 + mod[:, 4 * d:5 * d]) + mod[:, 3 * d:4 * d]
        ht = h.T.astype(BF16)
        ht_s[...] = ht
        qt = jnp.dot(wqt_ref[0], ht, preferred_element_type=F32)
        r8 = lax.broadcasted_iota(jnp.int32, (SUBLANES, LANES), 0)
        for hd in range(P_HEADS):
            s1_all = _bdot(keys_ref[0, hd, 0], qt[(2 * hd) * dk2:(2 * hd + 1) * dk2, :])
            s2_all = _bdot(keys_ref[0, hd, 1], qt[(2 * hd + 1) * dk2:(2 * hd + 2) * dk2, :])
            for tc in range(tm // LANES):
                ls = slice(tc * LANES, (tc + 1) * LANES)
                s1 = s1_all[:, ls]
                s2 = s2_all[:, ls]
                a = _top_sorted(s1)
                b = _top_sorted(s2)
                blocks = [a[0:1, :] + b]
                for i in range(2, SUBLANES + 1):
                    blocks.append(jnp.where(r8 < P_NCAND // i, a[i - 1:i, :] + b[0:SUBLANES, :], NEG_INF))
                blocks.append(a[SUBLANES:P_CAND_ROWS, :] + b[0:1, :])
                cand = jnp.concatenate(blocks, axis=0)
                cur = cand
                best = None
                for _ in range(P_NCAND):
                    prev, best = best, jnp.max(cur, axis=0, keepdims=True)
                    cur = jnp.where(cur == best, NEG_INF, cur)
                tau = 0.5 * (prev + best)
                top = a[0:1, :] + b[0:1, :]
                z = jnp.sum(jnp.where(cand >= tau, jnp.exp(cand - top), 0.0), axis=0, keepdims=True)
                s1_s[hd, :, ls] = s1
                e1_s[hd, :, ls] = jnp.exp(s1 - a[0:1, :])
                thr_s[hd, :, ls] = tau - s2
                e2_s[hd, :, ls] = jnp.exp(s2 - b[0:1, :]) / z
        acc_s[...] = jnp.zeros_like(acc_s)
        w_s[1] = jnp.zeros((te, tm), BF16)

    assert nsub == SUBLANES
    jj = jnp.minimum(j, ne - 1)
    grp = pl.ds(pl.multiple_of(jj * nsub, nsub), nsub)

    def gates(tc):
        ls = slice(tc * LANES, (tc + 1) * LANES)
        s1g = [s1_s[hd, grp, ls] for hd in range(P_HEADS)]
        e1g = [e1_s[hd, grp, ls] for hd in range(P_HEADS)]
        for ii in range(nsub):
            rows = slice(ii * P_NKEYS, (ii + 1) * P_NKEYS)
            gate = jnp.zeros((P_NKEYS, LANES), F32)
            for hd in range(P_HEADS):
                keep = s1g[hd][ii:ii + 1, :] >= thr_s[hd, :, ls]
                gate = gate + jnp.where(keep, e2_s[hd, :, ls], 0.0) * e1g[hd][ii:ii + 1, :]
            gate_s[rows, ls] = gate

    piece = min(tm, MXU_WIDTH)
    tc_per_piece = piece // LANES
    for pc in range(tm // piece):
        cs = slice(pc * piece, (pc + 1) * piece)
        acc_s[:, cs] += jnp.dot(vt_ref[0], w_s[(j + 1) % 2, :, cs], preferred_element_type=F32)
        for tc in range(pc * tc_per_piece, pc * tc_per_piece + (tc_per_piece + 1) // 2):
            gates(tc)
        act = jnp.dot(u_ref[0], ht_s[:, cs], preferred_element_type=F32)
        for tc in range(pc * tc_per_piece + (tc_per_piece + 1) // 2, (pc + 1) * tc_per_piece):
            gates(tc)
        w_s[j % 2, :, cs] = (gate_s[:, cs] * _gelu_tanh(act)).astype(BF16)

    @pl.when(j == ne)
    def _():
        o_ref[0] = x_ref[0] + mod_ref[0][:, 5 * d:6 * d] * acc_s[...].T


def _peer(x3, mod3, ng, wqt_all, keys_all, u_all, vt_all, layer, tm, te):
    g_, ttot, d = x3.shape
    ne = u_all.shape[1] // te
    nk = P_NKEYS
    if mod3.shape[1] == 1:
        mod_spec = pl.BlockSpec((1, 1, mod3.shape[2]), lambda g, i, j: (g, 0, 0))
    else:
        mod_spec = pl.BlockSpec((1, tm, mod3.shape[2]), lambda g, i, j: (g, i, 0))
    return pl.pallas_call(
        functools.partial(_peer_body, tm, te, ne, d),
        grid=(g_, ttot // tm, ne + 1),
        in_specs=[pl.BlockSpec((1, tm, d), lambda g, i, j: (g, i, 0)), mod_spec,
                  pl.BlockSpec(ng.shape, lambda g, i, j: (0, 0)),
                  pl.BlockSpec((1,) + wqt_all.shape[1:], lambda g, i, j: (layer, 0, 0)),
                  pl.BlockSpec((1,) + keys_all.shape[1:], lambda g, i, j: (layer, 0, 0, 0, 0)),
                  pl.BlockSpec((1, te, d), lambda g, i, j: (layer, jnp.minimum(j, ne - 1), 0)),
                  pl.BlockSpec((1, d, te), lambda g, i, j: (layer, 0, jnp.maximum(j - 1, 0)))],
        out_specs=pl.BlockSpec((1, tm, d), lambda g, i, j: (g, i, 0)),
        out_shape=jax.ShapeDtypeStruct(x3.shape, F32),
        scratch_shapes=[pltpu.VMEM((d, tm), BF16),
                        pltpu.VMEM((P_HEADS, nk, tm), F32), pltpu.VMEM((P_HEADS, nk, tm), F32),
                        pltpu.VMEM((P_HEADS, nk, tm), F32), pltpu.VMEM((P_HEADS, nk, tm), F32),
                        pltpu.VMEM((te, tm), F32), pltpu.VMEM((2, te, tm), BF16), pltpu.VMEM((d, tm), F32)],
        compiler_params=_params(3),
        name="peer",
    )(x3, mod3, ng, wqt_all, keys_all, u_all, vt_all)


def _final_norm_body(x_ref, g_ref, o_ref):
    o_ref[0] = _rmsnorm(x_ref[0], g_ref[...])


def _final_norm(x3, g, tt):
    g_, ttot, d = x3.shape
    return pl.pallas_call(
        _final_norm_body,
        grid=(g_, ttot // tt),
        in_specs=[_tok_spec(tt, d), _full_spec(g)],
        out_specs=_tok_spec(tt, d),
        out_shape=jax.ShapeDtypeStruct(x3.shape, F32),
        compiler_params=_params(2),
        name="final_norm",
    )(x3, g)


def _pair_state(S):
    b = S.shape[0]
    return S.reshape(b, B_HEADS // 2, 2, B_HEAD, B_HEAD).transpose(0, 1, 3, 2, 4).reshape(b, B_HEADS // 2, B_HEAD, LANES)


def _unpair_state(Sp):
    b = Sp.shape[0]
    return Sp.reshape(b, B_HEADS // 2, B_HEAD, 2, B_HEAD).transpose(0, 1, 3, 2, 4).reshape(b, B_HEADS, B_HEAD, B_HEAD)


def _trunk(x, mod_all, st, w, seq):
    bsz, t, d = x.shape
    depth = mod_all.shape[0]
    aw = d // 2
    bw = d // 2
    if seq:
        x3 = x
        tt = min(256, t)
        tm = min(512, t)
        mods = [mod_all[l][:, None, :] for l in range(depth)]
    else:
        x3 = x.reshape(1, bsz * t, d)
        tt = bsz * t
        tm = bsz * t
        mods = [mod_all[l][None] for l in range(depth)]
    row = lambda v_: v_.reshape(1, -1)
    new_even, new_odd = [], []
    C_acc = S_acc = None
    for l in range(depth):
        j = l // 2
        if l % 2 == 0:
            e = w["even"][j]
            if seq:
                prev = None
            else:
                conv0, shift0 = st[3][j], st[5][j]
                prev = (conv0[:, 0][None], conv0[:, 1][None], conv0[:, 2][None], shift0[:, 0][None])
            (q, k, v, so, g8, g8t, r, dec, kt, vb, al, be, gg, ctail, stail) = _even_pre(
                x3, mods[l], row(w["norm_mix_g"][l]), e["wcat"], e["conv_w"], row(e["conv_b"]), e["gate_b"],
                row(e["mu"]), row(e["w0"]), row(e["a0"]), e["wla"], e["g2"], row(e["k_k"]), row(e["k_a"]), prev, tt)
            if seq:
                C0 = jnp.zeros((bsz, A_HEADS, A_DK, A_DK), F32)
                nm0 = jnp.zeros((bsz, SUBLANES, LANES), F32)
                ha, C1, n1p, m1p = _mlstm_chunk(q, k, v, g8, g8t, C0, nm0, nm0, min(256, t))
                n1 = n1p[:, :A_HEADS, :]
                m1 = m1p[:, :A_HEADS, 0]
                S0p = jnp.zeros((bsz, B_HEADS // 2, B_HEAD, LANES), F32)
                y, S1p = _rwkv_scan(al, dec, be, kt, r, vb, S0p, min(8, bsz), min(128, t))
                conv1 = ctail[:, SUBLANES - (A_CONV - 1):, :]
                shift1 = stail[:, SUBLANES - 1:, :]
            else:
                ha2, C_acc, n1, m1 = _mlstm_step(q[0], k[0], v[0], g8[0], st[1][j], st[2][j], st[0], j, C_acc, 8)
                C1 = None
                ha = ha2[None]
                tok = lambda a_: a_[0][:, None, :]
                y2, S1p = _rwkv_scan(tok(al), tok(dec), tok(be), tok(kt), tok(r), tok(vb), _pair_state(st[4][j]), 8, 1)
                y = y2.reshape(1, bsz, bw)
                conv1 = jnp.concatenate([st[3][j][:, 1:], ctail[0][:, None, :]], axis=1)
                shift1 = stail[0][:, None, :]
            x3 = _even_post(x3, mods[l], ha, so, y, r, kt, vb, gg, row(e["a_norm_g"]), row(e["ln_g"]),
                            row(e["ln_b"]), row(e["r_k"]), e["w_out"], tt)
            new_even.append((C1, n1, m1, conv1, _unpair_state(S1p), shift1))
        else:
            o_ = w["odd"][j]
            q, k, v, lf, sg = _odd_pre(x3, mods[l], row(w["norm_mix_g"][l]), o_["w_in"], w["lb_logits"], j, tt)
            if seq:
                S0 = jnp.zeros((bsz, C_HEADS, C_DK, C_DK), F32)
                o, S1 = _hgrn_chunk(q, k, v, lf, S0, min(256, t), min(64, t), min(16, t), 4)
            else:
                o2, S_acc = _hgrn_step(q[0], k[0], v[0], lf[0], st[6], j, S_acc, 8)
                S1 = None
                o = o2[None]
            x3 = _odd_post(x3, mods[l], o, sg, row(jnp.tile(o_["norm_g"], C_HEADS)), o_["w_out"], tt)
            new_odd.append(S1)
        pw = w["peer"]
        x3 = _peer(x3, mods[l], row(w["norm_ffn_g"][l]), pw["wqt"], pw["keys"], pw["u"], pw["vt"], l, tm, 1024)
    y = _final_norm(x3, row(w["norm_final_g"]), tt).reshape(bsz, t, d)
    ev = [None if new_even[0][i] is None else jnp.stack([s[i] for s in new_even]) for i in range(6)]
    if seq:
        return (y, ev[0], ev[1], ev[2], ev[3], ev[4], ev[5], jnp.stack(new_odd))
    return (y, C_acc, ev[1], ev[2], ev[3], ev[4], ev[5], S_acc)


def _prepare_weights(norm_mix_g, norm_ffn_g, norm_final_g, even_w_in, even_w_out, mlstm_conv_w, mlstm_conv_b,
                     mlstm_gate_b, mlstm_norm_g, rwkv_mu, rwkv_w0, rwkv_w2, rwkv_a0, rwkv_a2, rwkv_g2, rwkv_k_k,
                     rwkv_k_a, rwkv_r_k, rwkv_ln_g, rwkv_ln_b, odd_w_in, odd_w_out, hgrn_lb_logits, hgrn_norm_g,
                     peer_w_q, peer_keys, peer_u, peer_v):
    d = even_w_in.shape[1]
    aw = d // 2
    bw = d // 2
    sw = 3 * bw + B_LORA_W + B_LORA_A + B_LORA_G
    even = []
    for j in range(even_w_in.shape[0]):
        wi = even_w_in[j]
        gates_w = wi[:, 4 * aw:4 * aw + 2 * A_HEADS]
        wcat = jnp.concatenate([wi[:, 0:4 * aw], wi[:, 4 * aw + 2 * A_HEADS:],
                                jnp.pad(gates_w, ((0, 0), (0, LANES - 2 * A_HEADS)))], axis=1).astype(BF16)
        assert wcat.shape[1] == 4 * aw + sw + LANES
        wla = jnp.zeros((B_LORA_W + B_LORA_A, 2 * bw), F32)
        wla = wla.at[:B_LORA_W, :bw].set(rwkv_w2[j]).at[B_LORA_W:, bw:].set(rwkv_a2[j]).astype(BF16)
        even.append(dict(
            wcat=wcat, conv_w=mlstm_conv_w[j], conv_b=mlstm_conv_b[j],
            gate_b=jnp.pad(mlstm_gate_b[j], (0, LANES - 2 * A_HEADS)).reshape(1, LANES),
            mu=rwkv_mu[j], w0=rwkv_w0[j], a0=rwkv_a0[j], wla=wla, g2=rwkv_g2[j].astype(BF16),
            k_k=rwkv_k_k[j], k_a=rwkv_k_a[j], r_k=rwkv_r_k[j], ln_g=rwkv_ln_g[j], ln_b=rwkv_ln_b[j],
            a_norm_g=mlstm_norm_g[j], w_out=even_w_out[j].astype(BF16)))
    odd = [dict(w_in=odd_w_in[j].astype(BF16), w_out=odd_w_out[j].astype(BF16), norm_g=hgrn_norm_g[j])
           for j in range(odd_w_in.shape[0])]
    peer = dict(wqt=jnp.swapaxes(peer_w_q, 1, 2).astype(BF16), keys=peer_keys.astype(BF16), u=peer_u.astype(BF16),
                vt=jnp.swapaxes(peer_v, 1, 2).astype(BF16))
    return dict(norm_mix_g=norm_mix_g, norm_ffn_g=norm_ffn_g, norm_final_g=norm_final_g, even=even, odd=odd,
                peer=peer, lb_logits=hgrn_lb_logits)


def kernel(x_prompt, x_sample, c_prompt, c_sample, state_mlstm_C, state_mlstm_n, state_mlstm_m, state_mlstm_conv, state_rwkv_S, state_rwkv_shift, state_hgrn_S, norm_mix_g, norm_ffn_g, norm_final_g, ada_w, ada_b, even_w_in, even_w_out, mlstm_conv_w, mlstm_conv_b, mlstm_gate_b, mlstm_norm_g, rwkv_mu, rwkv_w0, rwkv_w2, rwkv_a0, rwkv_a2, rwkv_g2, rwkv_k_k, rwkv_k_a, rwkv_r_k, rwkv_ln_g, rwkv_ln_b, odd_w_in, odd_w_out, hgrn_lb_logits, hgrn_norm_g, peer_w_q, peer_keys, peer_u, peer_v):
    w = _prepare_weights(norm_mix_g, norm_ffn_g, norm_final_g, even_w_in, even_w_out, mlstm_conv_w, mlstm_conv_b,
                         mlstm_gate_b, mlstm_norm_g, rwkv_mu, rwkv_w0, rwkv_w2, rwkv_a0, rwkv_a2, rwkv_g2, rwkv_k_k,
                         rwkv_k_a, rwkv_r_k, rwkv_ln_g, rwkv_ln_b, odd_w_in, odd_w_out, hgrn_lb_logits, hgrn_norm_g,
                         peer_w_q, peer_keys, peer_u, peer_v)
    bp = x_prompt.shape[0]
    mod = _ada(jnp.concatenate([c_prompt, c_sample], axis=0), ada_w, ada_b)
    out_p = _trunk(x_prompt, mod[:, :bp], None, w, True)
    st = (state_mlstm_C, state_mlstm_n, state_mlstm_m, state_mlstm_conv, state_rwkv_S, state_rwkv_shift,
          state_hgrn_S)
    out_s = _trunk(x_sample, mod[:, bp:], st, w, False)
    return (out_p[0], out_s[0]) + tuple(out_p[1:]) + tuple(out_s[1:])
```

```python
import functools
import math

import jax
import jax.numpy as jnp
from jax import lax
from jax.experimental import pallas as pl
from jax.experimental.pallas import tpu as pltpu

F32 = jnp.float32
BF16 = jnp.bfloat16

EPS = 1e-6
A_HEADS = 4
A_DK = 128
A_CONV = 4
B_HEADS = 8
B_HEAD = 64
B_LORA_W = 64
B_LORA_A = 64
B_LORA_G = 128
B_LN_EPS = 64e-5
C_HEADS = 8
C_DK = 128
P_HEADS = 8
P_NKEYS = 128
P_TOPK = 16

LANES = 128
SUBLANES = 8
MXU_WIDTH = 256
VMEM_LIMIT = 56 * 1024 * 1024
NEG_INF = float("-inf")


def _bdot(a, b):
    return jnp.dot(a.astype(BF16), b.astype(BF16), preferred_element_type=F32)


def _dot_nt(a, b):
    return lax.dot_general(a.astype(BF16), b.astype(BF16), (((1,), (1,)), ((), ())), preferred_element_type=F32)


def _dot_tn(a, b):
    return lax.dot_general(a.astype(BF16), b.astype(BF16), (((0,), (0,)), ((), ())), preferred_element_type=F32)


def _split(x, terms):
    parts = []
    rem = x
    for _ in range(terms):
        p = rem.astype(BF16)
        parts.append(p)
        rem = rem - p.astype(F32)
    return parts


def _dot_split(a, b_bf, terms):
    out = None
    for p in _split(a, terms):
        d = jnp.dot(p, b_bf, preferred_element_type=F32)
        out = d if out is None else out + d
    return out


def _dot_split_left(a_bf, b, terms):
    out = None
    for p in _split(b, terms):
        d = jnp.dot(a_bf, p, preferred_element_type=F32)
        out = d if out is None else out + d
    return out


def _sigmoid(x):
    return 1.0 / (1.0 + jnp.exp(-x))


def _silu(x):
    return x * _sigmoid(x)


def _softplus(x):
    return jnp.maximum(x, 0.0) + jnp.log(1.0 + jnp.exp(-jnp.abs(x)))


def _gelu_tanh(x):
    return 0.5 * x * (1.0 + jnp.tanh(math.sqrt(2.0 / math.pi) * (x + 0.044715 * (x * x * x))))


def _rmsnorm(x, g):
    return x * lax.rsqrt(jnp.mean(x * x, axis=-1, keepdims=True) + EPS) * g


def _group_matrix(n, group, value):
    r = lax.broadcasted_iota(jnp.int32, (n, n), 0) // group
    c = lax.broadcasted_iota(jnp.int32, (n, n), 1) // group
    return jnp.where(r == c, value, 0.0).astype(BF16)


def _full_spec(arr):
    nd = arr.ndim
    return pl.BlockSpec(arr.shape, lambda *_: (0,) * nd)


def _params(n_grid):
    return pltpu.CompilerParams(dimension_semantics=("arbitrary",) * n_grid, vmem_limit_bytes=VMEM_LIMIT)


def _tok_spec(tt, width):
    return pl.BlockSpec((1, tt, width), lambda g, i: (g, i, 0))


def _mod_spec(mod3, tt):
    width = mod3.shape[2]
    if mod3.shape[1] == 1:
        return pl.BlockSpec((1, 1, width), lambda g, i: (g, 0, 0))
    return pl.BlockSpec((1, tt, width), lambda g, i: (g, i, 0))


def _ada_body(c_ref, w_ref, b_ref, o_ref):
    c = c_ref[...]
    o_ref[0] = _bdot(_silu(c), w_ref[0]) + b_ref[0]


def _ada(c_all, ada_w, ada_b):
    depth, d, n6 = ada_w.shape
    bc = c_all.shape[0]
    tn = n6 // 4
    return pl.pallas_call(
        _ada_body,
        grid=(depth, n6 // tn),
        in_specs=[pl.BlockSpec((bc, d), lambda l, j: (0, 0)),
                  pl.BlockSpec((1, d, tn), lambda l, j: (l, 0, j)),
                  pl.BlockSpec((1, 1, tn), lambda l, j: (l, 0, j))],
        out_specs=pl.BlockSpec((1, bc, tn), lambda l, j: (l, 0, j)),
        out_shape=jax.ShapeDtypeStruct((depth, bc, n6), F32),
        compiler_params=_params(2),
        name="ada_mod",
    )(c_all, ada_w, ada_b.reshape(depth, 1, n6))


def _even_pre_body(seq, tt, d, aw, bw, *refs):
    n_in = 15 if seq else 19
    ins, outs = refs[:n_in], refs[n_in:]
    (x_ref, mod_ref, ng_ref, w_ref, cw_ref, cb_ref, gb_ref, mu_ref, w0_ref, a0_ref, wla_ref, g2_ref,
     kk_ref, ka_ref, gs_ref) = ins[:15]
    (q_ref, k_ref, v_ref, so_ref, g8_ref, g8t_ref, r_ref, dec_ref, kt_ref, vb_ref, al_ref, be_ref, gg_ref,
     ctail_ref, stail_ref) = outs[:15]
    a2 = 2 * aw
    sw = 3 * bw + B_LORA_W + B_LORA_A + B_LORA_G

    x = x_ref[0]
    mod = mod_ref[0]
    h = _rmsnorm(x, ng_ref[...]) * (1.0 + mod[:, d:2 * d]) + mod[:, 0:d]
    p = jnp.dot(h.astype(BF16), w_ref[...], preferred_element_type=F32)
    u = p[:, 0:a2]
    v = p[:, a2:a2 + aw]
    o = p[:, a2 + aw:a2 + 2 * aw]
    pb = p[:, 4 * aw:4 * aw + sw]
    gt = p[:, 4 * aw + sw:4 * aw + sw + LANES]

    if seq:
        ubuf, pbuf = outs[15], outs[16]

        @pl.when(pl.program_id(1) == 0)
        def _():
            ubuf[0:SUBLANES, :] = jnp.zeros((SUBLANES, a2), F32)
            pbuf[0:SUBLANES, :] = jnp.zeros((SUBLANES, sw), F32)

        ubuf[SUBLANES:SUBLANES + tt, :] = u
        pbuf[SUBLANES:SUBLANES + tt, :] = pb
        u1 = ubuf[SUBLANES - 1:SUBLANES - 1 + tt, :]
        u2 = ubuf[SUBLANES - 2:SUBLANES - 2 + tt, :]
        u3 = ubuf[SUBLANES - 3:SUBLANES - 3 + tt, :]
        pprev = pbuf[SUBLANES - 1:SUBLANES - 1 + tt, :]
        ubuf[0:SUBLANES, :] = u[tt - SUBLANES:tt, :]
        pbuf[0:SUBLANES, :] = pb[tt - SUBLANES:tt, :]
        ctail_ref[0] = u[tt - SUBLANES:tt, :]
        stail_ref[0] = pb[tt - SUBLANES:tt, :]
    else:
        u3, u2, u1, pprev = ins[15][0], ins[16][0], ins[17][0], ins[18][0]
        ctail_ref[0] = u
        stail_ref[0] = pb

    cw = cw_ref[...]
    y = cb_ref[...] + u3 * cw[0:1, :] + u2 * cw[1:2, :] + u1 * cw[2:3, :] + u * cw[3:4, :]
    qk = _silu(y)
    q_ref[0] = qk[:, 0:aw] * (A_DK ** -0.5)
    k_ref[0] = qk[:, aw:a2]
    v_ref[0] = v
    so_ref[0] = _sigmoid(o)
    g = gt + gb_ref[...]
    lane = lax.broadcasted_iota(jnp.int32, g.shape, 1)
    gates = jnp.where(lane < A_HEADS, g, -_softplus(-g))
    g8_ref[0] = gates[:, 0:2 * A_HEADS]
    g8t_ref[0] = gates.T[0:2 * A_HEADS, :]

    xb = pb + mu_ref[...] * (pprev - pb)
    r = xb[:, 0:bw]
    kb = xb[:, bw:2 * bw]
    vb = xb[:, 2 * bw:3 * bw]
    la = xb[:, 3 * bw:3 * bw + LANES]
    gl = xb[:, 3 * bw + LANES:3 * bw + 2 * LANES]
    lane2 = lax.broadcasted_iota(jnp.int32, la.shape, 1)
    la_act = jnp.where(lane2 < B_LORA_W, jnp.tanh(la), la)
    lw = jnp.dot(la_act.astype(BF16), wla_ref[...], preferred_element_type=F32)
    wlog = -_softplus(-(w0_ref[...] + lw[:, 0:bw])) - 0.5
    decay = jnp.exp(-jnp.exp(wlog))
    a = _sigmoid(a0_ref[...] + lw[:, bw:2 * bw])
    gg = jnp.dot(_sigmoid(gl).astype(BF16), g2_ref[...], preferred_element_type=F32)
    kk = kb * kk_ref[...]
    ss = _dot_split(kk * kk, gs_ref[...], 2)
    kkn = kk * lax.rsqrt(jnp.maximum(ss, 1e-24))
    r_ref[0] = r
    dec_ref[0] = decay
    kt_ref[0] = kb * (1.0 + (a - 1.0) * ka_ref[...])
    vb_ref[0] = vb
    al_ref[0] = -kkn
    be_ref[0] = kkn * a
    gg_ref[0] = gg


def _even_pre(x3, mod3, ng, wcat, cw, cb, gb, mu, w0, a0, wla, g2, k_k, k_a, prev, tt):
    g_, ttot, d = x3.shape
    aw = d // 2
    bw = d // 2
    a2 = 2 * aw
    sw = 3 * bw + B_LORA_W + B_LORA_A + B_LORA_G
    seq = prev is None
    gsum = _group_matrix(bw, B_HEAD, 1.0)
    consts = [ng, wcat, cw, cb, gb, mu, w0, a0, wla, g2, k_k, k_a, gsum]
    ins = [x3, mod3] + consts
    in_specs = [_tok_spec(tt, d), _mod_spec(mod3, tt)] + [_full_spec(c) for c in consts]
    if not seq:
        ins += list(prev)
        in_specs += [_tok_spec(tt, a2)] * 3 + [_tok_spec(tt, sw)]
    nt = ttot // tt
    tail_rows = SUBLANES if seq else tt
    tail_tot = g_ * SUBLANES if seq else ttot

    def tail_spec(width):
        if seq:
            return pl.BlockSpec((1, SUBLANES, width), lambda g, i: (g, 0, 0))
        return _tok_spec(tt, width)

    def tail_shape(width):
        if seq:
            return jax.ShapeDtypeStruct((g_, SUBLANES, width), F32)
        return jax.ShapeDtypeStruct((g_, ttot, width), F32)

    tok = lambda w: jax.ShapeDtypeStruct((g_, ttot, w), F32)
    out_shape = [tok(aw), tok(aw), tok(aw), tok(aw), tok(2 * A_HEADS),
                 jax.ShapeDtypeStruct((g_, 2 * A_HEADS, ttot), F32),
                 tok(bw), tok(bw), tok(bw), tok(bw), tok(bw), tok(bw), tok(bw),
                 tail_shape(a2), tail_shape(sw)]
    out_specs = [_tok_spec(tt, aw)] * 4 + [_tok_spec(tt, 2 * A_HEADS),
                                           pl.BlockSpec((1, 2 * A_HEADS, tt), lambda g, i: (g, 0, i))]
    out_specs += [_tok_spec(tt, bw)] * 7 + [tail_spec(a2), tail_spec(sw)]
    scratch = [pltpu.VMEM((tt + SUBLANES, a2), F32), pltpu.VMEM((tt + SUBLANES, sw), F32)] if seq else []
    del tail_rows, tail_tot
    return pl.pallas_call(
        functools.partial(_even_pre_body, seq, tt, d, aw, bw),
        grid=(g_, nt),
        in_specs=in_specs,
        out_specs=out_specs,
        out_shape=out_shape,
        scratch_shapes=scratch,
        compiler_params=_params(2),
        name="even_pre_seq" if seq else "even_pre_step",
    )(*ins)


def _mlstm_chunk_body(L, q_ref, k_ref, v_ref, g8_ref, g8t_ref, c0_ref, n0_ref, m0_ref,
                      h_ref, c1_ref, n1_ref, m1_ref, C_s, n_s, m_s):
    c = pl.program_id(1)

    @pl.when(c == 0)
    def _():
        C_s[...] = c0_ref[0]
        n_s[...] = n0_ref[0]
        m_s[...] = m0_ref[0]

    row = lax.broadcasted_iota(jnp.int32, (L, L), 0)
    col = lax.broadcasted_iota(jnp.int32, (L, L), 1)
    causal = col <= row
    tril = jnp.where(causal, 1.0, 0.0).astype(BF16)
    triu = jnp.where(row <= col, 1.0, 0.0).astype(BF16)
    g8 = g8_ref[0]
    g8t = g8t_ref[0]
    bcols = _dot_split_left(tril, g8, 3)
    brows = _dot_split(g8t, triu, 3)
    for hd in range(A_HEADS):
        sl = slice(hd * A_DK, (hd + 1) * A_DK)
        q = q_ref[0, :, sl]
        k = k_ref[0, :, sl]
        v = v_ref[0, :, sl]
        i_row = g8t[hd:hd + 1, :]
        i_col = g8[:, hd:hd + 1]
        b_row = brows[A_HEADS + hd:A_HEADS + hd + 1, :]
        b_col = bcols[:, A_HEADS + hd:A_HEADS + hd + 1]
        m_prev = m_s[hd:hd + 1, 0:1]
        C = C_s[hd]
        n = n_s[hd:hd + 1, :]
        dmat = jnp.where(causal, b_col - b_row + i_row, NEG_INF)
        inter = b_col + m_prev
        m_t = jnp.maximum(inter, jnp.max(dmat, axis=1, keepdims=True))
        w_intra = jnp.exp(dmat - m_t)
        w_inter = jnp.exp(inter - m_t)
        s = _dot_nt(q, k) * w_intra
        num = _bdot(s, v) + w_inter * _bdot(q, C)
        den = jnp.sum(s, axis=1, keepdims=True) + w_inter * jnp.sum(q * n, axis=1, keepdims=True)
        h_ref[0, :, sl] = num / jnp.maximum(jnp.abs(den), jnp.exp(-m_t))
        m_new = m_t[L - 1:L, :]
        b_last = b_col[L - 1:L, :]
        kw = k * jnp.exp(b_last - b_col + i_col - m_new)
        carry = jnp.exp(b_last + m_prev - m_new)
        C_s[hd] = carry * C + _dot_tn(kw, v)
        n_s[hd:hd + 1, :] = carry * n + jnp.sum(kw, axis=0, keepdims=True)
        m_s[hd:hd + 1, :] = jnp.broadcast_to(m_new, (1, LANES))

    @pl.when(c == pl.num_programs(1) - 1)
    def _():
        c1_ref[0] = C_s[...]
        n1_ref[0] = n_s[...]
        m1_ref[0] = m_s[...]


def _mlstm_chunk(q, k, v, g8, g8t, C0, n0p, m0p, L):
    b, t, aw = q.shape
    st = lambda *s: pl.BlockSpec((1,) + s, lambda bb, c: (bb,) + (0,) * len(s))
    return pl.pallas_call(
        functools.partial(_mlstm_chunk_body, L),
        grid=(b, t // L),
        in_specs=[_tok_spec(L, aw)] * 3 + [_tok_spec(L, 2 * A_HEADS),
                                           pl.BlockSpec((1, 2 * A_HEADS, L), lambda bb, c: (bb, 0, c)),
                                           st(A_HEADS, A_DK, A_DK), st(SUBLANES, LANES), st(SUBLANES, LANES)],
        out_specs=[_tok_spec(L, aw), st(A_HEADS, A_DK, A_DK), st(SUBLANES, LANES), st(SUBLANES, LANES)],
        out_shape=[jax.ShapeDtypeStruct((b, t, aw), F32), jax.ShapeDtypeStruct(C0.shape, F32),
                   jax.ShapeDtypeStruct(n0p.shape, F32), jax.ShapeDtypeStruct(m0p.shape, F32)],
        scratch_shapes=[pltpu.VMEM((A_HEADS, A_DK, A_DK), F32), pltpu.VMEM((SUBLANES, LANES), F32),
                        pltpu.VMEM((SUBLANES, LANES), F32)],
        compiler_params=_params(2),
        name="mlstm_chunk",
    )(q, k, v, g8, g8t, C0, n0p, m0p)


def _row8(x):
    r = lax.broadcasted_iota(jnp.int32, (SUBLANES, x.shape[1]), 0)
    return jnp.where(r == 0, jnp.broadcast_to(x, (SUBLANES, x.shape[1])), 0.0)


def _mlstm_step_body(bb, q_ref, k_ref, v_ref, g8_ref, n0_ref, m0_ref, c0_ref, *rest):
    h_ref, c1_ref, n1_ref, m1_ref = rest[-4:]
    g8 = g8_ref[...]
    m0 = m0_ref[...]
    for hd in range(A_HEADS):
        sl = slice(hd * A_DK, (hd + 1) * A_DK)
        li = g8[:, hd:hd + 1]
        lf = g8[:, A_HEADS + hd:A_HEADS + hd + 1]
        mp = m0[:, hd:hd + 1]
        m1 = jnp.maximum(lf + mp, li)
        wi = jnp.exp(li - m1)
        wf = jnp.exp(lf + mp - m1)
        floor = jnp.exp(-m1)
        for b in range(bb):
            q = q_ref[b:b + 1, sl]
            k = k_ref[b:b + 1, sl]
            v = v_ref[b:b + 1, sl]
            wib = wi[b:b + 1, :]
            wfb = wf[b:b + 1, :]
            C1 = wfb * c0_ref[0, b, hd] + wib * _dot_tn(_row8(k), _row8(v))
            n1 = wfb * n0_ref[b, hd:hd + 1, :] + wib * k
            num = _bdot(_row8(q), C1)[0:1, :]
            den = jnp.sum(q * n1, axis=1, keepdims=True)
            h_ref[b:b + 1, sl] = num / jnp.maximum(jnp.abs(den), floor[b:b + 1, :])
            c1_ref[0, b, hd] = C1
            n1_ref[b, hd:hd + 1, :] = n1
        m1_ref[:, hd:hd + 1] = m1


def _rwkv_body(bb, tb, ty, al_ref, w_ref, be_ref, kt_ref, r_ref, v_ref, s0_ref, y_ref, s1_ref, S_s, t3_s, y_s):
    @pl.when(pl.program_id(1) == 0)
    def _():
        S_s[...] = s0_ref[...]

    npair = B_HEADS // 2
    ntile = bb * npair
    nrow = min(tb, SUBLANES)
    row = lax.broadcasted_iota(jnp.int32, (B_HEAD, LANES), 0)
    lane = lax.broadcasted_iota(jnp.int32, (B_HEAD, LANES), 1)
    eye_pair = row == (lane % B_HEAD)
    wred = _group_matrix(2 * LANES, B_HEAD, 1.0)
    kk = lax.broadcasted_iota(jnp.int32, (nrow * LANES, LANES), 0)
    ll = lax.broadcasted_iota(jnp.int32, (nrow * LANES, LANES), 1)
    sel_head = ((kk % LANES) // B_HEAD) == (ll // B_HEAD)
    sel_off = (ll % B_HEAD) - (kk // LANES)

    def group(t8, carry):
        base = t8 * nrow if isinstance(t8, int) else pl.multiple_of(t8 * nrow, nrow)
        rows = pl.ds(base, nrow)
        for i in range(nrow):
            ri = slice(i, i + 1)
            lhs = []
            for b in range(bb):
                for p in range(npair):
                    sl = pl.ds(p * LANES, LANES)
                    lhs.append(jnp.concatenate(
                        [(S_s[b, p] * al_ref[b, rows, sl][ri]).astype(BF16),
                         jnp.where(eye_pair, v_ref[b, rows, sl][ri], 0.0).astype(BF16)], axis=1))
            red = jnp.dot(jnp.concatenate(lhs, axis=0), wred, preferred_element_type=F32)
            for b in range(bb):
                for p in range(npair):
                    sl = pl.ds(p * LANES, LANES)
                    idx = b * npair + p
                    sa = red[idx * B_HEAD:(idx + 1) * B_HEAD, 0:LANES]
                    vcol = red[idx * B_HEAD:(idx + 1) * B_HEAD, LANES:2 * LANES]
                    S = (S_s[b, p] * w_ref[b, rows, sl][ri] + sa * be_ref[b, rows, sl][ri]
                         + vcol * kt_ref[b, rows, sl][ri])
                    S_s[b, p] = S
                    t3_s[idx * B_HEAD:(idx + 1) * B_HEAD, i * LANES:(i + 1) * LANES] = (
                        S * r_ref[b, rows, sl][ri]).astype(BF16)
        wsel = jnp.where(sel_head & (sel_off == base % ty), 1.0, 0.0).astype(BF16)
        yg = jnp.dot(t3_s[...], wsel, preferred_element_type=F32)

        @pl.when(base % ty == 0)
        def _():
            y_s[...] = yg

        @pl.when(base % ty != 0)
        def _():
            y_s[...] += yg

        @pl.when((base + nrow) % ty == 0)
        def _():
            y_ref[0, base // ty] = y_s[...]
        return carry

    if tb == nrow:
        group(0, 0)
    else:
        lax.fori_loop(0, tb // nrow, group, 0)

    @pl.when(pl.program_id(1) == pl.num_programs(1) - 1)
    def _():
        s1_ref[...] = S_s[...]


def _rwkv_scan(al, w, be, kt, r, v, S0p, bb, tb):
    b, t, bw = al.shape
    npair = B_HEADS // 2
    ty = min(B_HEAD, t)
    nb = b // bb
    tok = pl.BlockSpec((bb, tb, bw), lambda i, j: (i, j, 0))
    st = pl.BlockSpec((bb, npair, B_HEAD, LANES), lambda i, j: (i, 0, 0, 0))
    nrow = min(tb, SUBLANES)
    rows = bb * npair * B_HEAD
    yt, S1p = pl.pallas_call(
        functools.partial(_rwkv_body, bb, tb, ty),
        grid=(nb, t // tb),
        in_specs=[tok] * 6 + [st],
        out_specs=[pl.BlockSpec((1, tb // ty, rows, LANES), lambda i, j: (i, j, 0, 0)), st],
        out_shape=[jax.ShapeDtypeStruct((nb, t // ty, rows, LANES), F32), jax.ShapeDtypeStruct(S0p.shape, F32)],
        scratch_shapes=[pltpu.VMEM((bb, npair, B_HEAD, LANES), F32), pltpu.VMEM((rows, nrow * LANES), BF16),
                        pltpu.VMEM((rows, LANES), F32)],
        compiler_params=_params(2),
        name="rwkv_scan",
    )(al, w, be, kt, r, v, S0p)
    y = jnp.stack([yt[..., 0:ty], yt[..., B_HEAD:B_HEAD + ty]], axis=3)
    y = y.reshape(nb, t // ty, bb, npair, B_HEAD, 2, ty).transpose(0, 2, 1, 6, 3, 5, 4).reshape(b, t, bw)
    return y, S1p


def _even_post_body(d, aw, bw, x_ref, mod_ref, ha_ref, so_ref, y_ref, r_ref, kt_ref, vb_ref, gg_ref,
                    ang_ref, lng_ref, lnb_ref, rk_ref, wo_ref, o_ref):
    ga = _group_matrix(aw, A_DK, 1.0 / A_DK)
    gb_mean = _group_matrix(bw, B_HEAD, 1.0 / B_HEAD)
    gb_sum = _group_matrix(bw, B_HEAD, 1.0)
    ha = ha_ref[0]
    ha = ha * lax.rsqrt(_dot_split(ha * ha, ga, 2) + EPS) * ang_ref[...] * so_ref[0]
    y = y_ref[0]
    yc = y - _dot_split(y, gb_mean, 2)
    var = _dot_split(yc * yc, gb_mean, 2)
    yn = yc * lax.rsqrt(var + B_LN_EPS) * lng_ref[...] + lnb_ref[...]
    bonus = _dot_split(r_ref[0] * kt_ref[0] * rk_ref[...], gb_sum, 2)
    y2 = (yn + bonus * vb_ref[0]) * gg_ref[0]
    out = (jnp.dot(ha.astype(BF16), wo_ref[0:aw, :], preferred_element_type=F32)
           + jnp.dot(y2.astype(BF16), wo_ref[aw:aw + bw, :], preferred_element_type=F32))
    o_ref[0] = x_ref[0] + mod_ref[0][:, 2 * d:3 * d] * out


def _even_post(x3, mod3, ha, so, y, r, kt, vb, gg, ang, lng, lnb, rk, wo, tt):
    g_, ttot, d = x3.shape
    aw = d // 2
    bw = d // 2
    consts = [ang, lng, lnb, rk, wo]
    return pl.pallas_call(
        functools.partial(_even_post_body, d, aw, bw),
        grid=(g_, ttot // tt),
        in_specs=[_tok_spec(tt, d), _mod_spec(mod3, tt)] + [_tok_spec(tt, aw)] * 7 + [_full_spec(c) for c in consts],
        out_specs=_tok_spec(tt, d),
        out_shape=jax.ShapeDtypeStruct(x3.shape, F32),
        compiler_params=_params(2),
        name="even_post",
    )(x3, mod3, ha, so, y, r, kt, vb, gg, *consts)


def _odd_pre_body(j, d, x_ref, mod_ref, ng_ref, w_ref, lbl_ref, q_ref, k_ref, v_ref, lf_ref, sg_ref):
    x = x_ref[0]
    mod = mod_ref[0]
    h = _rmsnorm(x, ng_ref[...]) * (1.0 + mod[:, d:2 * d]) + mod[:, 0:d]
    p = jnp.dot(h.astype(BF16), w_ref[...], preferred_element_type=F32)
    lbl = lbl_ref[...]
    n_odd = lbl.shape[0]
    mx = lbl[0:1, :]
    for i in range(1, n_odd):
        mx = jnp.maximum(mx, lbl[i:i + 1, :])
    ex = [jnp.exp(lbl[i:i + 1, :] - mx) for i in range(n_odd)]
    tot = ex[0]
    for i in range(1, n_odd):
        tot = tot + ex[i]
    lb = jnp.zeros_like(mx)
    for i in range(1, j + 1):
        lb = lb + ex[i] / tot
    f = lb + (1.0 - lb) * _sigmoid(p[:, d:2 * d])
    q_ref[0] = _silu(p[:, 0:d])
    k_ref[0] = 1.0 - f
    v_ref[0] = p[:, 2 * d:3 * d]
    lf_ref[0] = jnp.log(f)
    sg_ref[0] = _silu(p[:, 3 * d:4 * d])


def _odd_pre(x3, mod3, ng, w_in, lbl, j, tt):
    g_, ttot, d = x3.shape
    consts = [ng, w_in, lbl]
    return pl.pallas_call(
        functools.partial(_odd_pre_body, j, d),
        grid=(g_, ttot // tt),
        in_specs=[_tok_spec(tt, d), _mod_spec(mod3, tt)] + [_full_spec(c) for c in consts],
        out_specs=[_tok_spec(tt, d)] * 5,
        out_shape=[jax.ShapeDtypeStruct(x3.shape, F32)] * 5,
        compiler_params=_params(2),
        name="odd_pre",
    )(x3, mod3, *consts)


def _col_bcast(row, terms):
    parts = _split(row, terms)
    r = lax.broadcasted_iota(jnp.int32, (SUBLANES, row.shape[1]), 0)
    lhs = jnp.zeros((SUBLANES, row.shape[1]), F32)
    for i, p in enumerate(parts):
        lhs = jnp.where(r == i, jnp.broadcast_to(p.astype(F32), lhs.shape), lhs)
    ones = jnp.where(lax.broadcasted_iota(jnp.int32, (SUBLANES, LANES), 0) < terms, 1.0, 0.0).astype(BF16)
    return lax.dot_general(lhs.astype(BF16), ones, (((0,), (0,)), ((), ())), preferred_element_type=F32)


def _hgrn_chunk_body(L, sub, nchunk, hg, q_ref, k_ref, v_ref, g_ref, s0_ref, o_ref, s1_ref, S_s):
    @pl.when(pl.program_id(2) == 0)
    def _():
        S_s[...] = s0_ref[0]

    nj = L // sub
    rowi = lax.broadcasted_iota(jnp.int32, (L, C_DK), 0)
    rr = lax.broadcasted_iota(jnp.int32, (L, L), 0)
    cc = lax.broadcasted_iota(jnp.int32, (L, L), 1)
    tril = jnp.where(cc <= rr, 1.0, 0.0).astype(BF16)
    ones = jnp.ones((C_DK, LANES), BF16)

    def chunk(ci, carry):
        base = pl.multiple_of(ci * L, L)
        rows = pl.ds(base, L)
        b_all = _dot_split_left(tril, g_ref[0, rows, :], 3)
        heads = []
        for hd in range(hg):
            sl = slice(hd * C_DK, (hd + 1) * C_DK)
            heads.append((q_ref[0, rows, sl], k_ref[0, rows, sl], v_ref[0, rows, sl], b_all[:, sl], S_s[hd]))
        outs = [_bdot(q * jnp.exp(b), S) for (q, k, v, b, S) in heads]
        for dlt in range(sub):
            prs, vss = [], []
            for (q, k, v, b, S) in heads:
                if dlt == 0:
                    prs.append((q * k).astype(BF16))
                    vss.append(v)
                else:
                    ok = (rowi % sub) >= dlt
                    e = jnp.exp(jnp.where(ok, b - pltpu.roll(b, dlt, 0), NEG_INF))
                    prs.append((q * pltpu.roll(k, dlt, 0) * e).astype(BF16))
                    vss.append(pltpu.roll(v, dlt, 0))
            rs = jnp.dot(jnp.concatenate(prs, axis=0), ones, preferred_element_type=F32)
            outs = [o + rs[i * L:(i + 1) * L, :] * vs for i, (o, vs) in enumerate(zip(outs, vss))]
        for hd, ((q, k, v, b, S), o) in enumerate(zip(heads, outs)):
            if nj > 1:
                amat = None
                for jj in range(nj - 1):
                    end = (jj + 1) * sub
                    bref = b[end - 1:end, :]
                    qj = q * jnp.exp(jnp.where(rowi >= end, b - bref, NEG_INF))
                    kj = k * jnp.exp(jnp.where((rowi >= end - sub) & (rowi < end), bref - b, NEG_INF))
                    a = _dot_nt(qj, kj)
                    amat = a if amat is None else amat + a
                o = o + _bdot(amat, v)
            o_ref[0, rows, hd * C_DK:(hd + 1) * C_DK] = o
            b_last = b[L - 1:L, :]
            S_s[hd] = _col_bcast(jnp.exp(b_last), 3) * S + _dot_tn(k * jnp.exp(b_last - b), v)
        return carry

    lax.fori_loop(0, nchunk, chunk, 0)

    @pl.when(pl.program_id(2) == pl.num_programs(2) - 1)
    def _():
        s1_ref[0] = S_s[...]


def _hgrn_chunk(q, k, v, lf, S0, tblk, L, sub, hg):
    b, t, d = q.shape
    tok = pl.BlockSpec((1, tblk, hg * C_DK), lambda bb, h, c: (bb, c, h))
    st = pl.BlockSpec((1, hg, C_DK, C_DK), lambda bb, h, c: (bb, h, 0, 0))
    return pl.pallas_call(
        functools.partial(_hgrn_chunk_body, L, sub, tblk // L, hg),
        grid=(b, C_HEADS // hg, t // tblk),
        in_specs=[tok] * 4 + [st],
        out_specs=[tok, st],
        out_shape=[jax.ShapeDtypeStruct((b, t, d), F32), jax.ShapeDtypeStruct(S0.shape, F32)],
        scratch_shapes=[pltpu.VMEM((hg, C_DK, C_DK), F32)],
        compiler_params=_params(3),
        name="hgrn_chunk",
    )(q, k, v, lf, S0)


def _hgrn_step_body(bb, q_ref, k_ref, v_ref, g_ref, s0_ref, *rest):
    o_ref, s1_ref = rest[-2], rest[-1]
    for b in range(bb):
        for hd in range(C_HEADS):
            sl = slice(hd * C_DK, (hd + 1) * C_DK)
            q = q_ref[b:b + 1, sl]
            k = k_ref[b:b + 1, sl]
            v = v_ref[b:b + 1, sl]
            f = jnp.exp(g_ref[b:b + 1, sl])
            S1 = _col_bcast(f, 3) * s0_ref[0, b, hd] + _dot_tn(_row8(k), _row8(v))
            o_ref[b:b + 1, sl] = _bdot(_row8(q), S1)[0:1, :]
            s1_ref[0, b, hd] = S1


def _layer_state_call(body, name, rows_in, row_out_shape, row_spec, state_all, layer, state_acc, bb, extra_out=()):
    b = state_all.shape[1]
    blk = (1, bb) + state_all.shape[2:]
    zeros = (0,) * (state_all.ndim - 2)
    st = pl.BlockSpec(blk, lambda i: (layer, i) + zeros)
    ins = list(rows_in) + [state_all]
    in_specs = [row_spec(r) for r in rows_in] + [st]
    aliases = {}
    if state_acc is not None:
        ins.append(state_acc)
        in_specs.append(pl.BlockSpec(memory_space=pl.ANY))
        aliases = {len(ins) - 1: 1}
    return pl.pallas_call(
        body,
        grid=(b // bb,),
        in_specs=in_specs,
        out_specs=[row_spec(row_out_shape), st] + [row_spec(e) for e in extra_out],
        out_shape=[jax.ShapeDtypeStruct(row_out_shape.shape, F32), jax.ShapeDtypeStruct(state_all.shape, F32)]
        + [jax.ShapeDtypeStruct(e.shape, F32) for e in extra_out],
        input_output_aliases=aliases,
        compiler_params=_params(1),
        name=name,
    )(*ins)


def _row_spec_for(bb):
    def spec(arr):
        blk = (bb,) + arr.shape[1:]
        zeros = (0,) * (arr.ndim - 1)
        return pl.BlockSpec(blk, lambda i: (i,) + zeros)
    return spec


def _mlstm_step(q, k, v, g8, n0, m0, C_all, layer, C_acc, bb):
    return _layer_state_call(functools.partial(_mlstm_step_body, bb), "mlstm_step", [q, k, v, g8, n0, m0], q,
                             _row_spec_for(bb), C_all, layer, C_acc, bb, extra_out=(n0, m0))


def _hgrn_step(q, k, v, lf, S_all, layer, S_acc, bb):
    return _layer_state_call(functools.partial(_hgrn_step_body, bb), "hgrn_step", [q, k, v, lf], q,
                             _row_spec_for(bb), S_all, layer, S_acc, bb)


def _odd_post_body(d, x_ref, mod_ref, o_ref, sg_ref, ng_ref, wo_ref, out_ref):
    gm = _group_matrix(d, C_DK, 1.0 / C_DK)
    o = o_ref[0]
    o = o * lax.rsqrt(_dot_split(o * o, gm, 2) + EPS) * ng_ref[...] * sg_ref[0]
    out = jnp.dot(o.astype(BF16), wo_ref[...], preferred_element_type=F32)
    out_ref[0] = x_ref[0] + mod_ref[0][:, 2 * d:3 * d] * out


def _odd_post(x3, mod3, o, sg, ng_tiled, wo, tt):
    g_, ttot, d = x3.shape
    consts = [ng_tiled, wo]
    return pl.pallas_call(
        functools.partial(_odd_post_body, d),
        grid=(g_, ttot // tt),
        in_specs=[_tok_spec(tt, d), _mod_spec(mod3, tt)] + [_tok_spec(tt, d)] * 2 + [_full_spec(c) for c in consts],
        out_specs=_tok_spec(tt, d),
        out_shape=jax.ShapeDtypeStruct(x3.shape, F32),
        compiler_params=_params(2),
        name="odd_post",
    )(x3, mod3, o, sg, *consts)


P_NCAND = P_TOPK + 1
P_CAND_ROWS = 3 * SUBLANES


def _top_sorted(s):
    ridx = lax.broadcasted_iota(jnp.int32, (P_CAND_ROWS, s.shape[1]), 0)
    acc = jnp.full((P_CAND_ROWS, s.shape[1]), NEG_INF, F32)
    cur = s
    for i in range(P_NCAND):
        m = jnp.max(cur, axis=0, keepdims=True)
        acc = jnp.where(ridx == i, m, acc)
        cur = jnp.where(cur == m, NEG_INF, cur)
    return acc


def _peer_body(tm, te, ne, d, x_ref, mod_ref, ng_ref, wqt_ref, keys_ref, u_ref, vt_ref, vl_ref, o_ref,
               ht_s, s1_s, e1_s, thr_s, e2_s, gate_s, w_s, acc_s):
    j = pl.program_id(2)
    dk2 = keys_ref.shape[4]
    nsub = te // P_NKEYS

    @pl.when(j == 0)
    def _():
        x = x_ref[0]
        mod = mod_ref[0]
        h = _rmsnorm(x, ng_ref[...]) * (1.0 + mod[:, 4 * d:5 * d]) + mod[:, 3 * d:4 * d]
        ht = h.T.astype(BF16)
        ht_s[...] = ht
        qt = jnp.dot(wqt_ref[0], ht, preferred_element_type=F32)
        r8 = lax.broadcasted_iota(jnp.int32, (SUBLANES, LANES), 0)
        for hd in range(P_HEADS):
            s1_all = _bdot(keys_ref[0, hd, 0], qt[(2 * hd) * dk2:(2 * hd + 1) * dk2, :])
            s2_all = _bdot(keys_ref[0, hd, 1], qt[(2 * hd + 1) * dk2:(2 * hd + 2) * dk2, :])
            for tc in range(tm // LANES):
                ls = slice(tc * LANES, (tc + 1) * LANES)
                s1 = s1_all[:, ls]
                s2 = s2_all[:, ls]
                a = _top_sorted(s1)
                b = _top_sorted(s2)
                blocks = [a[0:1, :] + b]
                for i in range(2, SUBLANES + 1):
                    blocks.append(jnp.where(r8 < P_NCAND // i, a[i - 1:i, :] + b[0:SUBLANES, :], NEG_INF))
                blocks.append(a[SUBLANES:P_CAND_ROWS, :] + b[0:1, :])
                cand = jnp.concatenate(blocks, axis=0)
                cur = cand
                best = None
                for _ in range(P_NCAND):
                    prev, best = best, jnp.max(cur, axis=0, keepdims=True)
                    cur = jnp.where(cur == best, NEG_INF, cur)
                tau = 0.5 * (prev + best)
                top = a[0:1, :] + b[0:1, :]
                z = jnp.sum(jnp.where(cand >= tau, jnp.exp(cand - top), 0.0), axis=0, keepdims=True)
                s1_s[hd, :, ls] = s1
                e1_s[hd, :, ls] = jnp.exp(s1 - a[0:1, :])
                thr_s[hd, :, ls] = tau - s2
                e2_s[hd, :, ls] = jnp.exp(s2 - b[0:1, :]) / z
        acc_s[...] = jnp.zeros_like(acc_s)
        w_s[1] = jnp.zeros((te, tm), BF16)

    assert nsub == SUBLANES
    grp = pl.ds(pl.multiple_of(j * nsub, nsub), nsub)

    def gates(tc):
        ls = slice(tc * LANES, (tc + 1) * LANES)
        s1g = [s1_s[hd, grp, ls] for hd in range(P_HEADS)]
        e1g = [e1_s[hd, grp, ls] for hd in range(P_HEADS)]
        for ii in range(nsub):
            rows = slice(ii * P_NKEYS, (ii + 1) * P_NKEYS)
            gate = jnp.zeros((P_NKEYS, LANES), F32)
            for hd in range(P_HEADS):
                keep = s1g[hd][ii:ii + 1, :] >= thr_s[hd, :, ls]
                gate = gate + jnp.where(keep, e2_s[hd, :, ls], 0.0) * e1g[hd][ii:ii + 1, :]
            gate_s[rows, ls] = gate

    piece = min(tm, MXU_WIDTH)
    tc_per_piece = piece // LANES
    for pc in range(tm // piece):
        cs = slice(pc * piece, (pc + 1) * piece)
        acc_s[:, cs] += jnp.dot(vt_ref[0], w_s[(j + 1) % 2, :, cs], preferred_element_type=F32)
        for tc in range(pc * tc_per_piece, pc * tc_per_piece + (tc_per_piece + 1) // 2):
            gates(tc)
        act = jnp.dot(u_ref[0], ht_s[:, cs], preferred_element_type=F32)
        for tc in range(pc * tc_per_piece + (tc_per_piece + 1) // 2, (pc + 1) * tc_per_piece):
            gates(tc)
        w_s[j % 2, :, cs] = (gate_s[:, cs] * _gelu_tanh(act)).astype(BF16)

    @pl.when(j == ne - 1)
    def _():
        acc = acc_s[...] + jnp.dot(vl_ref[0], w_s[(ne - 1) % 2], preferred_element_type=F32)
        o_ref[0] = x_ref[0] + mod_ref[0][:, 5 * d:6 * d] * acc.T


def _peer(x3, mod3, ng, wqt_all, keys_all, u_all, vt_all, layer, tm, te):
    g_, ttot, d = x3.shape
    ne = u_all.shape[1] // te
    nk = P_NKEYS
    if mod3.shape[1] == 1:
        mod_spec = pl.BlockSpec((1, 1, mod3.shape[2]), lambda g, i, j: (g, 0, 0))
    else:
        mod_spec = pl.BlockSpec((1, tm, mod3.shape[2]), lambda g, i, j: (g, i, 0))
    return pl.pallas_call(
        functools.partial(_peer_body, tm, te, ne, d),
        grid=(g_, ttot // tm, ne),
        in_specs=[pl.BlockSpec((1, tm, d), lambda g, i, j: (g, i, 0)), mod_spec,
                  pl.BlockSpec(ng.shape, lambda g, i, j: (0, 0)),
                  pl.BlockSpec((1,) + wqt_all.shape[1:], lambda g, i, j: (layer, 0, 0)),
                  pl.BlockSpec((1,) + keys_all.shape[1:], lambda g, i, j: (layer, 0, 0, 0, 0)),
                  pl.BlockSpec((1, te, d), lambda g, i, j: (layer, j, 0)),
                  pl.BlockSpec((1, d, te), lambda g, i, j: (layer, 0, jnp.maximum(j - 1, 0))),
                  pl.BlockSpec((1, d, te), lambda g, i, j: (layer, 0, ne - 1))],
        out_specs=pl.BlockSpec((1, tm, d), lambda g, i, j: (g, i, 0)),
        out_shape=jax.ShapeDtypeStruct(x3.shape, F32),
        scratch_shapes=[pltpu.VMEM((d, tm), BF16),
                        pltpu.VMEM((P_HEADS, nk, tm), F32), pltpu.VMEM((P_HEADS, nk, tm), F32),
                        pltpu.VMEM((P_HEADS, nk, tm), F32), pltpu.VMEM((P_HEADS, nk, tm), F32),
                        pltpu.VMEM((te, tm), F32), pltpu.VMEM((2, te, tm), BF16), pltpu.VMEM((d, tm), F32)],
        compiler_params=_params(3),
        name="peer",
    )(x3, mod3, ng, wqt_all, keys_all, u_all, vt_all, vt_all)


def _final_norm_body(x_ref, g_ref, o_ref):
    o_ref[0] = _rmsnorm(x_ref[0], g_ref[...])


def _final_norm(x3, g, tt):
    g_, ttot, d = x3.shape
    return pl.pallas_call(
        _final_norm_body,
        grid=(g_, ttot // tt),
        in_specs=[_tok_spec(tt, d), _full_spec(g)],
        out_specs=_tok_spec(tt, d),
        out_shape=jax.ShapeDtypeStruct(x3.shape, F32),
        compiler_params=_params(2),
        name="final_norm",
    )(x3, g)


def _pair_state(S):
    b = S.shape[0]
    return S.reshape(b, B_HEADS // 2, 2, B_HEAD, B_HEAD).transpose(0, 1, 3, 2, 4).reshape(b, B_HEADS // 2, B_HEAD, LANES)


def _unpair_state(Sp):
    b = Sp.shape[0]
    return Sp.reshape(b, B_HEADS // 2, B_HEAD, 2, B_HEAD).transpose(0, 1, 3, 2, 4).reshape(b, B_HEADS, B_HEAD, B_HEAD)


def _trunk(x, mod_all, st, w, seq):
    bsz, t, d = x.shape
    depth = mod_all.shape[0]
    aw = d // 2
    bw = d // 2
    if seq:
        x3 = x
        tt = min(256, t)
        tm = min(512, t)
        mods = [mod_all[l][:, None, :] for l in range(depth)]
    else:
        x3 = x.reshape(1, bsz * t, d)
        tt = bsz * t
        tm = bsz * t
        mods = [mod_all[l][None] for l in range(depth)]
    row = lambda v_: v_.reshape(1, -1)
    new_even, new_odd = [], []
    C_acc = S_acc = None
    for l in range(depth):
        j = l // 2
        if l % 2 == 0:
            e = w["even"][j]
            if seq:
                prev = None
            else:
                conv0, shift0 = st[3][j], st[5][j]
                prev = (conv0[:, 0][None], conv0[:, 1][None], conv0[:, 2][None], shift0[:, 0][None])
            (q, k, v, so, g8, g8t, r, dec, kt, vb, al, be, gg, ctail, stail) = _even_pre(
                x3, mods[l], row(w["norm_mix_g"][l]), e["wcat"], e["conv_w"], row(e["conv_b"]), e["gate_b"],
                row(e["mu"]), row(e["w0"]), row(e["a0"]), e["wla"], e["g2"], row(e["k_k"]), row(e["k_a"]), prev, tt)
            if seq:
                C0 = jnp.zeros((bsz, A_HEADS, A_DK, A_DK), F32)
                nm0 = jnp.zeros((bsz, SUBLANES, LANES), F32)
                ha, C1, n1p, m1p = _mlstm_chunk(q, k, v, g8, g8t, C0, nm0, nm0, min(256, t))
                n1 = n1p[:, :A_HEADS, :]
                m1 = m1p[:, :A_HEADS, 0]
                S0p = jnp.zeros((bsz, B_HEADS // 2, B_HEAD, LANES), F32)
                y, S1p = _rwkv_scan(al, dec, be, kt, r, vb, S0p, min(8, bsz), min(128, t))
                conv1 = ctail[:, SUBLANES - (A_CONV - 1):, :]
                shift1 = stail[:, SUBLANES - 1:, :]
            else:
                ha2, C_acc, n1, m1 = _mlstm_step(q[0], k[0], v[0], g8[0], st[1][j], st[2][j], st[0], j, C_acc, 8)
                C1 = None
                ha = ha2[None]
                tok = lambda a_: a_[0][:, None, :]
                y2, S1p = _rwkv_scan(tok(al), tok(dec), tok(be), tok(kt), tok(r), tok(vb), _pair_state(st[4][j]), 8, 1)
                y = y2.reshape(1, bsz, bw)
                conv1 = jnp.concatenate([st[3][j][:, 1:], ctail[0][:, None, :]], axis=1)
                shift1 = stail[0][:, None, :]
            x3 = _even_post(x3, mods[l], ha, so, y, r, kt, vb, gg, row(e["a_norm_g"]), row(e["ln_g"]),
                            row(e["ln_b"]), row(e["r_k"]), e["w_out"], tt)
            new_even.append((C1, n1, m1, conv1, _unpair_state(S1p), shift1))
        else:
            o_ = w["odd"][j]
            q, k, v, lf, sg = _odd_pre(x3, mods[l], row(w["norm_mix_g"][l]), o_["w_in"], w["lb_logits"], j, tt)
            if seq:
                S0 = jnp.zeros((bsz, C_HEADS, C_DK, C_DK), F32)
                o, S1 = _hgrn_chunk(q, k, v, lf, S0, min(256, t), min(64, t), min(16, t), 4)
            else:
                o2, S_acc = _hgrn_step(q[0], k[0], v[0], lf[0], st[6], j, S_acc, 8)
                S1 = None
                o = o2[None]
            x3 = _odd_post(x3, mods[l], o, sg, row(jnp.tile(o_["norm_g"], C_HEADS)), o_["w_out"], tt)
            new_odd.append(S1)
        pw = w["peer"]
        x3 = _peer(x3, mods[l], row(w["norm_ffn_g"][l]), pw["wqt"], pw["keys"], pw["u"], pw["vt"], l, tm, 1024)
    y = _final_norm(x3, row(w["norm_final_g"]), tt).reshape(bsz, t, d)
    ev = [None if new_even[0][i] is None else jnp.stack([s[i] for s in new_even]) for i in range(6)]
    if seq:
        return (y, ev[0], ev[1], ev[2], ev[3], ev[4], ev[5], jnp.stack(new_odd))
    return (y, C_acc, ev[1], ev[2], ev[3], ev[4], ev[5], S_acc)


def _prepare_weights(norm_mix_g, norm_ffn_g, norm_final_g, even_w_in, even_w_out, mlstm_conv_w, mlstm_conv_b,
                     mlstm_gate_b, mlstm_norm_g, rwkv_mu, rwkv_w0, rwkv_w2, rwkv_a0, rwkv_a2, rwkv_g2, rwkv_k_k,
                     rwkv_k_a, rwkv_r_k, rwkv_ln_g, rwkv_ln_b, odd_w_in, odd_w_out, hgrn_lb_logits, hgrn_norm_g,
                     peer_w_q, peer_keys, peer_u, peer_v):
    d = even_w_in.shape[1]
    aw = d // 2
    bw = d // 2
    sw = 3 * bw + B_LORA_W + B_LORA_A + B_LORA_G
    even = []
    for j in range(even_w_in.shape[0]):
        wi = even_w_in[j]
        gates_w = wi[:, 4 * aw:4 * aw + 2 * A_HEADS]
        wcat = jnp.concatenate([wi[:, 0:4 * aw], wi[:, 4 * aw + 2 * A_HEADS:],
                                jnp.pad(gates_w, ((0, 0), (0, LANES - 2 * A_HEADS)))], axis=1).astype(BF16)
        assert wcat.shape[1] == 4 * aw + sw + LANES
        wla = jnp.zeros((B_LORA_W + B_LORA_A, 2 * bw), F32)
        wla = wla.at[:B_LORA_W, :bw].set(rwkv_w2[j]).at[B_LORA_W:, bw:].set(rwkv_a2[j]).astype(BF16)
        even.append(dict(
            wcat=wcat, conv_w=mlstm_conv_w[j], conv_b=mlstm_conv_b[j],
            gate_b=jnp.pad(mlstm_gate_b[j], (0, LANES - 2 * A_HEADS)).reshape(1, LANES),
            mu=rwkv_mu[j], w0=rwkv_w0[j], a0=rwkv_a0[j], wla=wla, g2=rwkv_g2[j].astype(BF16),
            k_k=rwkv_k_k[j], k_a=rwkv_k_a[j], r_k=rwkv_r_k[j], ln_g=rwkv_ln_g[j], ln_b=rwkv_ln_b[j],
            a_norm_g=mlstm_norm_g[j], w_out=even_w_out[j].astype(BF16)))
    odd = [dict(w_in=odd_w_in[j].astype(BF16), w_out=odd_w_out[j].astype(BF16), norm_g=hgrn_norm_g[j])
           for j in range(odd_w_in.shape[0])]
    peer = dict(wqt=jnp.swapaxes(peer_w_q, 1, 2).astype(BF16), keys=peer_keys.astype(BF16), u=peer_u.astype(BF16),
                vt=jnp.swapaxes(peer_v, 1, 2).astype(BF16))
    return dict(norm_mix_g=norm_mix_g, norm_ffn_g=norm_ffn_g, norm_final_g=norm_final_g, even=even, odd=odd,
                peer=peer, lb_logits=hgrn_lb_logits)


def kernel(x_prompt, x_sample, c_prompt, c_sample, state_mlstm_C, state_mlstm_n, state_mlstm_m, state_mlstm_conv, state_rwkv_S, state_rwkv_shift, state_hgrn_S, norm_mix_g, norm_ffn_g, norm_final_g, ada_w, ada_b, even_w_in, even_w_out, mlstm_conv_w, mlstm_conv_b, mlstm_gate_b, mlstm_norm_g, rwkv_mu, rwkv_w0, rwkv_w2, rwkv_a0, rwkv_a2, rwkv_g2, rwkv_k_k, rwkv_k_a, rwkv_r_k, rwkv_ln_g, rwkv_ln_b, odd_w_in, odd_w_out, hgrn_lb_logits, hgrn_norm_g, peer_w_q, peer_keys, peer_u, peer_v):
    w = _prepare_weights(norm_mix_g, norm_ffn_g, norm_final_g, even_w_in, even_w_out, mlstm_conv_w, mlstm_conv_b,
                         mlstm_gate_b, mlstm_norm_g, rwkv_mu, rwkv_w0, rwkv_w2, rwkv_a0, rwkv_a2, rwkv_g2, rwkv_k_k,
                         rwkv_k_a, rwkv_r_k, rwkv_ln_g, rwkv_ln_b, odd_w_in, odd_w_out, hgrn_lb_logits, hgrn_norm_g,
                         peer_w_q, peer_keys, peer_u, peer_v)
    bp = x_prompt.shape[0]
    mod = _ada(jnp.concatenate([c_prompt, c_sample], axis=0), ada_w, ada_b)
    out_p = _trunk(x_prompt, mod[:, :bp], None, w, True)
    st = (state_mlstm_C, state_mlstm_n, state_mlstm_m, state_mlstm_conv, state_rwkv_S, state_rwkv_shift,
          state_hgrn_S)
    out_s = _trunk(x_sample, mod[:, bp:], st, w, False)
    return (out_p[0], out_s[0]) + tuple(out_p[1:]) + tuple(out_s[1:])
```

```python
import functools
import math

import jax
import jax.numpy as jnp
import numpy as np
from jax import lax
from jax.experimental import pallas as pl
from jax.experimental.pallas import tpu as pltpu

F32 = jnp.float32
BF16 = jnp.bfloat16

EPS = 1e-6
A_HEADS = 4
A_DK = 128
A_CONV = 4
B_HEADS = 8
B_HEAD = 64
B_LORA_W = 64
B_LORA_A = 64
B_LORA_G = 128
B_LN_EPS = 64e-5
C_HEADS = 8
C_DK = 128
P_HEADS = 8
P_NKEYS = 128
P_TOPK = 16

LANES = 128
SUBLANES = 8
MXU_WIDTH = 256
VMEM_LIMIT = 56 * 1024 * 1024
NEG_INF = float("-inf")


def _bdot(a, b):
    return jnp.dot(a.astype(BF16), b.astype(BF16), preferred_element_type=F32)


def _dot_nt(a, b):
    return lax.dot_general(a.astype(BF16), b.astype(BF16), (((1,), (1,)), ((), ())), preferred_element_type=F32)


def _dot_tn(a, b):
    return lax.dot_general(a.astype(BF16), b.astype(BF16), (((0,), (0,)), ((), ())), preferred_element_type=F32)


def _split(x, terms):
    parts = []
    rem = x
    for _ in range(terms):
        p = rem.astype(BF16)
        parts.append(p)
        rem = rem - p.astype(F32)
    return parts


def _dot_split(a, b_bf, terms):
    out = None
    for p in _split(a, terms):
        d = jnp.dot(p, b_bf, preferred_element_type=F32)
        out = d if out is None else out + d
    return out


def _dot_split_left(a_bf, b, terms):
    out = None
    for p in _split(b, terms):
        d = jnp.dot(a_bf, p, preferred_element_type=F32)
        out = d if out is None else out + d
    return out


def _sigmoid(x):
    return 1.0 / (1.0 + jnp.exp(-x))


def _silu(x):
    return x * _sigmoid(x)


def _softplus(x):
    return jnp.maximum(x, 0.0) + jnp.log(1.0 + jnp.exp(-jnp.abs(x)))


def _gelu_tanh(x):
    c0 = -2.0 * math.sqrt(2.0 / math.pi)
    return x / (1.0 + jnp.exp(x * (c0 + (c0 * 0.044715) * (x * x))))


def _rmsnorm(x, g):
    return x * lax.rsqrt(jnp.mean(x * x, axis=-1, keepdims=True) + EPS) * g


def _group_matrix(n, group, value):
    r = lax.broadcasted_iota(jnp.int32, (n, n), 0) // group
    c = lax.broadcasted_iota(jnp.int32, (n, n), 1) // group
    return jnp.where(r == c, value, 0.0).astype(BF16)


def _full_spec(arr):
    nd = arr.ndim
    return pl.BlockSpec(arr.shape, lambda *_: (0,) * nd)


def _params(n_grid):
    return pltpu.CompilerParams(dimension_semantics=("arbitrary",) * n_grid, vmem_limit_bytes=VMEM_LIMIT)


def _tok_spec(tt, width):
    return pl.BlockSpec((1, tt, width), lambda g, i: (g, i, 0))


def _mod_spec(mod3, tt):
    width = mod3.shape[2]
    if mod3.shape[1] == 1:
        return pl.BlockSpec((1, 1, width), lambda g, i: (g, 0, 0))
    return pl.BlockSpec((1, tt, width), lambda g, i: (g, i, 0))


def _ada_body(c_ref, w_ref, b_ref, o_ref):
    c = c_ref[...]
    o_ref[0] = _bdot(_silu(c), w_ref[0]) + b_ref[0]


def _ada(c_all, ada_w, ada_b):
    depth, d, n6 = ada_w.shape
    bc = c_all.shape[0]
    tn = n6 // 4
    return pl.pallas_call(
        _ada_body,
        grid=(depth, n6 // tn),
        in_specs=[pl.BlockSpec((bc, d), lambda l, j: (0, 0)),
                  pl.BlockSpec((1, d, tn), lambda l, j: (l, 0, j)),
                  pl.BlockSpec((1, 1, tn), lambda l, j: (l, 0, j))],
        out_specs=pl.BlockSpec((1, bc, tn), lambda l, j: (l, 0, j)),
        out_shape=jax.ShapeDtypeStruct((depth, bc, n6), F32),
        compiler_params=_params(2),
        name="ada_mod",
    )(c_all, ada_w, ada_b.reshape(depth, 1, n6))


def _even_pre_body(seq, tt, d, aw, bw, *refs):
    n_in = 15 if seq else 19
    ins, outs = refs[:n_in], refs[n_in:]
    (x_ref, mod_ref, ng_ref, w_ref, cw_ref, cb_ref, gb_ref, mu_ref, w0_ref, a0_ref, wla_ref, g2_ref,
     kk_ref, ka_ref, gs_ref) = ins[:15]
    (q_ref, k_ref, v_ref, so_ref, g8_ref, g8t_ref, r_ref, dec_ref, kt_ref, vb_ref, al_ref, be_ref, gg_ref,
     ctail_ref, stail_ref) = outs[:15]
    a2 = 2 * aw
    sw = 3 * bw + B_LORA_W + B_LORA_A + B_LORA_G

    x = x_ref[0]
    mod = mod_ref[0]
    h = _rmsnorm(x, ng_ref[...]) * (1.0 + mod[:, d:2 * d]) + mod[:, 0:d]
    p = jnp.dot(h.astype(BF16), w_ref[...], preferred_element_type=F32)
    u = p[:, 0:a2]
    v = p[:, a2:a2 + aw]
    o = p[:, a2 + aw:a2 + 2 * aw]
    pb = p[:, 4 * aw:4 * aw + sw]
    gt = p[:, 4 * aw + sw:4 * aw + sw + LANES]

    if seq:
        ubuf, pbuf = outs[15], outs[16]

        @pl.when(pl.program_id(1) == 0)
        def _():
            ubuf[0:SUBLANES, :] = jnp.zeros((SUBLANES, a2), F32)
            pbuf[0:SUBLANES, :] = jnp.zeros((SUBLANES, sw), F32)

        ubuf[SUBLANES:SUBLANES + tt, :] = u
        pbuf[SUBLANES:SUBLANES + tt, :] = pb
        u1 = ubuf[SUBLANES - 1:SUBLANES - 1 + tt, :]
        u2 = ubuf[SUBLANES - 2:SUBLANES - 2 + tt, :]
        u3 = ubuf[SUBLANES - 3:SUBLANES - 3 + tt, :]
        pprev = pbuf[SUBLANES - 1:SUBLANES - 1 + tt, :]
        ubuf[0:SUBLANES, :] = u[tt - SUBLANES:tt, :]
        pbuf[0:SUBLANES, :] = pb[tt - SUBLANES:tt, :]
        ctail_ref[0] = u[tt - SUBLANES:tt, :]
        stail_ref[0] = pb[tt - SUBLANES:tt, :]
    else:
        u3, u2, u1, pprev = ins[15][0], ins[16][0], ins[17][0], ins[18][0]
        ctail_ref[0] = u
        stail_ref[0] = pb

    cw = cw_ref[...]
    y = cb_ref[...] + u3 * cw[0:1, :] + u2 * cw[1:2, :] + u1 * cw[2:3, :] + u * cw[3:4, :]
    qk = _silu(y)
    q_ref[0] = qk[:, 0:aw] * (A_DK ** -0.5)
    k_ref[0] = qk[:, aw:a2]
    v_ref[0] = v
    so_ref[0] = _sigmoid(o)
    g = gt + gb_ref[...]
    lane = lax.broadcasted_iota(jnp.int32, g.shape, 1)
    gates = jnp.where(lane < A_HEADS, g, -_softplus(-g))
    g8_ref[0] = gates[:, 0:2 * A_HEADS]
    g8t_ref[0] = gates.T[0:2 * A_HEADS, :]

    xb = pb + mu_ref[...] * (pprev - pb)
    r = xb[:, 0:bw]
    kb = xb[:, bw:2 * bw]
    vb = xb[:, 2 * bw:3 * bw]
    la = xb[:, 3 * bw:3 * bw + LANES]
    gl = xb[:, 3 * bw + LANES:3 * bw + 2 * LANES]
    lane2 = lax.broadcasted_iota(jnp.int32, la.shape, 1)
    la_act = jnp.where(lane2 < B_LORA_W, jnp.tanh(la), la)
    lw = jnp.dot(la_act.astype(BF16), wla_ref[...], preferred_element_type=F32)
    wlog = -_softplus(-(w0_ref[...] + lw[:, 0:bw])) - 0.5
    decay = jnp.exp(-jnp.exp(wlog))
    a = _sigmoid(a0_ref[...] + lw[:, bw:2 * bw])
    gg = jnp.dot(_sigmoid(gl).astype(BF16), g2_ref[...], preferred_element_type=F32)
    kk = kb * kk_ref[...]
    ss = _dot_split(kk * kk, gs_ref[...], 2)
    kkn = kk * lax.rsqrt(jnp.maximum(ss, 1e-24))
    r_ref[0] = r
    dec_ref[0] = decay
    kt_ref[0] = kb * (1.0 + (a - 1.0) * ka_ref[...])
    vb_ref[0] = vb
    al_ref[0] = -kkn
    be_ref[0] = kkn * a
    gg_ref[0] = gg


def _even_pre(x3, mod3, ng, wcat, cw, cb, gb, mu, w0, a0, wla, g2, k_k, k_a, prev, tt):
    g_, ttot, d = x3.shape
    aw = d // 2
    bw = d // 2
    a2 = 2 * aw
    sw = 3 * bw + B_LORA_W + B_LORA_A + B_LORA_G
    seq = prev is None
    gsum = _group_matrix(bw, B_HEAD, 1.0)
    consts = [ng, wcat, cw, cb, gb, mu, w0, a0, wla, g2, k_k, k_a, gsum]
    ins = [x3, mod3] + consts
    in_specs = [_tok_spec(tt, d), _mod_spec(mod3, tt)] + [_full_spec(c) for c in consts]
    if not seq:
        ins += list(prev)
        in_specs += [_tok_spec(tt, a2)] * 3 + [_tok_spec(tt, sw)]
    nt = ttot // tt
    tail_rows = SUBLANES if seq else tt
    tail_tot = g_ * SUBLANES if seq else ttot

    def tail_spec(width):
        if seq:
            return pl.BlockSpec((1, SUBLANES, width), lambda g, i: (g, 0, 0))
        return _tok_spec(tt, width)

    def tail_shape(width):
        if seq:
            return jax.ShapeDtypeStruct((g_, SUBLANES, width), F32)
        return jax.ShapeDtypeStruct((g_, ttot, width), F32)

    tok = lambda w: jax.ShapeDtypeStruct((g_, ttot, w), F32)
    out_shape = [tok(aw), tok(aw), tok(aw), tok(aw), tok(2 * A_HEADS),
                 jax.ShapeDtypeStruct((g_, 2 * A_HEADS, ttot), F32),
                 tok(bw), tok(bw), tok(bw), tok(bw), tok(bw), tok(bw), tok(bw),
                 tail_shape(a2), tail_shape(sw)]
    out_specs = [_tok_spec(tt, aw)] * 4 + [_tok_spec(tt, 2 * A_HEADS),
                                           pl.BlockSpec((1, 2 * A_HEADS, tt), lambda g, i: (g, 0, i))]
    out_specs += [_tok_spec(tt, bw)] * 7 + [tail_spec(a2), tail_spec(sw)]
    scratch = [pltpu.VMEM((tt + SUBLANES, a2), F32), pltpu.VMEM((tt + SUBLANES, sw), F32)] if seq else []
    del tail_rows, tail_tot
    return pl.pallas_call(
        functools.partial(_even_pre_body, seq, tt, d, aw, bw),
        grid=(g_, nt),
        in_specs=in_specs,
        out_specs=out_specs,
        out_shape=out_shape,
        scratch_shapes=scratch,
        compiler_params=_params(2),
        name="even_pre_seq" if seq else "even_pre_step",
    )(*ins)


def _mlstm_chunk_body(L, q_ref, k_ref, v_ref, g8_ref, g8t_ref, c0_ref, n0_ref, m0_ref,
                      h_ref, c1_ref, n1_ref, m1_ref, C_s, n_s, m_s):
    c = pl.program_id(1)

    @pl.when(c == 0)
    def _():
        C_s[...] = c0_ref[0]
        n_s[...] = n0_ref[0]
        m_s[...] = m0_ref[0]

    row = lax.broadcasted_iota(jnp.int32, (L, L), 0)
    col = lax.broadcasted_iota(jnp.int32, (L, L), 1)
    causal = col <= row
    tril = jnp.where(causal, 1.0, 0.0).astype(BF16)
    triu = jnp.where(row <= col, 1.0, 0.0).astype(BF16)
    g8 = g8_ref[0]
    g8t = g8t_ref[0]
    bcols = _dot_split_left(tril, g8, 3)
    brows = _dot_split(g8t, triu, 3)
    for hd in range(A_HEADS):
        sl = slice(hd * A_DK, (hd + 1) * A_DK)
        q = q_ref[0, :, sl]
        k = k_ref[0, :, sl]
        v = v_ref[0, :, sl]
        i_row = g8t[hd:hd + 1, :]
        i_col = g8[:, hd:hd + 1]
        b_row = brows[A_HEADS + hd:A_HEADS + hd + 1, :]
        b_col = bcols[:, A_HEADS + hd:A_HEADS + hd + 1]
        m_prev = m_s[hd:hd + 1, 0:1]
        C = C_s[hd]
        n = n_s[hd:hd + 1, :]
        dmat = jnp.where(causal, b_col - b_row + i_row, NEG_INF)
        inter = b_col + m_prev
        m_t = jnp.maximum(inter, jnp.max(dmat, axis=1, keepdims=True))
        w_intra = jnp.exp(dmat - m_t)
        w_inter = jnp.exp(inter - m_t)
        s = _dot_nt(q, k) * w_intra
        num = _bdot(s, v) + w_inter * _bdot(q, C)
        den = jnp.sum(s, axis=1, keepdims=True) + w_inter * jnp.sum(q * n, axis=1, keepdims=True)
        h_ref[0, :, sl] = num / jnp.maximum(jnp.abs(den), jnp.exp(-m_t))
        m_new = m_t[L - 1:L, :]
        b_last = b_col[L - 1:L, :]
        kw = k * jnp.exp(b_last - b_col + i_col - m_new)
        carry = jnp.exp(b_last + m_prev - m_new)
        C_s[hd] = carry * C + _dot_tn(kw, v)
        n_s[hd:hd + 1, :] = carry * n + jnp.sum(kw, axis=0, keepdims=True)
        m_s[hd:hd + 1, :] = jnp.broadcast_to(m_new, (1, LANES))

    @pl.when(c == pl.num_programs(1) - 1)
    def _():
        c1_ref[0] = C_s[...]
        n1_ref[0] = n_s[...]
        m1_ref[0] = m_s[...]


def _mlstm_chunk(q, k, v, g8, g8t, C0, n0p, m0p, L):
    b, t, aw = q.shape
    st = lambda *s: pl.BlockSpec((1,) + s, lambda bb, c: (bb,) + (0,) * len(s))
    return pl.pallas_call(
        functools.partial(_mlstm_chunk_body, L),
        grid=(b, t // L),
        in_specs=[_tok_spec(L, aw)] * 3 + [_tok_spec(L, 2 * A_HEADS),
                                           pl.BlockSpec((1, 2 * A_HEADS, L), lambda bb, c: (bb, 0, c)),
                                           st(A_HEADS, A_DK, A_DK), st(SUBLANES, LANES), st(SUBLANES, LANES)],
        out_specs=[_tok_spec(L, aw), st(A_HEADS, A_DK, A_DK), st(SUBLANES, LANES), st(SUBLANES, LANES)],
        out_shape=[jax.ShapeDtypeStruct((b, t, aw), F32), jax.ShapeDtypeStruct(C0.shape, F32),
                   jax.ShapeDtypeStruct(n0p.shape, F32), jax.ShapeDtypeStruct(m0p.shape, F32)],
        scratch_shapes=[pltpu.VMEM((A_HEADS, A_DK, A_DK), F32), pltpu.VMEM((SUBLANES, LANES), F32),
                        pltpu.VMEM((SUBLANES, LANES), F32)],
        compiler_params=_params(2),
        name="mlstm_chunk",
    )(q, k, v, g8, g8t, C0, n0p, m0p)


def _row8(x):
    r = lax.broadcasted_iota(jnp.int32, (SUBLANES, x.shape[1]), 0)
    return jnp.where(r == 0, jnp.broadcast_to(x, (SUBLANES, x.shape[1])), 0.0)


def _mlstm_step_body(bb, q_ref, k_ref, v_ref, g8_ref, n0_ref, m0_ref, c0_ref, *rest):
    h_ref, c1_ref, n1_ref, m1_ref = rest[-4:]
    g8 = g8_ref[...]
    m0 = m0_ref[...]
    for hd in range(A_HEADS):
        sl = slice(hd * A_DK, (hd + 1) * A_DK)
        li = g8[:, hd:hd + 1]
        lf = g8[:, A_HEADS + hd:A_HEADS + hd + 1]
        mp = m0[:, hd:hd + 1]
        m1 = jnp.maximum(lf + mp, li)
        wi = jnp.exp(li - m1)
        wf = jnp.exp(lf + mp - m1)
        floor = jnp.exp(-m1)
        for b in range(bb):
            q = q_ref[b:b + 1, sl]
            k = k_ref[b:b + 1, sl]
            v = v_ref[b:b + 1, sl]
            wib = wi[b:b + 1, :]
            wfb = wf[b:b + 1, :]
            C1 = wfb * c0_ref[0, b, hd] + wib * _dot_tn(_row8(k), _row8(v))
            n1 = wfb * n0_ref[b, hd:hd + 1, :] + wib * k
            num = _bdot(_row8(q), C1)[0:1, :]
            den = jnp.sum(q * n1, axis=1, keepdims=True)
            h_ref[b:b + 1, sl] = num / jnp.maximum(jnp.abs(den), floor[b:b + 1, :])
            c1_ref[0, b, hd] = C1
            n1_ref[b, hd:hd + 1, :] = n1
        m1_ref[:, hd:hd + 1] = m1


def _rwkv_body(bb, tb, ty, al_ref, w_ref, be_ref, kt_ref, r_ref, v_ref, s0_ref, y_ref, s1_ref, S_s, t3_s, y_s):
    @pl.when(pl.program_id(1) == 0)
    def _():
        S_s[...] = s0_ref[...]

    npair = B_HEADS // 2
    ntile = bb * npair
    nrow = min(tb, SUBLANES)
    row = lax.broadcasted_iota(jnp.int32, (B_HEAD, LANES), 0)
    lane = lax.broadcasted_iota(jnp.int32, (B_HEAD, LANES), 1)
    eye_pair = row == (lane % B_HEAD)
    wred = _group_matrix(2 * LANES, B_HEAD, 1.0)
    kk = lax.broadcasted_iota(jnp.int32, (nrow * LANES, LANES), 0)
    ll = lax.broadcasted_iota(jnp.int32, (nrow * LANES, LANES), 1)
    sel_head = ((kk % LANES) // B_HEAD) == (ll // B_HEAD)
    sel_off = (ll % B_HEAD) - (kk // LANES)

    def group(t8, carry):
        base = t8 * nrow if isinstance(t8, int) else pl.multiple_of(t8 * nrow, nrow)
        rows = pl.ds(base, nrow)
        for i in range(nrow):
            ri = slice(i, i + 1)
            lhs = []
            for b in range(bb):
                for p in range(npair):
                    sl = pl.ds(p * LANES, LANES)
                    lhs.append(jnp.concatenate(
                        [(S_s[b, p] * al_ref[b, rows, sl][ri]).astype(BF16),
                         jnp.where(eye_pair, v_ref[b, rows, sl][ri], 0.0).astype(BF16)], axis=1))
            red = jnp.dot(jnp.concatenate(lhs, axis=0), wred, preferred_element_type=F32)
            for b in range(bb):
                for p in range(npair):
                    sl = pl.ds(p * LANES, LANES)
                    idx = b * npair + p
                    sa = red[idx * B_HEAD:(idx + 1) * B_HEAD, 0:LANES]
                    vcol = red[idx * B_HEAD:(idx + 1) * B_HEAD, LANES:2 * LANES]
                    S = (S_s[b, p] * w_ref[b, rows, sl][ri] + sa * be_ref[b, rows, sl][ri]
                         + vcol * kt_ref[b, rows, sl][ri])
                    S_s[b, p] = S
                    t3_s[idx * B_HEAD:(idx + 1) * B_HEAD, i * LANES:(i + 1) * LANES] = (
                        S * r_ref[b, rows, sl][ri]).astype(BF16)
        wsel = jnp.where(sel_head & (sel_off == base % ty), 1.0, 0.0).astype(BF16)
        yg = jnp.dot(t3_s[...], wsel, preferred_element_type=F32)

        @pl.when(base % ty == 0)
        def _():
            y_s[...] = yg

        @pl.when(base % ty != 0)
        def _():
            y_s[...] += yg

        @pl.when((base + nrow) % ty == 0)
        def _():
            start = (base // ty) * ty
            trow = pl.ds(start, ty) if isinstance(start, int) else pl.ds(pl.multiple_of(start, ty), ty)
            half = npair * B_HEAD
            for b in range(bb):
                yt = y_s[b * half:(b + 1) * half, :].T
                y_ref[b, trow, 0:half] = yt[0:ty, :]
                y_ref[b, trow, half:2 * half] = yt[B_HEAD:B_HEAD + ty, :]
        return carry

    if tb == nrow:
        group(0, 0)
    else:
        lax.fori_loop(0, tb // nrow, group, 0)

    @pl.when(pl.program_id(1) == pl.num_programs(1) - 1)
    def _():
        s1_ref[...] = S_s[...]


def _rwkv_scan(al, w, be, kt, r, v, S0p, bb, tb):
    b, t, bw = al.shape
    npair = B_HEADS // 2
    ty = min(B_HEAD, t)
    nb = b // bb
    tok = pl.BlockSpec((bb, tb, bw), lambda i, j: (i, j, 0))
    st = pl.BlockSpec((bb, npair, B_HEAD, LANES), lambda i, j: (i, 0, 0, 0))
    nrow = min(tb, SUBLANES)
    rows = bb * npair * B_HEAD
    return pl.pallas_call(
        functools.partial(_rwkv_body, bb, tb, ty),
        grid=(nb, t // tb),
        in_specs=[tok] * 6 + [st],
        out_specs=[tok, st],
        out_shape=[jax.ShapeDtypeStruct((b, t, bw), F32), jax.ShapeDtypeStruct(S0p.shape, F32)],
        scratch_shapes=[pltpu.VMEM((bb, npair, B_HEAD, LANES), F32), pltpu.VMEM((rows, nrow * LANES), BF16),
                        pltpu.VMEM((rows, LANES), F32)],
        compiler_params=_params(2),
        name="rwkv_scan",
    )(al, w, be, kt, r, v, S0p)


PAIRED_HEADS = tuple(q + (B_HEADS // 2) * hh for q in range(B_HEADS // 2) for hh in range(2))


def _paired_channels(offset=0):
    return np.concatenate([np.arange(h * B_HEAD, (h + 1) * B_HEAD) for h in PAIRED_HEADS]) + offset


def _paired_to_natural(x):
    pos = {h: i for i, h in enumerate(PAIRED_HEADS)}
    return jnp.concatenate([x[:, pos[h] * B_HEAD:(pos[h] + 1) * B_HEAD] for h in range(B_HEADS)], axis=1)


def _even_post_body(d, aw, bw, x_ref, mod_ref, ha_ref, so_ref, y_ref, r_ref, kt_ref, vb_ref, gg_ref,
                    ang_ref, lng_ref, lnb_ref, rk_ref, wo_ref, o_ref):
    ga = _group_matrix(aw, A_DK, 1.0 / A_DK)
    gb_mean = _group_matrix(bw, B_HEAD, 1.0 / B_HEAD)
    gb_sum = _group_matrix(bw, B_HEAD, 1.0)
    r, kt, vb, gg = (_paired_to_natural(ref[0]) for ref in (r_ref, kt_ref, vb_ref, gg_ref))
    ha = ha_ref[0]
    ha = ha * lax.rsqrt(_dot_split(ha * ha, ga, 2) + EPS) * ang_ref[...] * so_ref[0]
    y = y_ref[0]
    yc = y - _dot_split(y, gb_mean, 2)
    var = _dot_split(yc * yc, gb_mean, 2)
    yn = yc * lax.rsqrt(var + B_LN_EPS) * lng_ref[...] + lnb_ref[...]
    bonus = _dot_split(r * kt * rk_ref[...], gb_sum, 2)
    y2 = (yn + bonus * vb) * gg
    out = (jnp.dot(ha.astype(BF16), wo_ref[0:aw, :], preferred_element_type=F32)
           + jnp.dot(y2.astype(BF16), wo_ref[aw:aw + bw, :], preferred_element_type=F32))
    o_ref[0] = x_ref[0] + mod_ref[0][:, 2 * d:3 * d] * out


def _even_post(x3, mod3, ha, so, y, r, kt, vb, gg, ang, lng, lnb, rk, wo, tt):
    g_, ttot, d = x3.shape
    aw = d // 2
    bw = d // 2
    consts = [ang, lng, lnb, rk, wo]
    return pl.pallas_call(
        functools.partial(_even_post_body, d, aw, bw),
        grid=(g_, ttot // tt),
        in_specs=[_tok_spec(tt, d), _mod_spec(mod3, tt)] + [_tok_spec(tt, aw)] * 7 + [_full_spec(c) for c in consts],
        out_specs=_tok_spec(tt, d),
        out_shape=jax.ShapeDtypeStruct(x3.shape, F32),
        compiler_params=_params(2),
        name="even_post",
    )(x3, mod3, ha, so, y, r, kt, vb, gg, *consts)


def _odd_pre_body(j, d, x_ref, mod_ref, ng_ref, w_ref, lbl_ref, q_ref, k_ref, v_ref, lf_ref, sg_ref):
    x = x_ref[0]
    mod = mod_ref[0]
    h = _rmsnorm(x, ng_ref[...]) * (1.0 + mod[:, d:2 * d]) + mod[:, 0:d]
    p = jnp.dot(h.astype(BF16), w_ref[...], preferred_element_type=F32)
    lbl = lbl_ref[...]
    n_odd = lbl.shape[0]
    mx = lbl[0:1, :]
    for i in range(1, n_odd):
        mx = jnp.maximum(mx, lbl[i:i + 1, :])
    ex = [jnp.exp(lbl[i:i + 1, :] - mx) for i in range(n_odd)]
    tot = ex[0]
    for i in range(1, n_odd):
        tot = tot + ex[i]
    lb = jnp.zeros_like(mx)
    for i in range(1, j + 1):
        lb = lb + ex[i] / tot
    f = lb + (1.0 - lb) * _sigmoid(p[:, d:2 * d])
    q_ref[0] = _silu(p[:, 0:d])
    k_ref[0] = 1.0 - f
    v_ref[0] = p[:, 2 * d:3 * d]
    lf_ref[0] = jnp.log(f)
    sg_ref[0] = _silu(p[:, 3 * d:4 * d])


def _odd_pre(x3, mod3, ng, w_in, lbl, j, tt):
    g_, ttot, d = x3.shape
    consts = [ng, w_in, lbl]
    return pl.pallas_call(
        functools.partial(_odd_pre_body, j, d),
        grid=(g_, ttot // tt),
        in_specs=[_tok_spec(tt, d), _mod_spec(mod3, tt)] + [_full_spec(c) for c in consts],
        out_specs=[_tok_spec(tt, d)] * 5,
        out_shape=[jax.ShapeDtypeStruct(x3.shape, F32)] * 5,
        compiler_params=_params(2),
        name="odd_pre",
    )(x3, mod3, *consts)


def _col_bcast(row, terms):
    parts = _split(row, terms)
    r = lax.broadcasted_iota(jnp.int32, (SUBLANES, row.shape[1]), 0)
    lhs = jnp.zeros((SUBLANES, row.shape[1]), F32)
    for i, p in enumerate(parts):
        lhs = jnp.where(r == i, jnp.broadcast_to(p.astype(F32), lhs.shape), lhs)
    ones = jnp.where(lax.broadcasted_iota(jnp.int32, (SUBLANES, LANES), 0) < terms, 1.0, 0.0).astype(BF16)
    return lax.dot_general(lhs.astype(BF16), ones, (((0,), (0,)), ((), ())), preferred_element_type=F32)


def _hgrn_chunk_body(L, sub, nchunk, hg, q_ref, k_ref, v_ref, g_ref, s0_ref, o_ref, s1_ref, S_s):
    @pl.when(pl.program_id(2) == 0)
    def _():
        S_s[...] = s0_ref[0]

    nj = L // sub
    rowi = lax.broadcasted_iota(jnp.int32, (L, C_DK), 0)
    rr = lax.broadcasted_iota(jnp.int32, (L, L), 0)
    cc = lax.broadcasted_iota(jnp.int32, (L, L), 1)
    tril = jnp.where(cc <= rr, 1.0, 0.0).astype(BF16)
    ones = jnp.ones((C_DK, LANES), BF16)

    def chunk(ci, carry):
        base = pl.multiple_of(ci * L, L)
        rows = pl.ds(base, L)
        b_all = _dot_split_left(tril, g_ref[0, rows, :], 3)
        heads = []
        for hd in range(hg):
            sl = slice(hd * C_DK, (hd + 1) * C_DK)
            heads.append((q_ref[0, rows, sl], k_ref[0, rows, sl], v_ref[0, rows, sl], b_all[:, sl], S_s[hd]))
        outs = [_bdot(q * jnp.exp(b), S) for (q, k, v, b, S) in heads]
        for dlt in range(sub):
            prs, vss = [], []
            for (q, k, v, b, S) in heads:
                if dlt == 0:
                    prs.append((q * k).astype(BF16))
                    vss.append(v)
                else:
                    ok = (rowi % sub) >= dlt
                    e = jnp.exp(jnp.where(ok, b - pltpu.roll(b, dlt, 0), NEG_INF))
                    prs.append((q * pltpu.roll(k, dlt, 0) * e).astype(BF16))
                    vss.append(pltpu.roll(v, dlt, 0))
            rs = jnp.dot(jnp.concatenate(prs, axis=0), ones, preferred_element_type=F32)
            outs = [o + rs[i * L:(i + 1) * L, :] * vs for i, (o, vs) in enumerate(zip(outs, vss))]
        for hd, ((q, k, v, b, S), o) in enumerate(zip(heads, outs)):
            if nj > 1:
                amat = None
                for jj in range(nj - 1):
                    end = (jj + 1) * sub
                    bref = b[end - 1:end, :]
                    qj = q * jnp.exp(jnp.where(rowi >= end, b - bref, NEG_INF))
                    kj = k * jnp.exp(jnp.where((rowi >= end - sub) & (rowi < end), bref - b, NEG_INF))
                    a = _dot_nt(qj, kj)
                    amat = a if amat is None else amat + a
                o = o + _bdot(amat, v)
            o_ref[0, rows, hd * C_DK:(hd + 1) * C_DK] = o
            b_last = b[L - 1:L, :]
            S_s[hd] = _col_bcast(jnp.exp(b_last), 3) * S + _dot_tn(k * jnp.exp(b_last - b), v)
        return carry

    lax.fori_loop(0, nchunk, chunk, 0)

    @pl.when(pl.program_id(2) == pl.num_programs(2) - 1)
    def _():
        s1_ref[0] = S_s[...]


def _hgrn_chunk(q, k, v, lf, S0, tblk, L, sub, hg):
    b, t, d = q.shape
    tok = pl.BlockSpec((1, tblk, hg * C_DK), lambda bb, h, c: (bb, c, h))
    st = pl.BlockSpec((1, hg, C_DK, C_DK), lambda bb, h, c: (bb, h, 0, 0))
    return pl.pallas_call(
        functools.partial(_hgrn_chunk_body, L, sub, tblk // L, hg),
        grid=(b, C_HEADS // hg, t // tblk),
        in_specs=[tok] * 4 + [st],
        out_specs=[tok, st],
        out_shape=[jax.ShapeDtypeStruct((b, t, d), F32), jax.ShapeDtypeStruct(S0.shape, F32)],
        scratch_shapes=[pltpu.VMEM((hg, C_DK, C_DK), F32)],
        compiler_params=_params(3),
        name="hgrn_chunk",
    )(q, k, v, lf, S0)


def _hgrn_step_body(bb, q_ref, k_ref, v_ref, g_ref, s0_ref, *rest):
    o_ref, s1_ref = rest[-2], rest[-1]
    for b in range(bb):
        for hd in range(C_HEADS):
            sl = slice(hd * C_DK, (hd + 1) * C_DK)
            q = q_ref[b:b + 1, sl]
            k = k_ref[b:b + 1, sl]
            v = v_ref[b:b + 1, sl]
            f = jnp.exp(g_ref[b:b + 1, sl])
            S1 = _col_bcast(f, 3) * s0_ref[0, b, hd] + _dot_tn(_row8(k), _row8(v))
            o_ref[b:b + 1, sl] = _bdot(_row8(q), S1)[0:1, :]
            s1_ref[0, b, hd] = S1


def _layer_state_call(body, name, rows_in, row_out_shape, row_spec, state_all, layer, state_acc, bb, extra_out=()):
    b = state_all.shape[1]
    blk = (1, bb) + state_all.shape[2:]
    zeros = (0,) * (state_all.ndim - 2)
    st = pl.BlockSpec(blk, lambda i: (layer, i) + zeros)
    ins = list(rows_in) + [state_all]
    in_specs = [row_spec(r) for r in rows_in] + [st]
    aliases = {}
    if state_acc is not None:
        ins.append(state_acc)
        in_specs.append(pl.BlockSpec(memory_space=pl.ANY))
        aliases = {len(ins) - 1: 1}
    return pl.pallas_call(
        body,
        grid=(b // bb,),
        in_specs=in_specs,
        out_specs=[row_spec(row_out_shape), st] + [row_spec(e) for e in extra_out],
        out_shape=[jax.ShapeDtypeStruct(row_out_shape.shape, F32), jax.ShapeDtypeStruct(state_all.shape, F32)]
        + [jax.ShapeDtypeStruct(e.shape, F32) for e in extra_out],
        input_output_aliases=aliases,
        compiler_params=_params(1),
        name=name,
    )(*ins)


def _row_spec_for(bb):
    def spec(arr):
        blk = (bb,) + arr.shape[1:]
        zeros = (0,) * (arr.ndim - 1)
        return pl.BlockSpec(blk, lambda i: (i,) + zeros)
    return spec


def _mlstm_step(q, k, v, g8, n0, m0, C_all, layer, C_acc, bb):
    return _layer_state_call(functools.partial(_mlstm_step_body, bb), "mlstm_step", [q, k, v, g8, n0, m0], q,
                             _row_spec_for(bb), C_all, layer, C_acc, bb, extra_out=(n0, m0))


def _hgrn_step(q, k, v, lf, S_all, layer, S_acc, bb):
    return _layer_state_call(functools.partial(_hgrn_step_body, bb), "hgrn_step", [q, k, v, lf], q,
                             _row_spec_for(bb), S_all, layer, S_acc, bb)


def _odd_post_body(d, x_ref, mod_ref, o_ref, sg_ref, ng_ref, wo_ref, out_ref):
    gm = _group_matrix(d, C_DK, 1.0 / C_DK)
    o = o_ref[0]
    o = o * lax.rsqrt(_dot_split(o * o, gm, 2) + EPS) * ng_ref[...] * sg_ref[0]
    out = jnp.dot(o.astype(BF16), wo_ref[...], preferred_element_type=F32)
    out_ref[0] = x_ref[0] + mod_ref[0][:, 2 * d:3 * d] * out


def _odd_post(x3, mod3, o, sg, ng_tiled, wo, tt):
    g_, ttot, d = x3.shape
    consts = [ng_tiled, wo]
    return pl.pallas_call(
        functools.partial(_odd_post_body, d),
        grid=(g_, ttot // tt),
        in_specs=[_tok_spec(tt, d), _mod_spec(mod3, tt)] + [_tok_spec(tt, d)] * 2 + [_full_spec(c) for c in consts],
        out_specs=_tok_spec(tt, d),
        out_shape=jax.ShapeDtypeStruct(x3.shape, F32),
        compiler_params=_params(2),
        name="odd_post",
    )(x3, mod3, o, sg, *consts)


P_NCAND = P_TOPK + 1
P_CAND_ROWS = 3 * SUBLANES


def _top_sorted(s):
    ridx = lax.broadcasted_iota(jnp.int32, (P_CAND_ROWS, s.shape[1]), 0)
    acc = jnp.full((P_CAND_ROWS, s.shape[1]), NEG_INF, F32)
    cur = s
    for i in range(P_NCAND):
        m = jnp.max(cur, axis=0, keepdims=True)
        acc = jnp.where(ridx == i, m, acc)
        cur = jnp.where(cur == m, NEG_INF, cur)
    return acc


def _peer_body(tm, te, ne, d, x_ref, mod_ref, ng_ref, wqt_ref, keys_ref, u_ref, vt_ref, vl_ref, o_ref,
               ht_s, s1_s, e1_s, thr_s, e2_s, gate_s, w_s, acc_s):
    j = pl.program_id(2)
    dk2 = keys_ref.shape[4]
    nsub = te // P_NKEYS

    @pl.when(j == 0)
    def _():
        x = x_ref[0]
        mod = mod_ref[0]
        h = _rmsnorm(x, ng_ref[...]) * (1.0 + mod[:, 4 * d:5 * d]) + mod[:, 3 * d:4 * d]
        ht = h.T.astype(BF16)
        ht_s[...] = ht
        qt = jnp.dot(wqt_ref[0], ht, preferred_element_type=F32)
        r8 = lax.broadcasted_iota(jnp.int32, (SUBLANES, LANES), 0)
        for hd in range(P_HEADS):
            s1_all = _bdot(keys_ref[0, hd, 0], qt[(2 * hd) * dk2:(2 * hd + 1) * dk2, :])
            s2_all = _bdot(keys_ref[0, hd, 1], qt[(2 * hd + 1) * dk2:(2 * hd + 2) * dk2, :])
            for tc in range(tm // LANES):
                ls = slice(tc * LANES, (tc + 1) * LANES)
                s1 = s1_all[:, ls]
                s2 = s2_all[:, ls]
                a = _top_sorted(s1)
                b = _top_sorted(s2)
                blocks = [a[0:1, :] + b]
                for i in range(2, SUBLANES + 1):
                    blocks.append(jnp.where(r8 < P_NCAND // i, a[i - 1:i, :] + b[0:SUBLANES, :], NEG_INF))
                blocks.append(a[SUBLANES:P_CAND_ROWS, :] + b[0:1, :])
                cand = jnp.concatenate(blocks, axis=0)
                cur = cand
                best = None
                for _ in range(P_NCAND):
                    prev, best = best, jnp.max(cur, axis=0, keepdims=True)
                    cur = jnp.where(cur == best, NEG_INF, cur)
                tau = 0.5 * (prev + best)
                top = a[0:1, :] + b[0:1, :]
                z = jnp.sum(jnp.where(cand >= tau, jnp.exp(cand - top), 0.0), axis=0, keepdims=True)
                s1_s[hd, :, ls] = s1
                e1_s[hd, :, ls] = jnp.exp(s1 - a[0:1, :])
                thr_s[hd, :, ls] = tau - s2
                e2_s[hd, :, ls] = jnp.exp(s2 - b[0:1, :]) / z
        acc_s[...] = jnp.zeros_like(acc_s)
        w_s[1] = jnp.zeros((te, tm), BF16)

    assert nsub == SUBLANES
    grp = pl.ds(pl.multiple_of(j * nsub, nsub), nsub)

    def gates(tc):
        ls = slice(tc * LANES, (tc + 1) * LANES)
        s1g = [s1_s[hd, grp, ls] for hd in range(P_HEADS)]
        e1g = [e1_s[hd, grp, ls] for hd in range(P_HEADS)]
        for ii in range(nsub):
            rows = slice(ii * P_NKEYS, (ii + 1) * P_NKEYS)
            gate = jnp.zeros((P_NKEYS, LANES), F32)
            for hd in range(P_HEADS):
                keep = s1g[hd][ii:ii + 1, :] >= thr_s[hd, :, ls]
                gate = gate + jnp.where(keep, e2_s[hd, :, ls], 0.0) * e1g[hd][ii:ii + 1, :]
            gate_s[rows, ls] = gate

    piece = min(tm, MXU_WIDTH)
    tc_per_piece = piece // LANES
    for pc in range(tm // piece):
        cs = slice(pc * piece, (pc + 1) * piece)
        acc_s[:, cs] += jnp.dot(vt_ref[0], w_s[(j + 1) % 2, :, cs], preferred_element_type=F32)
        for tc in range(pc * tc_per_piece, pc * tc_per_piece + (tc_per_piece + 1) // 2):
            gates(tc)
        act = jnp.dot(u_ref[0], ht_s[:, cs], preferred_element_type=F32)
        for tc in range(pc * tc_per_piece + (tc_per_piece + 1) // 2, (pc + 1) * tc_per_piece):
            gates(tc)
        w_s[j % 2, :, cs] = (gate_s[:, cs] * _gelu_tanh(act)).astype(BF16)

    @pl.when(j == ne - 1)
    def _():
        acc = acc_s[...] + jnp.dot(vl_ref[0], w_s[(ne - 1) % 2], preferred_element_type=F32)
        o_ref[0] = x_ref[0] + mod_ref[0][:, 5 * d:6 * d] * acc.T


def _peer(x3, mod3, ng, wqt_all, keys_all, u_all, vt_all, layer, tm, te):
    g_, ttot, d = x3.shape
    ne = u_all.shape[1] // te
    nk = P_NKEYS
    if mod3.shape[1] == 1:
        mod_spec = pl.BlockSpec((1, 1, mod3.shape[2]), lambda g, i, j: (g, 0, 0))
    else:
        mod_spec = pl.BlockSpec((1, tm, mod3.shape[2]), lambda g, i, j: (g, i, 0))
    return pl.pallas_call(
        functools.partial(_peer_body, tm, te, ne, d),
        grid=(g_, ttot // tm, ne),
        in_specs=[pl.BlockSpec((1, tm, d), lambda g, i, j: (g, i, 0)), mod_spec,
                  pl.BlockSpec(ng.shape, lambda g, i, j: (0, 0)),
                  pl.BlockSpec((1,) + wqt_all.shape[1:], lambda g, i, j: (layer, 0, 0)),
                  pl.BlockSpec((1,) + keys_all.shape[1:], lambda g, i, j: (layer, 0, 0, 0, 0)),
                  pl.BlockSpec((1, te, d), lambda g, i, j: (layer, j, 0)),
                  pl.BlockSpec((1, d, te), lambda g, i, j: (layer, 0, jnp.maximum(j - 1, 0))),
                  pl.BlockSpec((1, d, te), lambda g, i, j: (layer, 0, ne - 1))],
        out_specs=pl.BlockSpec((1, tm, d), lambda g, i, j: (g, i, 0)),
        out_shape=jax.ShapeDtypeStruct(x3.shape, F32),
        scratch_shapes=[pltpu.VMEM((d, tm), BF16),
                        pltpu.VMEM((P_HEADS, nk, tm), F32), pltpu.VMEM((P_HEADS, nk, tm), F32),
                        pltpu.VMEM((P_HEADS, nk, tm), F32), pltpu.VMEM((P_HEADS, nk, tm), F32),
                        pltpu.VMEM((te, tm), F32), pltpu.VMEM((2, te, tm), BF16), pltpu.VMEM((d, tm), F32)],
        compiler_params=_params(3),
        name="peer",
    )(x3, mod3, ng, wqt_all, keys_all, u_all, vt_all, vt_all)


def _final_norm_body(x_ref, g_ref, o_ref):
    o_ref[0] = _rmsnorm(x_ref[0], g_ref[...])


def _final_norm(x3, g, tt):
    g_, ttot, d = x3.shape
    return pl.pallas_call(
        _final_norm_body,
        grid=(g_, ttot // tt),
        in_specs=[_tok_spec(tt, d), _full_spec(g)],
        out_specs=_tok_spec(tt, d),
        out_shape=jax.ShapeDtypeStruct(x3.shape, F32),
        compiler_params=_params(2),
        name="final_norm",
    )(x3, g)


def _pair_state(S):
    b = S.shape[0]
    return S.reshape(b, 2, B_HEADS // 2, B_HEAD, B_HEAD).transpose(0, 2, 3, 1, 4).reshape(b, B_HEADS // 2, B_HEAD, LANES)


def _unpair_state(Sp):
    b = Sp.shape[0]
    return Sp.reshape(b, B_HEADS // 2, B_HEAD, 2, B_HEAD).transpose(0, 3, 1, 2, 4).reshape(b, B_HEADS, B_HEAD, B_HEAD)


def _shift_perm(bw):
    return np.concatenate([_paired_channels(0), _paired_channels(bw), _paired_channels(2 * bw),
                           np.arange(3 * bw, 3 * bw + B_LORA_W + B_LORA_A + B_LORA_G)])


def _trunk(x, mod_all, st, w, seq):
    bsz, t, d = x.shape
    depth = mod_all.shape[0]
    aw = d // 2
    bw = d // 2
    if seq:
        x3 = x
        tt = min(256, t)
        tm = min(512, t)
        mods = [mod_all[l][:, None, :] for l in range(depth)]
    else:
        x3 = x.reshape(1, bsz * t, d)
        tt = bsz * t
        tm = bsz * t
        mods = [mod_all[l][None] for l in range(depth)]
    row = lambda v_: v_.reshape(1, -1)
    new_even, new_odd = [], []
    sp = _shift_perm(bw)
    sp_inv = np.argsort(sp)
    C_acc = S_acc = None
    for l in range(depth):
        j = l // 2
        if l % 2 == 0:
            e = w["even"][j]
            if seq:
                prev = None
            else:
                conv0, shift0 = st[3][j], st[5][j]
                prev = (conv0[:, 0][None], conv0[:, 1][None], conv0[:, 2][None], shift0[:, 0][None][..., sp])
            (q, k, v, so, g8, g8t, r, dec, kt, vb, al, be, gg, ctail, stail) = _even_pre(
                x3, mods[l], row(w["norm_mix_g"][l]), e["wcat"], e["conv_w"], row(e["conv_b"]), e["gate_b"],
                row(e["mu"]), row(e["w0"]), row(e["a0"]), e["wla"], e["g2"], row(e["k_k"]), row(e["k_a"]), prev, tt)
            if seq:
                C0 = jnp.zeros((bsz, A_HEADS, A_DK, A_DK), F32)
                nm0 = jnp.zeros((bsz, SUBLANES, LANES), F32)
                ha, C1, n1p, m1p = _mlstm_chunk(q, k, v, g8, g8t, C0, nm0, nm0, min(256, t))
                n1 = n1p[:, :A_HEADS, :]
                m1 = m1p[:, :A_HEADS, 0]
                S0p = jnp.zeros((bsz, B_HEADS // 2, B_HEAD, LANES), F32)
                y, S1p = _rwkv_scan(al, dec, be, kt, r, vb, S0p, min(8, bsz), min(128, t))
                conv1 = ctail[:, SUBLANES - (A_CONV - 1):, :]
                shift1 = stail[:, SUBLANES - 1:, :][..., sp_inv]
            else:
                ha2, C_acc, n1, m1 = _mlstm_step(q[0], k[0], v[0], g8[0], st[1][j], st[2][j], st[0], j, C_acc, 8)
                C1 = None
                ha = ha2[None]
                tok = lambda a_: a_[0][:, None, :]
                y2, S1p = _rwkv_scan(tok(al), tok(dec), tok(be), tok(kt), tok(r), tok(vb), _pair_state(st[4][j]), 8, 1)
                y = y2.reshape(1, bsz, bw)
                conv1 = jnp.concatenate([st[3][j][:, 1:], ctail[0][:, None, :]], axis=1)
                shift1 = stail[0][:, None, :][..., sp_inv]
            x3 = _even_post(x3, mods[l], ha, so, y, r, kt, vb, gg, row(e["a_norm_g"]), row(e["ln_g"]),
                            row(e["ln_b"]), row(e["r_k"]), e["w_out"], tt)
            new_even.append((C1, n1, m1, conv1, _unpair_state(S1p), shift1))
        else:
            o_ = w["odd"][j]
            q, k, v, lf, sg = _odd_pre(x3, mods[l], row(w["norm_mix_g"][l]), o_["w_in"], w["lb_logits"], j, tt)
            if seq:
                S0 = jnp.zeros((bsz, C_HEADS, C_DK, C_DK), F32)
                o, S1 = _hgrn_chunk(q, k, v, lf, S0, min(256, t), min(64, t), min(16, t), 4)
            else:
                o2, S_acc = _hgrn_step(q[0], k[0], v[0], lf[0], st[6], j, S_acc, 8)
                S1 = None
                o = o2[None]
            x3 = _odd_post(x3, mods[l], o, sg, row(jnp.tile(o_["norm_g"], C_HEADS)), o_["w_out"], tt)
            new_odd.append(S1)
        pw = w["peer"]
        x3 = _peer(x3, mods[l], row(w["norm_ffn_g"][l]), pw["wqt"], pw["keys"], pw["u"], pw["vt"], l, tm, 1024)
    y = _final_norm(x3, row(w["norm_final_g"]), tt).reshape(bsz, t, d)
    ev = [None if new_even[0][i] is None else jnp.stack([s[i] for s in new_even]) for i in range(6)]
    if seq:
        return (y, ev[0], ev[1], ev[2], ev[3], ev[4], ev[5], jnp.stack(new_odd))
    return (y, C_acc, ev[1], ev[2], ev[3], ev[4], ev[5], S_acc)


def _prepare_weights(norm_mix_g, norm_ffn_g, norm_final_g, even_w_in, even_w_out, mlstm_conv_w, mlstm_conv_b,
                     mlstm_gate_b, mlstm_norm_g, rwkv_mu, rwkv_w0, rwkv_w2, rwkv_a0, rwkv_a2, rwkv_g2, rwkv_k_k,
                     rwkv_k_a, rwkv_r_k, rwkv_ln_g, rwkv_ln_b, odd_w_in, odd_w_out, hgrn_lb_logits, hgrn_norm_g,
                     peer_w_q, peer_keys, peer_u, peer_v):
    d = even_w_in.shape[1]
    aw = d // 2
    bw = d // 2
    sw = 3 * bw + B_LORA_W + B_LORA_A + B_LORA_G
    even = []
    for j in range(even_w_in.shape[0]):
        wi = even_w_in[j]
        gates_w = wi[:, 4 * aw:4 * aw + 2 * A_HEADS]
        pc = _paired_channels()
        sp = _shift_perm(bw)
        wcat = jnp.concatenate([wi[:, 0:4 * aw], wi[:, 4 * aw + 2 * A_HEADS:][:, sp],
                                jnp.pad(gates_w, ((0, 0), (0, LANES - 2 * A_HEADS)))], axis=1).astype(BF16)
        assert wcat.shape[1] == 4 * aw + sw + LANES
        wla = jnp.zeros((B_LORA_W + B_LORA_A, 2 * bw), F32)
        wla = wla.at[:B_LORA_W, :bw].set(rwkv_w2[j][:, pc]).at[B_LORA_W:, bw:].set(rwkv_a2[j][:, pc]).astype(BF16)
        even.append(dict(
            wcat=wcat, conv_w=mlstm_conv_w[j], conv_b=mlstm_conv_b[j],
            gate_b=jnp.pad(mlstm_gate_b[j], (0, LANES - 2 * A_HEADS)).reshape(1, LANES),
            mu=rwkv_mu[j][sp], w0=rwkv_w0[j][pc], a0=rwkv_a0[j][pc], wla=wla, g2=rwkv_g2[j][:, pc].astype(BF16),
            k_k=rwkv_k_k[j][pc], k_a=rwkv_k_a[j][pc], r_k=rwkv_r_k[j], ln_g=rwkv_ln_g[j], ln_b=rwkv_ln_b[j],
            a_norm_g=mlstm_norm_g[j], w_out=even_w_out[j].astype(BF16)))
    odd = [dict(w_in=odd_w_in[j].astype(BF16), w_out=odd_w_out[j].astype(BF16), norm_g=hgrn_norm_g[j])
           for j in range(odd_w_in.shape[0])]
    peer = dict(wqt=jnp.swapaxes(peer_w_q, 1, 2).astype(BF16), keys=peer_keys.astype(BF16), u=peer_u.astype(BF16),
                vt=jnp.swapaxes(peer_v, 1, 2).astype(BF16))
    return dict(norm_mix_g=norm_mix_g, norm_ffn_g=norm_ffn_g, norm_final_g=norm_final_g, even=even, odd=odd,
                peer=peer, lb_logits=hgrn_lb_logits)


def kernel(x_prompt, x_sample, c_prompt, c_sample, state_mlstm_C, state_mlstm_n, state_mlstm_m, state_mlstm_conv, state_rwkv_S, state_rwkv_shift, state_hgrn_S, norm_mix_g, norm_ffn_g, norm_final_g, ada_w, ada_b, even_w_in, even_w_out, mlstm_conv_w, mlstm_conv_b, mlstm_gate_b, mlstm_norm_g, rwkv_mu, rwkv_w0, rwkv_w2, rwkv_a0, rwkv_a2, rwkv_g2, rwkv_k_k, rwkv_k_a, rwkv_r_k, rwkv_ln_g, rwkv_ln_b, odd_w_in, odd_w_out, hgrn_lb_logits, hgrn_norm_g, peer_w_q, peer_keys, peer_u, peer_v):
    w = _prepare_weights(norm_mix_g, norm_ffn_g, norm_final_g, even_w_in, even_w_out, mlstm_conv_w, mlstm_conv_b,
                         mlstm_gate_b, mlstm_norm_g, rwkv_mu, rwkv_w0, rwkv_w2, rwkv_a0, rwkv_a2, rwkv_g2, rwkv_k_k,
                         rwkv_k_a, rwkv_r_k, rwkv_ln_g, rwkv_ln_b, odd_w_in, odd_w_out, hgrn_lb_logits, hgrn_norm_g,
                         peer_w_q, peer_keys, peer_u, peer_v)
    bp = x_prompt.shape[0]
    mod = _ada(jnp.concatenate([c_prompt, c_sample], axis=0), ada_w, ada_b)
    out_p = _trunk(x_prompt, mod[:, :bp], None, w, True)
    st = (state_mlstm_C, state_mlstm_n, state_mlstm_m, state_mlstm_conv, state_rwkv_S, state_rwkv_shift,
          state_hgrn_S)
    out_s = _trunk(x_sample, mod[:, bp:], st, w, False)
    return (out_p[0], out_s[0]) + tuple(out_p[1:]) + tuple(out_s[1:])
```

```python
import functools
import math

import jax
import jax.numpy as jnp
import numpy as np
from jax import lax
from jax.experimental import pallas as pl
from jax.experimental.pallas import tpu as pltpu

F32 = jnp.float32
BF16 = jnp.bfloat16

EPS = 1e-6
A_HEADS = 4
A_DK = 128
A_CONV = 4
B_HEADS = 8
B_HEAD = 64
B_LORA_W = 64
B_LORA_A = 64
B_LORA_G = 128
B_LN_EPS = 64e-5
C_HEADS = 8
C_DK = 128
P_HEADS = 8
P_NKEYS = 128
P_TOPK = 16

LANES = 128
SUBLANES = 8
MXU_WIDTH = 256
VMEM_LIMIT = 56 * 1024 * 1024
NEG_INF = float("-inf")


def _bdot(a, b):
    return jnp.dot(a.astype(BF16), b.astype(BF16), preferred_element_type=F32)


def _dot_nt(a, b):
    return lax.dot_general(a.astype(BF16), b.astype(BF16), (((1,), (1,)), ((), ())), preferred_element_type=F32)


def _dot_tn(a, b):
    return lax.dot_general(a.astype(BF16), b.astype(BF16), (((0,), (0,)), ((), ())), preferred_element_type=F32)


def _split(x, terms):
    parts = []
    rem = x
    for _ in range(terms):
        p = rem.astype(BF16)
        parts.append(p)
        rem = rem - p.astype(F32)
    return parts


def _dot_split(a, b_bf, terms):
    out = None
    for p in _split(a, terms):
        d = jnp.dot(p, b_bf, preferred_element_type=F32)
        out = d if out is None else out + d
    return out


def _dot_split_left(a_bf, b, terms):
    out = None
    for p in _split(b, terms):
        d = jnp.dot(a_bf, p, preferred_element_type=F32)
        out = d if out is None else out + d
    return out


def _sigmoid(x):
    return 1.0 / (1.0 + jnp.exp(-x))


def _silu(x):
    return x * _sigmoid(x)


def _softplus(x):
    return jnp.maximum(x, 0.0) + jnp.log(1.0 + jnp.exp(-jnp.abs(x)))


def _gelu_tanh(x):
    c0 = -2.0 * math.sqrt(2.0 / math.pi)
    return x / (1.0 + jnp.exp(x * (c0 + (c0 * 0.044715) * (x * x))))


def _rmsnorm(x, g):
    return x * lax.rsqrt(jnp.mean(x * x, axis=-1, keepdims=True) + EPS) * g


def _group_matrix(n, group, value):
    r = lax.broadcasted_iota(jnp.int32, (n, n), 0) // group
    c = lax.broadcasted_iota(jnp.int32, (n, n), 1) // group
    return jnp.where(r == c, value, 0.0).astype(BF16)


def _full_spec(arr):
    nd = arr.ndim
    return pl.BlockSpec(arr.shape, lambda *_: (0,) * nd)


def _params(n_grid):
    return pltpu.CompilerParams(dimension_semantics=("arbitrary",) * n_grid, vmem_limit_bytes=VMEM_LIMIT)


def _tok_spec(tt, width):
    return pl.BlockSpec((1, tt, width), lambda g, i: (g, i, 0))


def _mod_spec(mod3, tt):
    width = mod3.shape[2]
    if mod3.shape[1] == 1:
        return pl.BlockSpec((1, 1, width), lambda g, i: (g, 0, 0))
    return pl.BlockSpec((1, tt, width), lambda g, i: (g, i, 0))


def _ada_body(c_ref, w_ref, b_ref, o_ref):
    c = c_ref[...]
    o_ref[0] = _bdot(_silu(c), w_ref[0]) + b_ref[0]


def _ada(c_all, ada_w, ada_b):
    depth, d, n6 = ada_w.shape
    bc = c_all.shape[0]
    tn = n6 // 4
    return pl.pallas_call(
        _ada_body,
        grid=(depth, n6 // tn),
        in_specs=[pl.BlockSpec((bc, d), lambda l, j: (0, 0)),
                  pl.BlockSpec((1, d, tn), lambda l, j: (l, 0, j)),
                  pl.BlockSpec((1, 1, tn), lambda l, j: (l, 0, j))],
        out_specs=pl.BlockSpec((1, bc, tn), lambda l, j: (l, 0, j)),
        out_shape=jax.ShapeDtypeStruct((depth, bc, n6), F32),
        compiler_params=_params(2),
        name="ada_mod",
    )(c_all, ada_w, ada_b.reshape(depth, 1, n6))


def _even_pre_body(seq, tt, d, aw, bw, *refs):
    n_in = 15 if seq else 19
    ins, outs = refs[:n_in], refs[n_in:]
    (x_ref, mod_ref, ng_ref, w_ref, cw_ref, cb_ref, gb_ref, mu_ref, w0_ref, a0_ref, wla_ref, g2_ref,
     kk_ref, ka_ref, gs_ref) = ins[:15]
    (q_ref, k_ref, v_ref, so_ref, g8_ref, g8t_ref, r_ref, dec_ref, kt_ref, vb_ref, al_ref, be_ref, gg_ref,
     ctail_ref, stail_ref) = outs[:15]
    a2 = 2 * aw
    sw = 3 * bw + B_LORA_W + B_LORA_A + B_LORA_G

    x = x_ref[0]
    mod = mod_ref[0]
    h = _rmsnorm(x, ng_ref[...]) * (1.0 + mod[:, d:2 * d]) + mod[:, 0:d]
    p = jnp.dot(h.astype(BF16), w_ref[...], preferred_element_type=F32)
    u = p[:, 0:a2]
    v = p[:, a2:a2 + aw]
    o = p[:, a2 + aw:a2 + 2 * aw]
    pb = p[:, 4 * aw:4 * aw + sw]
    gt = p[:, 4 * aw + sw:4 * aw + sw + LANES]

    if seq:
        ubuf, pbuf = outs[15], outs[16]

        @pl.when(pl.program_id(1) == 0)
        def _():
            ubuf[0:SUBLANES, :] = jnp.zeros((SUBLANES, a2), F32)
            pbuf[0:SUBLANES, :] = jnp.zeros((SUBLANES, sw), F32)

        ubuf[SUBLANES:SUBLANES + tt, :] = u
        pbuf[SUBLANES:SUBLANES + tt, :] = pb
        u1 = ubuf[SUBLANES - 1:SUBLANES - 1 + tt, :]
        u2 = ubuf[SUBLANES - 2:SUBLANES - 2 + tt, :]
        u3 = ubuf[SUBLANES - 3:SUBLANES - 3 + tt, :]
        pprev = pbuf[SUBLANES - 1:SUBLANES - 1 + tt, :]
        ubuf[0:SUBLANES, :] = u[tt - SUBLANES:tt, :]
        pbuf[0:SUBLANES, :] = pb[tt - SUBLANES:tt, :]
        ctail_ref[0] = u[tt - SUBLANES:tt, :]
        stail_ref[0] = pb[tt - SUBLANES:tt, :]
    else:
        u3, u2, u1, pprev = ins[15][0], ins[16][0], ins[17][0], ins[18][0]
        ctail_ref[0] = u
        stail_ref[0] = pb

    cw = cw_ref[...]
    y = cb_ref[...] + u3 * cw[0:1, :] + u2 * cw[1:2, :] + u1 * cw[2:3, :] + u * cw[3:4, :]
    qk = _silu(y)
    q_ref[0] = qk[:, 0:aw] * (A_DK ** -0.5)
    k_ref[0] = qk[:, aw:a2]
    v_ref[0] = v
    so_ref[0] = _sigmoid(o)
    g = gt + gb_ref[...]
    lane = lax.broadcasted_iota(jnp.int32, g.shape, 1)
    gates = jnp.where(lane < A_HEADS, g, -_softplus(-g))
    g8_ref[0] = gates[:, 0:2 * A_HEADS]
    g8t_ref[0] = gates.T[0:2 * A_HEADS, :]

    xb = pb + mu_ref[...] * (pprev - pb)
    r = xb[:, 0:bw]
    kb = xb[:, bw:2 * bw]
    vb = xb[:, 2 * bw:3 * bw]
    la = xb[:, 3 * bw:3 * bw + LANES]
    gl = xb[:, 3 * bw + LANES:3 * bw + 2 * LANES]
    lane2 = lax.broadcasted_iota(jnp.int32, la.shape, 1)
    la_act = jnp.where(lane2 < B_LORA_W, jnp.tanh(la), la)
    lw = jnp.dot(la_act.astype(BF16), wla_ref[...], preferred_element_type=F32)
    wlog = -_softplus(-(w0_ref[...] + lw[:, 0:bw])) - 0.5
    decay = jnp.exp(-jnp.exp(wlog))
    a = _sigmoid(a0_ref[...] + lw[:, bw:2 * bw])
    gg = jnp.dot(_sigmoid(gl).astype(BF16), g2_ref[...], preferred_element_type=F32)
    kk = kb * kk_ref[...]
    ss = _dot_split(kk * kk, gs_ref[...], 2)
    kkn = kk * lax.rsqrt(jnp.maximum(ss, 1e-24))
    r_ref[0] = r
    dec_ref[0] = decay
    kt_ref[0] = kb * (1.0 + (a - 1.0) * ka_ref[...])
    vb_ref[0] = vb
    al_ref[0] = -kkn
    be_ref[0] = kkn * a
    gg_ref[0] = gg


def _even_pre(x3, mod3, ng, wcat, cw, cb, gb, mu, w0, a0, wla, g2, k_k, k_a, prev, tt):
    g_, ttot, d = x3.shape
    aw = d // 2
    bw = d // 2
    a2 = 2 * aw
    sw = 3 * bw + B_LORA_W + B_LORA_A + B_LORA_G
    seq = prev is None
    gsum = _group_matrix(bw, B_HEAD, 1.0)
    consts = [ng, wcat, cw, cb, gb, mu, w0, a0, wla, g2, k_k, k_a, gsum]
    ins = [x3, mod3] + consts
    in_specs = [_tok_spec(tt, d), _mod_spec(mod3, tt)] + [_full_spec(c) for c in consts]
    if not seq:
        ins += list(prev)
        in_specs += [_tok_spec(tt, a2)] * 3 + [_tok_spec(tt, sw)]
    nt = ttot // tt
    tail_rows = SUBLANES if seq else tt
    tail_tot = g_ * SUBLANES if seq else ttot

    def tail_spec(width):
        if seq:
            return pl.BlockSpec((1, SUBLANES, width), lambda g, i: (g, 0, 0))
        return _tok_spec(tt, width)

    def tail_shape(width):
        if seq:
            return jax.ShapeDtypeStruct((g_, SUBLANES, width), F32)
        return jax.ShapeDtypeStruct((g_, ttot, width), F32)

    tok = lambda w: jax.ShapeDtypeStruct((g_, ttot, w), F32)
    out_shape = [tok(aw), tok(aw), tok(aw), tok(aw), tok(2 * A_HEADS),
                 jax.ShapeDtypeStruct((g_, 2 * A_HEADS, ttot), F32),
                 tok(bw), tok(bw), tok(bw), tok(bw), tok(bw), tok(bw), tok(bw),
                 tail_shape(a2), tail_shape(sw)]
    out_specs = [_tok_spec(tt, aw)] * 4 + [_tok_spec(tt, 2 * A_HEADS),
                                           pl.BlockSpec((1, 2 * A_HEADS, tt), lambda g, i: (g, 0, i))]
    out_specs += [_tok_spec(tt, bw)] * 7 + [tail_spec(a2), tail_spec(sw)]
    scratch = [pltpu.VMEM((tt + SUBLANES, a2), F32), pltpu.VMEM((tt + SUBLANES, sw), F32)] if seq else []
    del tail_rows, tail_tot
    return pl.pallas_call(
        functools.partial(_even_pre_body, seq, tt, d, aw, bw),
        grid=(g_, nt),
        in_specs=in_specs,
        out_specs=out_specs,
        out_shape=out_shape,
        scratch_shapes=scratch,
        compiler_params=_params(2),
        name="even_pre_seq" if seq else "even_pre_step",
    )(*ins)


def _mlstm_chunk_body(L, q_ref, k_ref, v_ref, g8_ref, g8t_ref, c0_ref, n0_ref, m0_ref,
                      h_ref, c1_ref, n1_ref, m1_ref, C_s, n_s, m_s):
    c = pl.program_id(1)

    @pl.when(c == 0)
    def _():
        C_s[...] = c0_ref[0]
        n_s[...] = n0_ref[0]
        m_s[...] = m0_ref[0]

    row = lax.broadcasted_iota(jnp.int32, (L, L), 0)
    col = lax.broadcasted_iota(jnp.int32, (L, L), 1)
    causal = col <= row
    tril = jnp.where(causal, 1.0, 0.0).astype(BF16)
    triu = jnp.where(row <= col, 1.0, 0.0).astype(BF16)
    g8 = g8_ref[0]
    g8t = g8t_ref[0]
    bcols = _dot_split_left(tril, g8, 3)
    brows = _dot_split(g8t, triu, 3)
    for hd in range(A_HEADS):
        sl = slice(hd * A_DK, (hd + 1) * A_DK)
        q = q_ref[0, :, sl]
        k = k_ref[0, :, sl]
        v = v_ref[0, :, sl]
        i_row = g8t[hd:hd + 1, :]
        i_col = g8[:, hd:hd + 1]
        b_row = brows[A_HEADS + hd:A_HEADS + hd + 1, :]
        b_col = bcols[:, A_HEADS + hd:A_HEADS + hd + 1]
        m_prev = m_s[hd:hd + 1, 0:1]
        C = C_s[hd]
        n = n_s[hd:hd + 1, :]
        dmat = jnp.where(causal, b_col - b_row + i_row, NEG_INF)
        inter = b_col + m_prev
        m_t = jnp.maximum(inter, jnp.max(dmat, axis=1, keepdims=True))
        w_intra = jnp.exp(dmat - m_t)
        w_inter = jnp.exp(inter - m_t)
        s = _dot_nt(q, k) * w_intra
        num = _bdot(s, v) + w_inter * _bdot(q, C)
        den = jnp.sum(s, axis=1, keepdims=True) + w_inter * jnp.sum(q * n, axis=1, keepdims=True)
        h_ref[0, :, sl] = num / jnp.maximum(jnp.abs(den), jnp.exp(-m_t))
        m_new = m_t[L - 1:L, :]
        b_last = b_col[L - 1:L, :]
        kw = k * jnp.exp(b_last - b_col + i_col - m_new)
        carry = jnp.exp(b_last + m_prev - m_new)
        C_s[hd] = carry * C + _dot_tn(kw, v)
        n_s[hd:hd + 1, :] = carry * n + jnp.sum(kw, axis=0, keepdims=True)
        m_s[hd:hd + 1, :] = jnp.broadcast_to(m_new, (1, LANES))

    @pl.when(c == pl.num_programs(1) - 1)
    def _():
        c1_ref[0] = C_s[...]
        n1_ref[0] = n_s[...]
        m1_ref[0] = m_s[...]


def _mlstm_chunk(q, k, v, g8, g8t, C0, n0p, m0p, L):
    b, t, aw = q.shape
    st = lambda *s: pl.BlockSpec((1,) + s, lambda bb, c: (bb,) + (0,) * len(s))
    return pl.pallas_call(
        functools.partial(_mlstm_chunk_body, L),
        grid=(b, t // L),
        in_specs=[_tok_spec(L, aw)] * 3 + [_tok_spec(L, 2 * A_HEADS),
                                           pl.BlockSpec((1, 2 * A_HEADS, L), lambda bb, c: (bb, 0, c)),
                                           st(A_HEADS, A_DK, A_DK), st(SUBLANES, LANES), st(SUBLANES, LANES)],
        out_specs=[_tok_spec(L, aw), st(A_HEADS, A_DK, A_DK), st(SUBLANES, LANES), st(SUBLANES, LANES)],
        out_shape=[jax.ShapeDtypeStruct((b, t, aw), F32), jax.ShapeDtypeStruct(C0.shape, F32),
                   jax.ShapeDtypeStruct(n0p.shape, F32), jax.ShapeDtypeStruct(m0p.shape, F32)],
        scratch_shapes=[pltpu.VMEM((A_HEADS, A_DK, A_DK), F32), pltpu.VMEM((SUBLANES, LANES), F32),
                        pltpu.VMEM((SUBLANES, LANES), F32)],
        compiler_params=_params(2),
        name="mlstm_chunk",
    )(q, k, v, g8, g8t, C0, n0p, m0p)


def _row8(x):
    r = lax.broadcasted_iota(jnp.int32, (SUBLANES, x.shape[1]), 0)
    return jnp.where(r == 0, jnp.broadcast_to(x, (SUBLANES, x.shape[1])), 0.0)


def _mlstm_step_body(bb, q_ref, k_ref, v_ref, g8_ref, n0_ref, m0_ref, c0_ref, *rest):
    h_ref, c1_ref, n1_ref, m1_ref = rest[-4:]
    g8 = g8_ref[...]
    m0 = m0_ref[...]
    for hd in range(A_HEADS):
        sl = slice(hd * A_DK, (hd + 1) * A_DK)
        li = g8[:, hd:hd + 1]
        lf = g8[:, A_HEADS + hd:A_HEADS + hd + 1]
        mp = m0[:, hd:hd + 1]
        m1 = jnp.maximum(lf + mp, li)
        wi = jnp.exp(li - m1)
        wf = jnp.exp(lf + mp - m1)
        floor = jnp.exp(-m1)
        for b in range(bb):
            q = q_ref[b:b + 1, sl]
            k = k_ref[b:b + 1, sl]
            v = v_ref[b:b + 1, sl]
            wib = wi[b:b + 1, :]
            wfb = wf[b:b + 1, :]
            C1 = wfb * c0_ref[0, b, hd] + wib * _dot_tn(_row8(k), _row8(v))
            n1 = wfb * n0_ref[b, hd:hd + 1, :] + wib * k
            num = _bdot(_row8(q), C1)[0:1, :]
            den = jnp.sum(q * n1, axis=1, keepdims=True)
            h_ref[b:b + 1, sl] = num / jnp.maximum(jnp.abs(den), floor[b:b + 1, :])
            c1_ref[0, b, hd] = C1
            n1_ref[b, hd:hd + 1, :] = n1
        m1_ref[:, hd:hd + 1] = m1


def _rwkv_body(bb, tb, ty, al_ref, w_ref, be_ref, kt_ref, r_ref, v_ref, s0_ref, y_ref, s1_ref, S_s, t3_s, y_s):
    @pl.when(pl.program_id(1) == 0)
    def _():
        S_s[...] = s0_ref[...]

    npair = B_HEADS // 2
    ntile = bb * npair
    nrow = min(tb, SUBLANES)
    row = lax.broadcasted_iota(jnp.int32, (B_HEAD, LANES), 0)
    lane = lax.broadcasted_iota(jnp.int32, (B_HEAD, LANES), 1)
    eye_pair = row == (lane % B_HEAD)
    wred = _group_matrix(2 * LANES, B_HEAD, 1.0)
    kk = lax.broadcasted_iota(jnp.int32, (nrow * LANES, LANES), 0)
    ll = lax.broadcasted_iota(jnp.int32, (nrow * LANES, LANES), 1)
    sel_head = ((kk % LANES) // B_HEAD) == (ll // B_HEAD)
    sel_off = (ll % B_HEAD) - (kk // LANES)

    def group(t8, carry):
        base = t8 * nrow if isinstance(t8, int) else pl.multiple_of(t8 * nrow, nrow)
        rows = pl.ds(base, nrow)
        for i in range(nrow):
            ri = slice(i, i + 1)
            lhs = []
            for b in range(bb):
                for p in range(npair):
                    sl = pl.ds(p * LANES, LANES)
                    lhs.append(jnp.concatenate(
                        [(S_s[b, p] * al_ref[b, rows, sl][ri]).astype(BF16),
                         jnp.where(eye_pair, v_ref[b, rows, sl][ri], 0.0).astype(BF16)], axis=1))
            red = jnp.dot(jnp.concatenate(lhs, axis=0), wred, preferred_element_type=F32)
            for b in range(bb):
                for p in range(npair):
                    sl = pl.ds(p * LANES, LANES)
                    idx = b * npair + p
                    sa = red[idx * B_HEAD:(idx + 1) * B_HEAD, 0:LANES]
                    vcol = red[idx * B_HEAD:(idx + 1) * B_HEAD, LANES:2 * LANES]
                    S = (S_s[b, p] * w_ref[b, rows, sl][ri] + sa * be_ref[b, rows, sl][ri]
                         + vcol * kt_ref[b, rows, sl][ri])
                    S_s[b, p] = S
                    t3_s[idx * B_HEAD:(idx + 1) * B_HEAD, i * LANES:(i + 1) * LANES] = (
                        S * r_ref[b, rows, sl][ri]).astype(BF16)
        wsel = jnp.where(sel_head & (sel_off == base % ty), 1.0, 0.0).astype(BF16)
        yg = jnp.dot(t3_s[...], wsel, preferred_element_type=F32)

        @pl.when(base % ty == 0)
        def _():
            y_s[...] = yg

        @pl.when(base % ty != 0)
        def _():
            y_s[...] += yg

        @pl.when((base + nrow) % ty == 0)
        def _():
            start = (base // ty) * ty
            trow = pl.ds(start, ty) if isinstance(start, int) else pl.ds(pl.multiple_of(start, ty), ty)
            half = npair * B_HEAD
            for b in range(bb):
                yt = y_s[b * half:(b + 1) * half, :].T
                y_ref[b, trow, 0:half] = yt[0:ty, :]
                y_ref[b, trow, half:2 * half] = yt[B_HEAD:B_HEAD + ty, :]
        return carry

    if tb == nrow:
        group(0, 0)
    else:
        lax.fori_loop(0, tb // nrow, group, 0)

    @pl.when(pl.program_id(1) == pl.num_programs(1) - 1)
    def _():
        s1_ref[...] = S_s[...]


def _rwkv_scan(al, w, be, kt, r, v, S0p, bb, tb):
    b, t, bw = al.shape
    npair = B_HEADS // 2
    ty = min(B_HEAD, t)
    nb = b // bb
    tok = pl.BlockSpec((bb, tb, bw), lambda i, j: (i, j, 0))
    st = pl.BlockSpec((bb, npair, B_HEAD, LANES), lambda i, j: (i, 0, 0, 0))
    nrow = min(tb, SUBLANES)
    rows = bb * npair * B_HEAD
    return pl.pallas_call(
        functools.partial(_rwkv_body, bb, tb, ty),
        grid=(nb, t // tb),
        in_specs=[tok] * 6 + [st],
        out_specs=[tok, st],
        out_shape=[jax.ShapeDtypeStruct((b, t, bw), F32), jax.ShapeDtypeStruct(S0p.shape, F32)],
        scratch_shapes=[pltpu.VMEM((bb, npair, B_HEAD, LANES), F32), pltpu.VMEM((rows, nrow * LANES), BF16),
                        pltpu.VMEM((rows, LANES), F32)],
        compiler_params=_params(2),
        name="rwkv_scan",
    )(al, w, be, kt, r, v, S0p)


PAIRED_HEADS = tuple(q + (B_HEADS // 2) * hh for q in range(B_HEADS // 2) for hh in range(2))


def _paired_channels(offset=0):
    return np.concatenate([np.arange(h * B_HEAD, (h + 1) * B_HEAD) for h in PAIRED_HEADS]) + offset


def _paired_to_natural(x):
    pos = {h: i for i, h in enumerate(PAIRED_HEADS)}
    return jnp.concatenate([x[:, pos[h] * B_HEAD:(pos[h] + 1) * B_HEAD] for h in range(B_HEADS)], axis=1)


def _even_post_body(d, aw, bw, x_ref, mod_ref, ha_ref, so_ref, y_ref, r_ref, kt_ref, vb_ref, gg_ref,
                    ang_ref, lng_ref, lnb_ref, rk_ref, wo_ref, o_ref):
    ga = _group_matrix(aw, A_DK, 1.0 / A_DK)
    gb_mean = _group_matrix(bw, B_HEAD, 1.0 / B_HEAD)
    gb_sum = _group_matrix(bw, B_HEAD, 1.0)
    r, kt, vb, gg = (_paired_to_natural(ref[0]) for ref in (r_ref, kt_ref, vb_ref, gg_ref))
    ha = ha_ref[0]
    ha = ha * lax.rsqrt(_dot_split(ha * ha, ga, 2) + EPS) * ang_ref[...] * so_ref[0]
    y = y_ref[0]
    yc = y - _dot_split(y, gb_mean, 2)
    var = _dot_split(yc * yc, gb_mean, 2)
    yn = yc * lax.rsqrt(var + B_LN_EPS) * lng_ref[...] + lnb_ref[...]
    bonus = _dot_split(r * kt * rk_ref[...], gb_sum, 2)
    y2 = (yn + bonus * vb) * gg
    out = (jnp.dot(ha.astype(BF16), wo_ref[0:aw, :], preferred_element_type=F32)
           + jnp.dot(y2.astype(BF16), wo_ref[aw:aw + bw, :], preferred_element_type=F32))
    o_ref[0] = x_ref[0] + mod_ref[0][:, 2 * d:3 * d] * out


def _even_post(x3, mod3, ha, so, y, r, kt, vb, gg, ang, lng, lnb, rk, wo, tt):
    g_, ttot, d = x3.shape
    aw = d // 2
    bw = d // 2
    consts = [ang, lng, lnb, rk, wo]
    return pl.pallas_call(
        functools.partial(_even_post_body, d, aw, bw),
        grid=(g_, ttot // tt),
        in_specs=[_tok_spec(tt, d), _mod_spec(mod3, tt)] + [_tok_spec(tt, aw)] * 7 + [_full_spec(c) for c in consts],
        out_specs=_tok_spec(tt, d),
        out_shape=jax.ShapeDtypeStruct(x3.shape, F32),
        compiler_params=_params(2),
        name="even_post",
    )(x3, mod3, ha, so, y, r, kt, vb, gg, *consts)


def _odd_pre_body(j, d, x_ref, mod_ref, ng_ref, w_ref, lbl_ref, q_ref, k_ref, v_ref, lf_ref, sg_ref):
    x = x_ref[0]
    mod = mod_ref[0]
    h = _rmsnorm(x, ng_ref[...]) * (1.0 + mod[:, d:2 * d]) + mod[:, 0:d]
    p = jnp.dot(h.astype(BF16), w_ref[...], preferred_element_type=F32)
    lbl = lbl_ref[...]
    n_odd = lbl.shape[0]
    mx = lbl[0:1, :]
    for i in range(1, n_odd):
        mx = jnp.maximum(mx, lbl[i:i + 1, :])
    ex = [jnp.exp(lbl[i:i + 1, :] - mx) for i in range(n_odd)]
    tot = ex[0]
    for i in range(1, n_odd):
        tot = tot + ex[i]
    lb = jnp.zeros_like(mx)
    for i in range(1, j + 1):
        lb = lb + ex[i] / tot
    f = lb + (1.0 - lb) * _sigmoid(p[:, d:2 * d])
    q_ref[0] = _silu(p[:, 0:d])
    k_ref[0] = 1.0 - f
    v_ref[0] = p[:, 2 * d:3 * d]
    lf_ref[0] = jnp.log(f)
    sg_ref[0] = _silu(p[:, 3 * d:4 * d])


def _odd_pre(x3, mod3, ng, w_in, lbl, j, tt):
    g_, ttot, d = x3.shape
    consts = [ng, w_in, lbl]
    return pl.pallas_call(
        functools.partial(_odd_pre_body, j, d),
        grid=(g_, ttot // tt),
        in_specs=[_tok_spec(tt, d), _mod_spec(mod3, tt)] + [_full_spec(c) for c in consts],
        out_specs=[_tok_spec(tt, d)] * 5,
        out_shape=[jax.ShapeDtypeStruct(x3.shape, F32)] * 5,
        compiler_params=_params(2),
        name="odd_pre",
    )(x3, mod3, *consts)


def _col_bcast(row, terms):
    parts = _split(row, terms)
    r = lax.broadcasted_iota(jnp.int32, (SUBLANES, row.shape[1]), 0)
    lhs = jnp.zeros((SUBLANES, row.shape[1]), F32)
    for i, p in enumerate(parts):
        lhs = jnp.where(r == i, jnp.broadcast_to(p.astype(F32), lhs.shape), lhs)
    ones = jnp.where(lax.broadcasted_iota(jnp.int32, (SUBLANES, LANES), 0) < terms, 1.0, 0.0).astype(BF16)
    return lax.dot_general(lhs.astype(BF16), ones, (((0,), (0,)), ((), ())), preferred_element_type=F32)


def _hgrn_chunk_body(L, sub, nchunk, hg, q_ref, k_ref, v_ref, g_ref, s0_ref, o_ref, s1_ref, S_s):
    @pl.when(pl.program_id(2) == 0)
    def _():
        S_s[...] = s0_ref[0]

    nj = L // sub
    rowi = lax.broadcasted_iota(jnp.int32, (L, C_DK), 0)
    rr = lax.broadcasted_iota(jnp.int32, (L, L), 0)
    cc = lax.broadcasted_iota(jnp.int32, (L, L), 1)
    tril = jnp.where(cc <= rr, 1.0, 0.0).astype(BF16)
    ones = jnp.ones((C_DK, LANES), BF16)

    def chunk(ci, carry):
        base = pl.multiple_of(ci * L, L)
        rows = pl.ds(base, L)
        b_all = _dot_split_left(tril, g_ref[0, rows, :], 3)
        heads = []
        for hd in range(hg):
            sl = slice(hd * C_DK, (hd + 1) * C_DK)
            heads.append((q_ref[0, rows, sl], k_ref[0, rows, sl], v_ref[0, rows, sl], b_all[:, sl], S_s[hd]))
        outs = [_bdot(q * jnp.exp(b), S) for (q, k, v, b, S) in heads]
        for dlt in range(sub):
            prs, vss = [], []
            for (q, k, v, b, S) in heads:
                if dlt == 0:
                    prs.append((q * k).astype(BF16))
                    vss.append(v)
                else:
                    ok = (rowi % sub) >= dlt
                    e = jnp.exp(jnp.where(ok, b - pltpu.roll(b, dlt, 0), NEG_INF))
                    prs.append((q * pltpu.roll(k, dlt, 0) * e).astype(BF16))
                    vss.append(pltpu.roll(v, dlt, 0))
            rs = jnp.dot(jnp.concatenate(prs, axis=0), ones, preferred_element_type=F32)
            outs = [o + rs[i * L:(i + 1) * L, :] * vs for i, (o, vs) in enumerate(zip(outs, vss))]
        for hd, ((q, k, v, b, S), o) in enumerate(zip(heads, outs)):
            if nj > 1:
                amat = None
                for jj in range(nj - 1):
                    end = (jj + 1) * sub
                    bref = b[end - 1:end, :]
                    qj = q * jnp.exp(jnp.where(rowi >= end, b - bref, NEG_INF))
                    kj = k * jnp.exp(jnp.where((rowi >= end - sub) & (rowi < end), bref - b, NEG_INF))
                    a = _dot_nt(qj, kj)
                    amat = a if amat is None else amat + a
                o = o + _bdot(amat, v)
            o_ref[0, rows, hd * C_DK:(hd + 1) * C_DK] = o
            b_last = b[L - 1:L, :]
            S_s[hd] = _col_bcast(jnp.exp(b_last), 3) * S + _dot_tn(k * jnp.exp(b_last - b), v)
        return carry

    lax.fori_loop(0, nchunk, chunk, 0)

    @pl.when(pl.program_id(2) == pl.num_programs(2) - 1)
    def _():
        s1_ref[0] = S_s[...]


def _hgrn_chunk(q, k, v, lf, S0, tblk, L, sub, hg):
    b, t, d = q.shape
    tok = pl.BlockSpec((1, tblk, hg * C_DK), lambda bb, h, c: (bb, c, h))
    st = pl.BlockSpec((1, hg, C_DK, C_DK), lambda bb, h, c: (bb, h, 0, 0))
    return pl.pallas_call(
        functools.partial(_hgrn_chunk_body, L, sub, tblk // L, hg),
        grid=(b, C_HEADS // hg, t // tblk),
        in_specs=[tok] * 4 + [st],
        out_specs=[tok, st],
        out_shape=[jax.ShapeDtypeStruct((b, t, d), F32), jax.ShapeDtypeStruct(S0.shape, F32)],
        scratch_shapes=[pltpu.VMEM((hg, C_DK, C_DK), F32)],
        compiler_params=_params(3),
        name="hgrn_chunk",
    )(q, k, v, lf, S0)


def _hgrn_step_body(bb, q_ref, k_ref, v_ref, g_ref, s0_ref, *rest):
    o_ref, s1_ref = rest[-2], rest[-1]
    for b in range(bb):
        for hd in range(C_HEADS):
            sl = slice(hd * C_DK, (hd + 1) * C_DK)
            q = q_ref[b:b + 1, sl]
            k = k_ref[b:b + 1, sl]
            v = v_ref[b:b + 1, sl]
            f = jnp.exp(g_ref[b:b + 1, sl])
            S1 = _col_bcast(f, 3) * s0_ref[0, b, hd] + _dot_tn(_row8(k), _row8(v))
            o_ref[b:b + 1, sl] = _bdot(_row8(q), S1)[0:1, :]
            s1_ref[0, b, hd] = S1


def _layer_state_call(body, name, rows_in, row_out_shape, row_spec, state_all, layer, state_acc, bb, extra_out=()):
    b = state_all.shape[1]
    blk = (1, bb) + state_all.shape[2:]
    zeros = (0,) * (state_all.ndim - 2)
    st = pl.BlockSpec(blk, lambda i: (layer, i) + zeros)
    ins = list(rows_in) + [state_all]
    in_specs = [row_spec(r) for r in rows_in] + [st]
    aliases = {}
    if state_acc is not None:
        ins.append(state_acc)
        in_specs.append(pl.BlockSpec(memory_space=pl.ANY))
        aliases = {len(ins) - 1: 1}
    return pl.pallas_call(
        body,
        grid=(b // bb,),
        in_specs=in_specs,
        out_specs=[row_spec(row_out_shape), st] + [row_spec(e) for e in extra_out],
        out_shape=[jax.ShapeDtypeStruct(row_out_shape.shape, F32), jax.ShapeDtypeStruct(state_all.shape, F32)]
        + [jax.ShapeDtypeStruct(e.shape, F32) for e in extra_out],
        input_output_aliases=aliases,
        compiler_params=_params(1),
        name=name,
    )(*ins)


def _row_spec_for(bb):
    def spec(arr):
        blk = (bb,) + arr.shape[1:]
        zeros = (0,) * (arr.ndim - 1)
        return pl.BlockSpec(blk, lambda i: (i,) + zeros)
    return spec


def _mlstm_step(q, k, v, g8, n0, m0, C_all, layer, C_acc, bb):
    return _layer_state_call(functools.partial(_mlstm_step_body, bb), "mlstm_step", [q, k, v, g8, n0, m0], q,
                             _row_spec_for(bb), C_all, layer, C_acc, bb, extra_out=(n0, m0))


def _hgrn_step(q, k, v, lf, S_all, layer, S_acc, bb):
    return _layer_state_call(functools.partial(_hgrn_step_body, bb), "hgrn_step", [q, k, v, lf], q,
                             _row_spec_for(bb), S_all, layer, S_acc, bb)


def _odd_post_body(d, x_ref, mod_ref, o_ref, sg_ref, ng_ref, wo_ref, out_ref):
    gm = _group_matrix(d, C_DK, 1.0 / C_DK)
    o = o_ref[0]
    o = o * lax.rsqrt(_dot_split(o * o, gm, 2) + EPS) * ng_ref[...] * sg_ref[0]
    out = jnp.dot(o.astype(BF16), wo_ref[...], preferred_element_type=F32)
    out_ref[0] = x_ref[0] + mod_ref[0][:, 2 * d:3 * d] * out


def _odd_post(x3, mod3, o, sg, ng_tiled, wo, tt):
    g_, ttot, d = x3.shape
    consts = [ng_tiled, wo]
    return pl.pallas_call(
        functools.partial(_odd_post_body, d),
        grid=(g_, ttot // tt),
        in_specs=[_tok_spec(tt, d), _mod_spec(mod3, tt)] + [_tok_spec(tt, d)] * 2 + [_full_spec(c) for c in consts],
        out_specs=_tok_spec(tt, d),
        out_shape=jax.ShapeDtypeStruct(x3.shape, F32),
        compiler_params=_params(2),
        name="odd_post",
    )(x3, mod3, o, sg, *consts)


P_NCAND = P_TOPK + 1
P_CAND_ROWS = 3 * SUBLANES


def _oddeven_merge(lo, hi, r):
    step = r * 2
    if step < hi - lo:
        yield from _oddeven_merge(lo, hi, step)
        yield from _oddeven_merge(lo + r, hi, step)
        yield from [(i, i + r) for i in range(lo + r, hi - r, step)]
    else:
        yield (lo, lo + r)


def _oddeven_merge_sort(lo, hi):
    if hi - lo >= 1:
        mid = lo + (hi - lo) // 2
        yield from _oddeven_merge_sort(lo, mid)
        yield from _oddeven_merge_sort(mid + 1, hi)
        yield from _oddeven_merge(lo, hi, 1)


def _top_sorted(s):
    nv = s.shape[0] // SUBLANES
    assert nv == P_TOPK
    tiles = [s[r * SUBLANES:(r + 1) * SUBLANES, :] for r in range(nv)]
    v = list(tiles)
    for i, j in _oddeven_merge_sort(0, nv - 1):
        v[i], v[j] = jnp.maximum(v[i], v[j]), jnp.minimum(v[i], v[j])
    shift = SUBLANES // 2
    while shift >= 1:
        other = [pltpu.roll(x, shift, 0) for x in v]
        v = [jnp.maximum(v[r], other[nv - 1 - r]) for r in range(nv)]
        dist = nv // 2
        while dist >= 1:
            for r in range(nv):
                if r & dist == 0:
                    v[r], v[r + dist] = jnp.maximum(v[r], v[r + dist]), jnp.minimum(v[r], v[r + dist])
            dist //= 2
        shift //= 2
    rest = None
    for x in tiles:
        m = jnp.where(x < v[nv - 1], x, NEG_INF)
        rest = m if rest is None else jnp.maximum(rest, m)
    nxt = jnp.max(rest, axis=0, keepdims=True)
    r8 = lax.broadcasted_iota(jnp.int32, (SUBLANES, s.shape[1]), 0)
    groups = []
    for g in range(nv // SUBLANES):
        acc = v[g * SUBLANES + SUBLANES - 1]
        for r in range(SUBLANES - 1):
            acc = jnp.where(r8 == r, v[g * SUBLANES + r], acc)
        groups.append(acc)
    groups.append(jnp.where(r8 == 0, nxt, NEG_INF))
    return jnp.concatenate(groups, axis=0)


def _peer_body(tm, te, ne, d, x_ref, mod_ref, ng_ref, wqt_ref, keys_ref, u_ref, vt_ref, vl_ref, o_ref,
               ht_s, s1_s, e1_s, thr_s, e2_s, gate_s, w_s, acc_s):
    j = pl.program_id(2)
    dk2 = keys_ref.shape[4]
    nsub = te // P_NKEYS

    @pl.when(j == 0)
    def _():
        x = x_ref[0]
        mod = mod_ref[0]
        h = _rmsnorm(x, ng_ref[...]) * (1.0 + mod[:, 4 * d:5 * d]) + mod[:, 3 * d:4 * d]
        ht = h.T.astype(BF16)
        ht_s[...] = ht
        qt = jnp.dot(wqt_ref[0], ht, preferred_element_type=F32)
        r8 = lax.broadcasted_iota(jnp.int32, (SUBLANES, LANES), 0)
        for hd in range(P_HEADS):
            s1_all = _bdot(keys_ref[0, hd, 0], qt[(2 * hd) * dk2:(2 * hd + 1) * dk2, :])
            s2_all = _bdot(keys_ref[0, hd, 1], qt[(2 * hd + 1) * dk2:(2 * hd + 2) * dk2, :])
            for tc in range(tm // LANES):
                ls = slice(tc * LANES, (tc + 1) * LANES)
                s1 = s1_all[:, ls]
                s2 = s2_all[:, ls]
                a = _top_sorted(s1)
                b = _top_sorted(s2)
                blocks = [a[0:1, :] + b]
                for i in range(2, SUBLANES + 1):
                    blocks.append(jnp.where(r8 < P_NCAND // i, a[i - 1:i, :] + b[0:SUBLANES, :], NEG_INF))
                blocks.append(a[SUBLANES:P_CAND_ROWS, :] + b[0:1, :])
                cand = jnp.concatenate(blocks, axis=0)
                cur = cand
                best = None
                for _ in range(P_NCAND):
                    prev, best = best, jnp.max(cur, axis=0, keepdims=True)
                    cur = jnp.where(cur == best, NEG_INF, cur)
                tau = 0.5 * (prev + best)
                top = a[0:1, :] + b[0:1, :]
                z = jnp.sum(jnp.where(cand >= tau, jnp.exp(cand - top), 0.0), axis=0, keepdims=True)
                s1_s[hd, :, ls] = s1
                e1_s[hd, :, ls] = jnp.exp(s1 - a[0:1, :])
                thr_s[hd, :, ls] = tau - s2
                e2_s[hd, :, ls] = jnp.exp(s2 - b[0:1, :]) / z
        acc_s[...] = jnp.zeros_like(acc_s)
        w_s[1] = jnp.zeros((te, tm), BF16)

    assert nsub == SUBLANES
    grp = pl.ds(pl.multiple_of(j * nsub, nsub), nsub)

    def gates(tc):
        ls = slice(tc * LANES, (tc + 1) * LANES)
        s1g = [s1_s[hd, grp, ls] for hd in range(P_HEADS)]
        e1g = [e1_s[hd, grp, ls] for hd in range(P_HEADS)]
        for ii in range(nsub):
            rows = slice(ii * P_NKEYS, (ii + 1) * P_NKEYS)
            gate = jnp.zeros((P_NKEYS, LANES), F32)
            for hd in range(P_HEADS):
                keep = s1g[hd][ii:ii + 1, :] >= thr_s[hd, :, ls]
                gate = gate + jnp.where(keep, e2_s[hd, :, ls], 0.0) * e1g[hd][ii:ii + 1, :]
            gate_s[rows, ls] = gate

    piece = min(tm, MXU_WIDTH)
    tc_per_piece = piece // LANES
    for pc in range(tm // piece):
        cs = slice(pc * piece, (pc + 1) * piece)
        acc_s[:, cs] += jnp.dot(vt_ref[0], w_s[(j + 1) % 2, :, cs], preferred_element_type=F32)
        for tc in range(pc * tc_per_piece, pc * tc_per_piece + (tc_per_piece + 1) // 2):
            gates(tc)
        act = jnp.dot(u_ref[0], ht_s[:, cs], preferred_element_type=F32)
        for tc in range(pc * tc_per_piece + (tc_per_piece + 1) // 2, (pc + 1) * tc_per_piece):
            gates(tc)
        w_s[j % 2, :, cs] = (gate_s[:, cs] * _gelu_tanh(act)).astype(BF16)

    @pl.when(j == ne - 1)
    def _():
        acc = acc_s[...] + jnp.dot(vl_ref[0], w_s[(ne - 1) % 2], preferred_element_type=F32)
        o_ref[0] = x_ref[0] + mod_ref[0][:, 5 * d:6 * d] * acc.T


def _peer(x3, mod3, ng, wqt_all, keys_all, u_all, vt_all, layer, tm, te):
    g_, ttot, d = x3.shape
    ne = u_all.shape[1] // te
    nk = P_NKEYS
    if mod3.shape[1] == 1:
        mod_spec = pl.BlockSpec((1, 1, mod3.shape[2]), lambda g, i, j: (g, 0, 0))
    else:
        mod_spec = pl.BlockSpec((1, tm, mod3.shape[2]), lambda g, i, j: (g, i, 0))
    return pl.pallas_call(
        functools.partial(_peer_body, tm, te, ne, d),
        grid=(g_, ttot // tm, ne),
        in_specs=[pl.BlockSpec((1, tm, d), lambda g, i, j: (g, i, 0)), mod_spec,
                  pl.BlockSpec(ng.shape, lambda g, i, j: (0, 0)),
                  pl.BlockSpec((1,) + wqt_all.shape[1:], lambda g, i, j: (layer, 0, 0)),
                  pl.BlockSpec((1,) + keys_all.shape[1:], lambda g, i, j: (layer, 0, 0, 0, 0)),
                  pl.BlockSpec((1, te, d), lambda g, i, j: (layer, j, 0)),
                  pl.BlockSpec((1, d, te), lambda g, i, j: (layer, 0, jnp.maximum(j - 1, 0))),
                  pl.BlockSpec((1, d, te), lambda g, i, j: (layer, 0, ne - 1))],
        out_specs=pl.BlockSpec((1, tm, d), lambda g, i, j: (g, i, 0)),
        out_shape=jax.ShapeDtypeStruct(x3.shape, F32),
        scratch_shapes=[pltpu.VMEM((d, tm), BF16),
                        pltpu.VMEM((P_HEADS, nk, tm), F32), pltpu.VMEM((P_HEADS, nk, tm), F32),
                        pltpu.VMEM((P_HEADS, nk, tm), F32), pltpu.VMEM((P_HEADS, nk, tm), F32),
                        pltpu.VMEM((te, tm), F32), pltpu.VMEM((2, te, tm), BF16), pltpu.VMEM((d, tm), F32)],
        compiler_params=_params(3),
        name="peer",
    )(x3, mod3, ng, wqt_all, keys_all, u_all, vt_all, vt_all)


def _final_norm_body(x_ref, g_ref, o_ref):
    o_ref[0] = _rmsnorm(x_ref[0], g_ref[...])


def _final_norm(x3, g, tt):
    g_, ttot, d = x3.shape
    return pl.pallas_call(
        _final_norm_body,
        grid=(g_, ttot // tt),
        in_specs=[_tok_spec(tt, d), _full_spec(g)],
        out_specs=_tok_spec(tt, d),
        out_shape=jax.ShapeDtypeStruct(x3.shape, F32),
        compiler_params=_params(2),
        name="final_norm",
    )(x3, g)


def _pair_state(S):
    b = S.shape[0]
    return S.reshape(b, 2, B_HEADS // 2, B_HEAD, B_HEAD).transpose(0, 2, 3, 1, 4).reshape(b, B_HEADS // 2, B_HEAD, LANES)


def _unpair_state(Sp):
    b = Sp.shape[0]
    return Sp.reshape(b, B_HEADS // 2, B_HEAD, 2, B_HEAD).transpose(0, 3, 1, 2, 4).reshape(b, B_HEADS, B_HEAD, B_HEAD)


def _shift_perm(bw):
    return np.concatenate([_paired_channels(0), _paired_channels(bw), _paired_channels(2 * bw),
                           np.arange(3 * bw, 3 * bw + B_LORA_W + B_LORA_A + B_LORA_G)])


def _trunk(x, mod_all, st, w, seq):
    bsz, t, d = x.shape
    depth = mod_all.shape[0]
    aw = d // 2
    bw = d // 2
    if seq:
        x3 = x
        tt = min(256, t)
        tm = min(512, t)
        mods = [mod_all[l][:, None, :] for l in range(depth)]
    else:
        x3 = x.reshape(1, bsz * t, d)
        tt = bsz * t
        tm = bsz * t
        mods = [mod_all[l][None] for l in range(depth)]
    row = lambda v_: v_.reshape(1, -1)
    new_even, new_odd = [], []
    sp = _shift_perm(bw)
    sp_inv = np.argsort(sp)
    C_acc = S_acc = None
    for l in range(depth):
        j = l // 2
        if l % 2 == 0:
            e = w["even"][j]
            if seq:
                prev = None
            else:
                conv0, shift0 = st[3][j], st[5][j]
                prev = (conv0[:, 0][None], conv0[:, 1][None], conv0[:, 2][None], shift0[:, 0][None][..., sp])
            (q, k, v, so, g8, g8t, r, dec, kt, vb, al, be, gg, ctail, stail) = _even_pre(
                x3, mods[l], row(w["norm_mix_g"][l]), e["wcat"], e["conv_w"], row(e["conv_b"]), e["gate_b"],
                row(e["mu"]), row(e["w0"]), row(e["a0"]), e["wla"], e["g2"], row(e["k_k"]), row(e["k_a"]), prev, tt)
            if seq:
                C0 = jnp.zeros((bsz, A_HEADS, A_DK, A_DK), F32)
                nm0 = jnp.zeros((bsz, SUBLANES, LANES), F32)
                ha, C1, n1p, m1p = _mlstm_chunk(q, k, v, g8, g8t, C0, nm0, nm0, min(256, t))
                n1 = n1p[:, :A_HEADS, :]
                m1 = m1p[:, :A_HEADS, 0]
                S0p = jnp.zeros((bsz, B_HEADS // 2, B_HEAD, LANES), F32)
                y, S1p = _rwkv_scan(al, dec, be, kt, r, vb, S0p, min(8, bsz), min(128, t))
                conv1 = ctail[:, SUBLANES - (A_CONV - 1):, :]
                shift1 = stail[:, SUBLANES - 1:, :][..., sp_inv]
            else:
                ha2, C_acc, n1, m1 = _mlstm_step(q[0], k[0], v[0], g8[0], st[1][j], st[2][j], st[0], j, C_acc, 8)
                C1 = None
                ha = ha2[None]
                tok = lambda a_: a_[0][:, None, :]
                y2, S1p = _rwkv_scan(tok(al), tok(dec), tok(be), tok(kt), tok(r), tok(vb), _pair_state(st[4][j]), 8, 1)
                y = y2.reshape(1, bsz, bw)
                conv1 = jnp.concatenate([st[3][j][:, 1:], ctail[0][:, None, :]], axis=1)
                shift1 = stail[0][:, None, :][..., sp_inv]
            x3 = _even_post(x3, mods[l], ha, so, y, r, kt, vb, gg, row(e["a_norm_g"]), row(e["ln_g"]),
                            row(e["ln_b"]), row(e["r_k"]), e["w_out"], tt)
            new_even.append((C1, n1, m1, conv1, _unpair_state(S1p), shift1))
        else:
            o_ = w["odd"][j]
            q, k, v, lf, sg = _odd_pre(x3, mods[l], row(w["norm_mix_g"][l]), o_["w_in"], w["lb_logits"], j, tt)
            if seq:
                S0 = jnp.zeros((bsz, C_HEADS, C_DK, C_DK), F32)
                o, S1 = _hgrn_chunk(q, k, v, lf, S0, min(256, t), min(64, t), min(16, t), 8)
            else:
                o2, S_acc = _hgrn_step(q[0], k[0], v[0], lf[0], st[6], j, S_acc, 8)
                S1 = None
                o = o2[None]
            x3 = _odd_post(x3, mods[l], o, sg, row(jnp.tile(o_["norm_g"], C_HEADS)), o_["w_out"], tt)
            new_odd.append(S1)
        pw = w["peer"]
        x3 = _peer(x3, mods[l], row(w["norm_ffn_g"][l]), pw["wqt"], pw["keys"], pw["u"], pw["vt"], l, tm, 1024)
    y = _final_norm(x3, row(w["norm_final_g"]), tt).reshape(bsz, t, d)
    ev = [None if new_even[0][i] is None else jnp.stack([s[i] for s in new_even]) for i in range(6)]
    if seq:
        return (y, ev[0], ev[1], ev[2], ev[3], ev[4], ev[5], jnp.stack(new_odd))
    return (y, C_acc, ev[1], ev[2], ev[3], ev[4], ev[5], S_acc)


def _prepare_weights(norm_mix_g, norm_ffn_g, norm_final_g, even_w_in, even_w_out, mlstm_conv_w, mlstm_conv_b,
                     mlstm_gate_b, mlstm_norm_g, rwkv_mu, rwkv_w0, rwkv_w2, rwkv_a0, rwkv_a2, rwkv_g2, rwkv_k_k,
                     rwkv_k_a, rwkv_r_k, rwkv_ln_g, rwkv_ln_b, odd_w_in, odd_w_out, hgrn_lb_logits, hgrn_norm_g,
                     peer_w_q, peer_keys, peer_u, peer_v):
    d = even_w_in.shape[1]
    aw = d // 2
    bw = d // 2
    sw = 3 * bw + B_LORA_W + B_LORA_A + B_LORA_G
    even = []
    for j in range(even_w_in.shape[0]):
        wi = even_w_in[j]
        gates_w = wi[:, 4 * aw:4 * aw + 2 * A_HEADS]
        pc = _paired_channels()
        sp = _shift_perm(bw)
        wcat = jnp.concatenate([wi[:, 0:4 * aw], wi[:, 4 * aw + 2 * A_HEADS:][:, sp],
                                jnp.pad(gates_w, ((0, 0), (0, LANES - 2 * A_HEADS)))], axis=1).astype(BF16)
        assert wcat.shape[1] == 4 * aw + sw + LANES
        wla = jnp.zeros((B_LORA_W + B_LORA_A, 2 * bw), F32)
        wla = wla.at[:B_LORA_W, :bw].set(rwkv_w2[j][:, pc]).at[B_LORA_W:, bw:].set(rwkv_a2[j][:, pc]).astype(BF16)
        even.append(dict(
            wcat=wcat, conv_w=mlstm_conv_w[j], conv_b=mlstm_conv_b[j],
            gate_b=jnp.pad(mlstm_gate_b[j], (0, LANES - 2 * A_HEADS)).reshape(1, LANES),
            mu=rwkv_mu[j][sp], w0=rwkv_w0[j][pc], a0=rwkv_a0[j][pc], wla=wla, g2=rwkv_g2[j][:, pc].astype(BF16),
            k_k=rwkv_k_k[j][pc], k_a=rwkv_k_a[j][pc], r_k=rwkv_r_k[j], ln_g=rwkv_ln_g[j], ln_b=rwkv_ln_b[j],
            a_norm_g=mlstm_norm_g[j], w_out=even_w_out[j].astype(BF16)))
    odd = [dict(w_in=odd_w_in[j].astype(BF16), w_out=odd_w_out[j].astype(BF16), norm_g=hgrn_norm_g[j])
           for j in range(odd_w_in.shape[0])]
    peer = dict(wqt=jnp.swapaxes(peer_w_q, 1, 2).astype(BF16), keys=peer_keys.astype(BF16), u=peer_u.astype(BF16),
                vt=jnp.swapaxes(peer_v, 1, 2).astype(BF16))
    return dict(norm_mix_g=norm_mix_g, norm_ffn_g=norm_ffn_g, norm_final_g=norm_final_g, even=even, odd=odd,
                peer=peer, lb_logits=hgrn_lb_logits)


def kernel(x_prompt, x_sample, c_prompt, c_sample, state_mlstm_C, state_mlstm_n, state_mlstm_m, state_mlstm_conv, state_rwkv_S, state_rwkv_shift, state_hgrn_S, norm_mix_g, norm_ffn_g, norm_final_g, ada_w, ada_b, even_w_in, even_w_out, mlstm_conv_w, mlstm_conv_b, mlstm_gate_b, mlstm_norm_g, rwkv_mu, rwkv_w0, rwkv_w2, rwkv_a0, rwkv_a2, rwkv_g2, rwkv_k_k, rwkv_k_a, rwkv_r_k, rwkv_ln_g, rwkv_ln_b, odd_w_in, odd_w_out, hgrn_lb_logits, hgrn_norm_g, peer_w_q, peer_keys, peer_u, peer_v):
    w = _prepare_weights(norm_mix_g, norm_ffn_g, norm_final_g, even_w_in, even_w_out, mlstm_conv_w, mlstm_conv_b,
                         mlstm_gate_b, mlstm_norm_g, rwkv_mu, rwkv_w0, rwkv_w2, rwkv_a0, rwkv_a2, rwkv_g2, rwkv_k_k,
                         rwkv_k_a, rwkv_r_k, rwkv_ln_g, rwkv_ln_b, odd_w_in, odd_w_out, hgrn_lb_logits, hgrn_norm_g,
                         peer_w_q, peer_keys, peer_u, peer_v)
    bp = x_prompt.shape[0]
    mod = _ada(jnp.concatenate([c_prompt, c_sample], axis=0), ada_w, ada_b)
    out_p = _trunk(x_prompt, mod[:, :bp], None, w, True)
    st = (state_mlstm_C, state_mlstm_n, state_mlstm_m, state_mlstm_conv, state_rwkv_S, state_rwkv_shift,
          state_hgrn_S)
    out_s = _trunk(x_sample, mod[:, bp:], st, w, False)
    return (out_p[0], out_s[0]) + tuple(out_p[1:]) + tuple(out_s[1:])
```

```python
import functools
import math

import jax
import jax.numpy as jnp
import numpy as np
from jax import lax
from jax.experimental import pallas as pl
from jax.experimental.pallas import tpu as pltpu

F32 = jnp.float32
BF16 = jnp.bfloat16

EPS = 1e-6
A_HEADS = 4
A_DK = 128
A_CONV = 4
B_HEADS = 8
B_HEAD = 64
B_LORA_W = 64
B_LORA_A = 64
B_LORA_G = 128
B_LN_EPS = 64e-5
C_HEADS = 8
C_DK = 128
P_HEADS = 8
P_NKEYS = 128
P_TOPK = 16

LANES = 128
SUBLANES = 8
MXU_WIDTH = 256
VMEM_LIMIT = 56 * 1024 * 1024
NEG_INF = float("-inf")


def _bdot(a, b):
    return jnp.dot(a.astype(BF16), b.astype(BF16), preferred_element_type=F32)


def _dot_nt(a, b):
    return lax.dot_general(a.astype(BF16), b.astype(BF16), (((1,), (1,)), ((), ())), preferred_element_type=F32)


def _dot_tn(a, b):
    return lax.dot_general(a.astype(BF16), b.astype(BF16), (((0,), (0,)), ((), ())), preferred_element_type=F32)


def _split(x, terms):
    parts = []
    rem = x
    for _ in range(terms):
        p = rem.astype(BF16)
        parts.append(p)
        rem = rem - p.astype(F32)
    return parts


def _dot_split(a, b_bf, terms):
    out = None
    for p in _split(a, terms):
        d = jnp.dot(p, b_bf, preferred_element_type=F32)
        out = d if out is None else out + d
    return out


def _dot_split_left(a_bf, b, terms):
    out = None
    for p in _split(b, terms):
        d = jnp.dot(a_bf, p, preferred_element_type=F32)
        out = d if out is None else out + d
    return out


def _sigmoid(x):
    return 1.0 / (1.0 + jnp.exp(-x))


def _silu(x):
    return x * _sigmoid(x)


def _softplus(x):
    return jnp.maximum(x, 0.0) + jnp.log(1.0 + jnp.exp(-jnp.abs(x)))


def _gelu_tanh(x):
    c0 = -2.0 * math.sqrt(2.0 / math.pi)
    return x / (1.0 + jnp.exp(x * (c0 + (c0 * 0.044715) * (x * x))))


def _rmsnorm(x, g):
    return x * lax.rsqrt(jnp.mean(x * x, axis=-1, keepdims=True) + EPS) * g


def _group_matrix(n, group, value):
    r = lax.broadcasted_iota(jnp.int32, (n, n), 0) // group
    c = lax.broadcasted_iota(jnp.int32, (n, n), 1) // group
    return jnp.where(r == c, value, 0.0).astype(BF16)


def _full_spec(arr):
    nd = arr.ndim
    return pl.BlockSpec(arr.shape, lambda *_: (0,) * nd)


def _params(n_grid):
    return pltpu.CompilerParams(dimension_semantics=("arbitrary",) * n_grid, vmem_limit_bytes=VMEM_LIMIT)


def _tok_spec(tt, width):
    return pl.BlockSpec((1, tt, width), lambda g, i: (g, i, 0))


def _mod_spec(mod3, tt):
    width = mod3.shape[2]
    if mod3.shape[1] == 1:
        return pl.BlockSpec((1, 1, width), lambda g, i: (g, 0, 0))
    return pl.BlockSpec((1, tt, width), lambda g, i: (g, i, 0))


def _ada_body(c_ref, w_ref, b_ref, o_ref):
    c = c_ref[...]
    o_ref[0] = _bdot(_silu(c), w_ref[0]) + b_ref[0]


def _ada(c_all, ada_w, ada_b):
    depth, d, n6 = ada_w.shape
    bc = c_all.shape[0]
    tn = n6 // 4
    return pl.pallas_call(
        _ada_body,
        grid=(depth, n6 // tn),
        in_specs=[pl.BlockSpec((bc, d), lambda l, j: (0, 0)),
                  pl.BlockSpec((1, d, tn), lambda l, j: (l, 0, j)),
                  pl.BlockSpec((1, 1, tn), lambda l, j: (l, 0, j))],
        out_specs=pl.BlockSpec((1, bc, tn), lambda l, j: (l, 0, j)),
        out_shape=jax.ShapeDtypeStruct((depth, bc, n6), F32),
        compiler_params=_params(2),
        name="ada_mod",
    )(c_all, ada_w, ada_b.reshape(depth, 1, n6))


def _even_pre_body(seq, tt, d, aw, bw, *refs):
    n_in = 15 if seq else 19
    ins, outs = refs[:n_in], refs[n_in:]
    (x_ref, mod_ref, ng_ref, w_ref, cw_ref, cb_ref, gb_ref, mu_ref, w0_ref, a0_ref, wla_ref, g2_ref,
     kk_ref, ka_ref, gs_ref) = ins[:15]
    (q_ref, k_ref, v_ref, so_ref, g8_ref, g8t_ref, r_ref, dec_ref, kt_ref, vb_ref, al_ref, be_ref, gg_ref,
     ctail_ref, stail_ref) = outs[:15]
    a2 = 2 * aw
    sw = 3 * bw + B_LORA_W + B_LORA_A + B_LORA_G

    x = x_ref[0]
    mod = mod_ref[0]
    h = _rmsnorm(x, ng_ref[...]) * (1.0 + mod[:, d:2 * d]) + mod[:, 0:d]
    p = jnp.dot(h.astype(BF16), w_ref[...], preferred_element_type=F32)
    u = p[:, 0:a2]
    v = p[:, a2:a2 + aw]
    o = p[:, a2 + aw:a2 + 2 * aw]
    pb = p[:, 4 * aw:4 * aw + sw]
    gt = p[:, 4 * aw + sw:4 * aw + sw + LANES]

    if seq:
        ubuf, pbuf = outs[15], outs[16]

        @pl.when(pl.program_id(1) == 0)
        def _():
            ubuf[0:SUBLANES, :] = jnp.zeros((SUBLANES, a2), F32)
            pbuf[0:SUBLANES, :] = jnp.zeros((SUBLANES, sw), F32)

        ubuf[SUBLANES:SUBLANES + tt, :] = u
        pbuf[SUBLANES:SUBLANES + tt, :] = pb
        u1 = ubuf[SUBLANES - 1:SUBLANES - 1 + tt, :]
        u2 = ubuf[SUBLANES - 2:SUBLANES - 2 + tt, :]
        u3 = ubuf[SUBLANES - 3:SUBLANES - 3 + tt, :]
        pprev = pbuf[SUBLANES - 1:SUBLANES - 1 + tt, :]
        ubuf[0:SUBLANES, :] = u[tt - SUBLANES:tt, :]
        pbuf[0:SUBLANES, :] = pb[tt - SUBLANES:tt, :]
        ctail_ref[0] = u[tt - SUBLANES:tt, :]
        stail_ref[0] = pb[tt - SUBLANES:tt, :]
    else:
        u3, u2, u1, pprev = ins[15][0], ins[16][0], ins[17][0], ins[18][0]
        ctail_ref[0] = u
        stail_ref[0] = pb

    cw = cw_ref[...]
    y = cb_ref[...] + u3 * cw[0:1, :] + u2 * cw[1:2, :] + u1 * cw[2:3, :] + u * cw[3:4, :]
    qk = _silu(y)
    q_ref[0] = qk[:, 0:aw] * (A_DK ** -0.5)
    k_ref[0] = qk[:, aw:a2]
    v_ref[0] = v
    so_ref[0] = _sigmoid(o)
    g = gt + gb_ref[...]
    lane = lax.broadcasted_iota(jnp.int32, g.shape, 1)
    gates = jnp.where(lane < A_HEADS, g, -_softplus(-g))
    g8_ref[0] = gates[:, 0:2 * A_HEADS]
    g8t_ref[0] = gates.T[0:2 * A_HEADS, :]

    xb = pb + mu_ref[...] * (pprev - pb)
    r = xb[:, 0:bw]
    kb = xb[:, bw:2 * bw]
    vb = xb[:, 2 * bw:3 * bw]
    la = xb[:, 3 * bw:3 * bw + LANES]
    gl = xb[:, 3 * bw + LANES:3 * bw + 2 * LANES]
    lane2 = lax.broadcasted_iota(jnp.int32, la.shape, 1)
    la_act = jnp.where(lane2 < B_LORA_W, jnp.tanh(la), la)
    lw = jnp.dot(la_act.astype(BF16), wla_ref[...], preferred_element_type=F32)
    wlog = -_softplus(-(w0_ref[...] + lw[:, 0:bw])) - 0.5
    decay = jnp.exp(-jnp.exp(wlog))
    a = _sigmoid(a0_ref[...] + lw[:, bw:2 * bw])
    gg = jnp.dot(_sigmoid(gl).astype(BF16), g2_ref[...], preferred_element_type=F32)
    kk = kb * kk_ref[...]
    ss = _dot_split(kk * kk, gs_ref[...], 2)
    kkn = kk * lax.rsqrt(jnp.maximum(ss, 1e-24))
    r_ref[0] = r
    dec_ref[0] = decay
    kt_ref[0] = kb * (1.0 + (a - 1.0) * ka_ref[...])
    vb_ref[0] = vb
    al_ref[0] = -kkn
    be_ref[0] = kkn * a
    gg_ref[0] = gg


def _even_pre(x3, mod3, ng, wcat, cw, cb, gb, mu, w0, a0, wla, g2, k_k, k_a, prev, tt):
    g_, ttot, d = x3.shape
    aw = d // 2
    bw = d // 2
    a2 = 2 * aw
    sw = 3 * bw + B_LORA_W + B_LORA_A + B_LORA_G
    seq = prev is None
    gsum = _group_matrix(bw, B_HEAD, 1.0)
    consts = [ng, wcat, cw, cb, gb, mu, w0, a0, wla, g2, k_k, k_a, gsum]
    ins = [x3, mod3] + consts
    in_specs = [_tok_spec(tt, d), _mod_spec(mod3, tt)] + [_full_spec(c) for c in consts]
    if not seq:
        ins += list(prev)
        in_specs += [_tok_spec(tt, a2)] * 3 + [_tok_spec(tt, sw)]
    nt = ttot // tt
    tail_rows = SUBLANES if seq else tt
    tail_tot = g_ * SUBLANES if seq else ttot

    def tail_spec(width):
        if seq:
            return pl.BlockSpec((1, SUBLANES, width), lambda g, i: (g, 0, 0))
        return _tok_spec(tt, width)

    def tail_shape(width):
        if seq:
            return jax.ShapeDtypeStruct((g_, SUBLANES, width), F32)
        return jax.ShapeDtypeStruct((g_, ttot, width), F32)

    tok = lambda w: jax.ShapeDtypeStruct((g_, ttot, w), F32)
    out_shape = [tok(aw), tok(aw), tok(aw), tok(aw), tok(2 * A_HEADS),
                 jax.ShapeDtypeStruct((g_, 2 * A_HEADS, ttot), F32),
                 tok(bw), tok(bw), tok(bw), tok(bw), tok(bw), tok(bw), tok(bw),
                 tail_shape(a2), tail_shape(sw)]
    out_specs = [_tok_spec(tt, aw)] * 4 + [_tok_spec(tt, 2 * A_HEADS),
                                           pl.BlockSpec((1, 2 * A_HEADS, tt), lambda g, i: (g, 0, i))]
    out_specs += [_tok_spec(tt, bw)] * 7 + [tail_spec(a2), tail_spec(sw)]
    scratch = [pltpu.VMEM((tt + SUBLANES, a2), F32), pltpu.VMEM((tt + SUBLANES, sw), F32)] if seq else []
    del tail_rows, tail_tot
    return pl.pallas_call(
        functools.partial(_even_pre_body, seq, tt, d, aw, bw),
        grid=(g_, nt),
        in_specs=in_specs,
        out_specs=out_specs,
        out_shape=out_shape,
        scratch_shapes=scratch,
        compiler_params=_params(2),
        name="even_pre_seq" if seq else "even_pre_step",
    )(*ins)


def _mlstm_chunk_body(L, q_ref, k_ref, v_ref, g8_ref, g8t_ref, c0_ref, n0_ref, m0_ref,
                      h_ref, c1_ref, n1_ref, m1_ref, C_s, n_s, m_s):
    c = pl.program_id(1)

    @pl.when(c == 0)
    def _():
        C_s[...] = c0_ref[0]
        n_s[...] = n0_ref[0]
        m_s[...] = m0_ref[0]

    row = lax.broadcasted_iota(jnp.int32, (L, L), 0)
    col = lax.broadcasted_iota(jnp.int32, (L, L), 1)
    causal = col <= row
    tril = jnp.where(causal, 1.0, 0.0).astype(BF16)
    triu = jnp.where(row <= col, 1.0, 0.0).astype(BF16)
    g8 = g8_ref[0]
    g8t = g8t_ref[0]
    bcols = _dot_split_left(tril, g8, 3)
    brows = _dot_split(g8t, triu, 3)
    for hd in range(A_HEADS):
        sl = slice(hd * A_DK, (hd + 1) * A_DK)
        q = q_ref[0, :, sl]
        k = k_ref[0, :, sl]
        v = v_ref[0, :, sl]
        i_row = g8t[hd:hd + 1, :]
        i_col = g8[:, hd:hd + 1]
        b_row = brows[A_HEADS + hd:A_HEADS + hd + 1, :]
        b_col = bcols[:, A_HEADS + hd:A_HEADS + hd + 1]
        m_prev = m_s[hd:hd + 1, 0:1]
        C = C_s[hd]
        n = n_s[hd:hd + 1, :]
        dmat = jnp.where(causal, b_col - b_row + i_row, NEG_INF)
        inter = b_col + m_prev
        m_t = jnp.maximum(inter, jnp.max(dmat, axis=1, keepdims=True))
        w_intra = jnp.exp(dmat - m_t)
        w_inter = jnp.exp(inter - m_t)
        s = _dot_nt(q, k) * w_intra
        num = _bdot(s, v) + w_inter * _bdot(q, C)
        den = jnp.sum(s, axis=1, keepdims=True) + w_inter * jnp.sum(q * n, axis=1, keepdims=True)
        h_ref[0, :, sl] = num / jnp.maximum(jnp.abs(den), jnp.exp(-m_t))
        m_new = m_t[L - 1:L, :]
        b_last = b_col[L - 1:L, :]
        kw = k * jnp.exp(b_last - b_col + i_col - m_new)
        carry = jnp.exp(b_last + m_prev - m_new)
        C_s[hd] = carry * C + _dot_tn(kw, v)
        n_s[hd:hd + 1, :] = carry * n + jnp.sum(kw, axis=0, keepdims=True)
        m_s[hd:hd + 1, :] = jnp.broadcast_to(m_new, (1, LANES))

    @pl.when(c == pl.num_programs(1) - 1)
    def _():
        c1_ref[0] = C_s[...]
        n1_ref[0] = n_s[...]
        m1_ref[0] = m_s[...]


def _mlstm_chunk(q, k, v, g8, g8t, C0, n0p, m0p, L):
    b, t, aw = q.shape
    st = lambda *s: pl.BlockSpec((1,) + s, lambda bb, c: (bb,) + (0,) * len(s))
    return pl.pallas_call(
        functools.partial(_mlstm_chunk_body, L),
        grid=(b, t // L),
        in_specs=[_tok_spec(L, aw)] * 3 + [_tok_spec(L, 2 * A_HEADS),
                                           pl.BlockSpec((1, 2 * A_HEADS, L), lambda bb, c: (bb, 0, c)),
                                           st(A_HEADS, A_DK, A_DK), st(SUBLANES, LANES), st(SUBLANES, LANES)],
        out_specs=[_tok_spec(L, aw), st(A_HEADS, A_DK, A_DK), st(SUBLANES, LANES), st(SUBLANES, LANES)],
        out_shape=[jax.ShapeDtypeStruct((b, t, aw), F32), jax.ShapeDtypeStruct(C0.shape, F32),
                   jax.ShapeDtypeStruct(n0p.shape, F32), jax.ShapeDtypeStruct(m0p.shape, F32)],
        scratch_shapes=[pltpu.VMEM((A_HEADS, A_DK, A_DK), F32), pltpu.VMEM((SUBLANES, LANES), F32),
                        pltpu.VMEM((SUBLANES, LANES), F32)],
        compiler_params=_params(2),
        name="mlstm_chunk",
    )(q, k, v, g8, g8t, C0, n0p, m0p)


def _row8(x):
    r = lax.broadcasted_iota(jnp.int32, (SUBLANES, x.shape[1]), 0)
    return jnp.where(r == 0, jnp.broadcast_to(x, (SUBLANES, x.shape[1])), 0.0)


def _mlstm_step_body(bb, q_ref, k_ref, v_ref, g8_ref, n0_ref, m0_ref, c0_ref, *rest):
    h_ref, c1_ref, n1_ref, m1_ref = rest[-4:]
    g8 = g8_ref[...]
    m0 = m0_ref[...]
    for hd in range(A_HEADS):
        sl = slice(hd * A_DK, (hd + 1) * A_DK)
        li = g8[:, hd:hd + 1]
        lf = g8[:, A_HEADS + hd:A_HEADS + hd + 1]
        mp = m0[:, hd:hd + 1]
        m1 = jnp.maximum(lf + mp, li)
        wi = jnp.exp(li - m1)
        wf = jnp.exp(lf + mp - m1)
        floor = jnp.exp(-m1)
        for b in range(bb):
            q = q_ref[b:b + 1, sl]
            k = k_ref[b:b + 1, sl]
            v = v_ref[b:b + 1, sl]
            wib = wi[b:b + 1, :]
            wfb = wf[b:b + 1, :]
            C1 = wfb * c0_ref[0, b, hd] + wib * _dot_tn(_row8(k), _row8(v))
            n1 = wfb * n0_ref[b, hd:hd + 1, :] + wib * k
            num = _bdot(_row8(q), C1)[0:1, :]
            den = jnp.sum(q * n1, axis=1, keepdims=True)
            h_ref[b:b + 1, sl] = num / jnp.maximum(jnp.abs(den), floor[b:b + 1, :])
            c1_ref[0, b, hd] = C1
            n1_ref[b, hd:hd + 1, :] = n1
        m1_ref[:, hd:hd + 1] = m1


def _rwkv_body(bb, tb, ty, al_ref, w_ref, be_ref, kt_ref, r_ref, v_ref, s0_ref, y_ref, s1_ref, S_s, t3_s, y_s):
    @pl.when(pl.program_id(1) == 0)
    def _():
        S_s[...] = s0_ref[...]

    npair = B_HEADS // 2
    ntile = bb * npair
    nrow = min(tb, SUBLANES)
    row = lax.broadcasted_iota(jnp.int32, (B_HEAD, LANES), 0)
    lane = lax.broadcasted_iota(jnp.int32, (B_HEAD, LANES), 1)
    eye_pair = row == (lane % B_HEAD)
    wred = _group_matrix(2 * LANES, B_HEAD, 1.0)
    kk = lax.broadcasted_iota(jnp.int32, (nrow * LANES, LANES), 0)
    ll = lax.broadcasted_iota(jnp.int32, (nrow * LANES, LANES), 1)
    sel_head = ((kk % LANES) // B_HEAD) == (ll // B_HEAD)
    sel_off = (ll % B_HEAD) - (kk // LANES)

    def group(t8, carry):
        base = t8 * nrow if isinstance(t8, int) else pl.multiple_of(t8 * nrow, nrow)
        rows = pl.ds(base, nrow)
        for i in range(nrow):
            ri = slice(i, i + 1)
            lhs = []
            for b in range(bb):
                for p in range(npair):
                    sl = pl.ds(p * LANES, LANES)
                    lhs.append(jnp.concatenate(
                        [(S_s[b, p] * al_ref[b, rows, sl][ri]).astype(BF16),
                         jnp.where(eye_pair, v_ref[b, rows, sl][ri], 0.0).astype(BF16)], axis=1))
            red = jnp.dot(jnp.concatenate(lhs, axis=0), wred, preferred_element_type=F32)
            for b in range(bb):
                for p in range(npair):
                    sl = pl.ds(p * LANES, LANES)
                    idx = b * npair + p
                    sa = red[idx * B_HEAD:(idx + 1) * B_HEAD, 0:LANES]
                    vcol = red[idx * B_HEAD:(idx + 1) * B_HEAD, LANES:2 * LANES]
                    S = (S_s[b, p] * w_ref[b, rows, sl][ri] + sa * be_ref[b, rows, sl][ri]
                         + vcol * kt_ref[b, rows, sl][ri])
                    S_s[b, p] = S
                    t3_s[idx * B_HEAD:(idx + 1) * B_HEAD, i * LANES:(i + 1) * LANES] = (
                        S * r_ref[b, rows, sl][ri]).astype(BF16)
        wsel = jnp.where(sel_head & (sel_off == base % ty), 1.0, 0.0).astype(BF16)
        yg = jnp.dot(t3_s[...], wsel, preferred_element_type=F32)

        @pl.when(base % ty == 0)
        def _():
            y_s[...] = yg

        @pl.when(base % ty != 0)
        def _():
            y_s[...] += yg

        @pl.when((base + nrow) % ty == 0)
        def _():
            start = (base // ty) * ty
            trow = pl.ds(start, ty) if isinstance(start, int) else pl.ds(pl.multiple_of(start, ty), ty)
            half = npair * B_HEAD
            for b in range(bb):
                yt = y_s[b * half:(b + 1) * half, :].T
                y_ref[b, trow, 0:half] = yt[0:ty, :]
                y_ref[b, trow, half:2 * half] = yt[B_HEAD:B_HEAD + ty, :]
        return carry

    if tb == nrow:
        group(0, 0)
    else:
        lax.fori_loop(0, tb // nrow, group, 0)

    @pl.when(pl.program_id(1) == pl.num_programs(1) - 1)
    def _():
        s1_ref[...] = S_s[...]


def _rwkv_scan(al, w, be, kt, r, v, S0p, bb, tb):
    b, t, bw = al.shape
    npair = B_HEADS // 2
    ty = min(B_HEAD, t)
    nb = b // bb
    tok = pl.BlockSpec((bb, tb, bw), lambda i, j: (i, j, 0))
    st = pl.BlockSpec((bb, npair, B_HEAD, LANES), lambda i, j: (i, 0, 0, 0))
    nrow = min(tb, SUBLANES)
    rows = bb * npair * B_HEAD
    return pl.pallas_call(
        functools.partial(_rwkv_body, bb, tb, ty),
        grid=(nb, t // tb),
        in_specs=[tok] * 6 + [st],
        out_specs=[tok, st],
        out_shape=[jax.ShapeDtypeStruct((b, t, bw), F32), jax.ShapeDtypeStruct(S0p.shape, F32)],
        scratch_shapes=[pltpu.VMEM((bb, npair, B_HEAD, LANES), F32), pltpu.VMEM((rows, nrow * LANES), BF16),
                        pltpu.VMEM((rows, LANES), F32)],
        compiler_params=_params(2),
        name="rwkv_scan",
    )(al, w, be, kt, r, v, S0p)


PAIRED_HEADS = tuple(q + (B_HEADS // 2) * hh for q in range(B_HEADS // 2) for hh in range(2))


def _paired_channels(offset=0):
    return np.concatenate([np.arange(h * B_HEAD, (h + 1) * B_HEAD) for h in PAIRED_HEADS]) + offset


def _paired_to_natural(x):
    pos = {h: i for i, h in enumerate(PAIRED_HEADS)}
    return jnp.concatenate([x[:, pos[h] * B_HEAD:(pos[h] + 1) * B_HEAD] for h in range(B_HEADS)], axis=1)


def _even_post_body(d, aw, bw, x_ref, mod_ref, ha_ref, so_ref, y_ref, r_ref, kt_ref, vb_ref, gg_ref,
                    ang_ref, lng_ref, lnb_ref, rk_ref, wo_ref, o_ref):
    ga = _group_matrix(aw, A_DK, 1.0 / A_DK)
    gb_mean = _group_matrix(bw, B_HEAD, 1.0 / B_HEAD)
    gb_sum = _group_matrix(bw, B_HEAD, 1.0)
    r, kt, vb, gg = (_paired_to_natural(ref[0]) for ref in (r_ref, kt_ref, vb_ref, gg_ref))
    ha = ha_ref[0]
    ha = ha * lax.rsqrt(_dot_split(ha * ha, ga, 2) + EPS) * ang_ref[...] * so_ref[0]
    y = y_ref[0]
    yc = y - _dot_split(y, gb_mean, 2)
    var = _dot_split(yc * yc, gb_mean, 2)
    yn = yc * lax.rsqrt(var + B_LN_EPS) * lng_ref[...] + lnb_ref[...]
    bonus = _dot_split(r * kt * rk_ref[...], gb_sum, 2)
    y2 = (yn + bonus * vb) * gg
    out = (jnp.dot(ha.astype(BF16), wo_ref[0:aw, :], preferred_element_type=F32)
           + jnp.dot(y2.astype(BF16), wo_ref[aw:aw + bw, :], preferred_element_type=F32))
    o_ref[0] = x_ref[0] + mod_ref[0][:, 2 * d:3 * d] * out


def _even_post(x3, mod3, ha, so, y, r, kt, vb, gg, ang, lng, lnb, rk, wo, tt):
    g_, ttot, d = x3.shape
    aw = d // 2
    bw = d // 2
    consts = [ang, lng, lnb, rk, wo]
    return pl.pallas_call(
        functools.partial(_even_post_body, d, aw, bw),
        grid=(g_, ttot // tt),
        in_specs=[_tok_spec(tt, d), _mod_spec(mod3, tt)] + [_tok_spec(tt, aw)] * 7 + [_full_spec(c) for c in consts],
        out_specs=_tok_spec(tt, d),
        out_shape=jax.ShapeDtypeStruct(x3.shape, F32),
        compiler_params=_params(2),
        name="even_post",
    )(x3, mod3, ha, so, y, r, kt, vb, gg, *consts)


def _odd_pre_body(j, d, x_ref, mod_ref, ng_ref, w_ref, lbl_ref, q_ref, k_ref, v_ref, lf_ref, sg_ref):
    x = x_ref[0]
    mod = mod_ref[0]
    h = _rmsnorm(x, ng_ref[...]) * (1.0 + mod[:, d:2 * d]) + mod[:, 0:d]
    p = jnp.dot(h.astype(BF16), w_ref[...], preferred_element_type=F32)
    lbl = lbl_ref[...]
    n_odd = lbl.shape[0]
    mx = lbl[0:1, :]
    for i in range(1, n_odd):
        mx = jnp.maximum(mx, lbl[i:i + 1, :])
    ex = [jnp.exp(lbl[i:i + 1, :] - mx) for i in range(n_odd)]
    tot = ex[0]
    for i in range(1, n_odd):
        tot = tot + ex[i]
    lb = jnp.zeros_like(mx)
    for i in range(1, j + 1):
        lb = lb + ex[i] / tot
    f = lb + (1.0 - lb) * _sigmoid(p[:, d:2 * d])
    q_ref[0] = _silu(p[:, 0:d])
    k_ref[0] = 1.0 - f
    v_ref[0] = p[:, 2 * d:3 * d]
    lf_ref[0] = jnp.log(f)
    sg_ref[0] = _silu(p[:, 3 * d:4 * d])


def _odd_pre(x3, mod3, ng, w_in, lbl, j, tt):
    g_, ttot, d = x3.shape
    consts = [ng, w_in, lbl]
    return pl.pallas_call(
        functools.partial(_odd_pre_body, j, d),
        grid=(g_, ttot // tt),
        in_specs=[_tok_spec(tt, d), _mod_spec(mod3, tt)] + [_full_spec(c) for c in consts],
        out_specs=[_tok_spec(tt, d)] * 5,
        out_shape=[jax.ShapeDtypeStruct(x3.shape, F32)] * 5,
        compiler_params=_params(2),
        name="odd_pre",
    )(x3, mod3, *consts)


def _col_bcast(row, terms):
    parts = _split(row, terms)
    r = lax.broadcasted_iota(jnp.int32, (SUBLANES, row.shape[1]), 0)
    lhs = jnp.zeros((SUBLANES, row.shape[1]), F32)
    for i, p in enumerate(parts):
        lhs = jnp.where(r == i, jnp.broadcast_to(p.astype(F32), lhs.shape), lhs)
    ones = jnp.where(lax.broadcasted_iota(jnp.int32, (SUBLANES, LANES), 0) < terms, 1.0, 0.0).astype(BF16)
    return lax.dot_general(lhs.astype(BF16), ones, (((0,), (0,)), ((), ())), preferred_element_type=F32)


def _hgrn_chunk_body(L, sub, nchunk, hg, q_ref, k_ref, v_ref, g_ref, s0_ref, o_ref, s1_ref, S_s):
    @pl.when(pl.program_id(2) == 0)
    def _():
        S_s[...] = s0_ref[0]

    nj = L // sub
    rowi = lax.broadcasted_iota(jnp.int32, (L, C_DK), 0)
    rr = lax.broadcasted_iota(jnp.int32, (L, L), 0)
    cc = lax.broadcasted_iota(jnp.int32, (L, L), 1)
    tril = jnp.where(cc <= rr, 1.0, 0.0).astype(BF16)
    ones = jnp.ones((C_DK, LANES), BF16)

    def chunk(ci, carry):
        base = pl.multiple_of(ci * L, L)
        rows = pl.ds(base, L)
        b_all = _dot_split_left(tril, g_ref[0, rows, :], 3)
        heads = []
        for hd in range(hg):
            sl = slice(hd * C_DK, (hd + 1) * C_DK)
            heads.append((q_ref[0, rows, sl], k_ref[0, rows, sl], v_ref[0, rows, sl], b_all[:, sl], S_s[hd]))
        outs = [_bdot(q * jnp.exp(b), S) for (q, k, v, b, S) in heads]
        for dlt in range(sub):
            prs, vss = [], []
            for (q, k, v, b, S) in heads:
                if dlt == 0:
                    prs.append((q * k).astype(BF16))
                    vss.append(v)
                else:
                    ok = (rowi % sub) >= dlt
                    e = jnp.exp(jnp.where(ok, b - pltpu.roll(b, dlt, 0), NEG_INF))
                    prs.append((q * pltpu.roll(k, dlt, 0) * e).astype(BF16))
                    vss.append(pltpu.roll(v, dlt, 0))
            rs = jnp.dot(jnp.concatenate(prs, axis=0), ones, preferred_element_type=F32)
            outs = [o + rs[i * L:(i + 1) * L, :] * vs for i, (o, vs) in enumerate(zip(outs, vss))]
        for hd, ((q, k, v, b, S), o) in enumerate(zip(heads, outs)):
            if nj > 1:
                amat = None
                for jj in range(nj - 1):
                    end = (jj + 1) * sub
                    bref = b[end - 1:end, :]
                    qj = q * jnp.exp(jnp.where(rowi >= end, b - bref, NEG_INF))
                    kj = k * jnp.exp(jnp.where((rowi >= end - sub) & (rowi < end), bref - b, NEG_INF))
                    a = _dot_nt(qj, kj)
                    amat = a if amat is None else amat + a
                o = o + _bdot(amat, v)
            o_ref[0, rows, hd * C_DK:(hd + 1) * C_DK] = o
            b_last = b[L - 1:L, :]
            S_s[hd] = _col_bcast(jnp.exp(b_last), 3) * S + _dot_tn(k * jnp.exp(b_last - b), v)
        return carry

    lax.fori_loop(0, nchunk, chunk, 0)

    @pl.when(pl.program_id(2) == pl.num_programs(2) - 1)
    def _():
        s1_ref[0] = S_s[...]


def _hgrn_chunk(q, k, v, lf, S0, tblk, L, sub, hg):
    b, t, d = q.shape
    tok = pl.BlockSpec((1, tblk, hg * C_DK), lambda bb, h, c: (bb, c, h))
    st = pl.BlockSpec((1, hg, C_DK, C_DK), lambda bb, h, c: (bb, h, 0, 0))
    return pl.pallas_call(
        functools.partial(_hgrn_chunk_body, L, sub, tblk // L, hg),
        grid=(b, C_HEADS // hg, t // tblk),
        in_specs=[tok] * 4 + [st],
        out_specs=[tok, st],
        out_shape=[jax.ShapeDtypeStruct((b, t, d), F32), jax.ShapeDtypeStruct(S0.shape, F32)],
        scratch_shapes=[pltpu.VMEM((hg, C_DK, C_DK), F32)],
        compiler_params=_params(3),
        name="hgrn_chunk",
    )(q, k, v, lf, S0)


def _hgrn_step_body(bb, q_ref, k_ref, v_ref, g_ref, s0_ref, *rest):
    o_ref, s1_ref = rest[-2], rest[-1]
    for b in range(bb):
        for hd in range(C_HEADS):
            sl = slice(hd * C_DK, (hd + 1) * C_DK)
            q = q_ref[b:b + 1, sl]
            k = k_ref[b:b + 1, sl]
            v = v_ref[b:b + 1, sl]
            f = jnp.exp(g_ref[b:b + 1, sl])
            S1 = _col_bcast(f, 3) * s0_ref[0, b, hd] + _dot_tn(_row8(k), _row8(v))
            o_ref[b:b + 1, sl] = _bdot(_row8(q), S1)[0:1, :]
            s1_ref[0, b, hd] = S1


def _layer_state_call(body, name, rows_in, row_out_shape, row_spec, state_all, layer, state_acc, bb, extra_out=()):
    b = state_all.shape[1]
    blk = (1, bb) + state_all.shape[2:]
    zeros = (0,) * (state_all.ndim - 2)
    st = pl.BlockSpec(blk, lambda i: (layer, i) + zeros)
    ins = list(rows_in) + [state_all]
    in_specs = [row_spec(r) for r in rows_in] + [st]
    aliases = {}
    if state_acc is not None:
        ins.append(state_acc)
        in_specs.append(pl.BlockSpec(memory_space=pl.ANY))
        aliases = {len(ins) - 1: 1}
    return pl.pallas_call(
        body,
        grid=(b // bb,),
        in_specs=in_specs,
        out_specs=[row_spec(row_out_shape), st] + [row_spec(e) for e in extra_out],
        out_shape=[jax.ShapeDtypeStruct(row_out_shape.shape, F32), jax.ShapeDtypeStruct(state_all.shape, F32)]
        + [jax.ShapeDtypeStruct(e.shape, F32) for e in extra_out],
        input_output_aliases=aliases,
        compiler_params=_params(1),
        name=name,
    )(*ins)


def _row_spec_for(bb):
    def spec(arr):
        blk = (bb,) + arr.shape[1:]
        zeros = (0,) * (arr.ndim - 1)
        return pl.BlockSpec(blk, lambda i: (i,) + zeros)
    return spec


def _mlstm_step(q, k, v, g8, n0, m0, C_all, layer, C_acc, bb):
    return _layer_state_call(functools.partial(_mlstm_step_body, bb), "mlstm_step", [q, k, v, g8, n0, m0], q,
                             _row_spec_for(bb), C_all, layer, C_acc, bb, extra_out=(n0, m0))


def _hgrn_step(q, k, v, lf, S_all, layer, S_acc, bb):
    return _layer_state_call(functools.partial(_hgrn_step_body, bb), "hgrn_step", [q, k, v, lf], q,
                             _row_spec_for(bb), S_all, layer, S_acc, bb)


def _odd_post_body(d, x_ref, mod_ref, o_ref, sg_ref, ng_ref, wo_ref, out_ref):
    gm = _group_matrix(d, C_DK, 1.0 / C_DK)
    o = o_ref[0]
    o = o * lax.rsqrt(_dot_split(o * o, gm, 2) + EPS) * ng_ref[...] * sg_ref[0]
    out = jnp.dot(o.astype(BF16), wo_ref[...], preferred_element_type=F32)
    out_ref[0] = x_ref[0] + mod_ref[0][:, 2 * d:3 * d] * out


def _odd_post(x3, mod3, o, sg, ng_tiled, wo, tt):
    g_, ttot, d = x3.shape
    consts = [ng_tiled, wo]
    return pl.pallas_call(
        functools.partial(_odd_post_body, d),
        grid=(g_, ttot // tt),
        in_specs=[_tok_spec(tt, d), _mod_spec(mod3, tt)] + [_tok_spec(tt, d)] * 2 + [_full_spec(c) for c in consts],
        out_specs=_tok_spec(tt, d),
        out_shape=jax.ShapeDtypeStruct(x3.shape, F32),
        compiler_params=_params(2),
        name="odd_post",
    )(x3, mod3, o, sg, *consts)


P_NCAND = P_TOPK + 1
P_CAND_ROWS = 3 * SUBLANES


def _oddeven_merge(lo, hi, r):
    step = r * 2
    if step < hi - lo:
        yield from _oddeven_merge(lo, hi, step)
        yield from _oddeven_merge(lo + r, hi, step)
        yield from [(i, i + r) for i in range(lo + r, hi - r, step)]
    else:
        yield (lo, lo + r)


def _oddeven_merge_sort(lo, hi):
    if hi - lo >= 1:
        mid = lo + (hi - lo) // 2
        yield from _oddeven_merge_sort(lo, mid)
        yield from _oddeven_merge_sort(mid + 1, hi)
        yield from _oddeven_merge(lo, hi, 1)


def _best_of_tiles(tiles):
    nv = P_TOPK
    v = list(tiles) + [jnp.full(tiles[0].shape, NEG_INF, F32)] * (nv - len(tiles))
    for i, j in _oddeven_merge_sort(0, nv - 1):
        v[i], v[j] = jnp.maximum(v[i], v[j]), jnp.minimum(v[i], v[j])
    shift = SUBLANES // 2
    while shift >= 1:
        other = [pltpu.roll(x, shift, 0) for x in v]
        v = [jnp.maximum(v[r], other[nv - 1 - r]) for r in range(nv)]
        dist = nv // 2
        while dist >= 1:
            for r in range(nv):
                if r & dist == 0:
                    v[r], v[r + dist] = jnp.maximum(v[r], v[r + dist]), jnp.minimum(v[r], v[r + dist])
            dist //= 2
        shift //= 2
    rest = None
    for x in tiles:
        m = jnp.where(x < v[nv - 1], x, NEG_INF)
        rest = m if rest is None else jnp.maximum(rest, m)
    return v, jnp.max(rest, axis=0, keepdims=True)


def _top_sorted(s):
    nv = s.shape[0] // SUBLANES
    assert nv == P_TOPK
    v, nxt = _best_of_tiles([s[r * SUBLANES:(r + 1) * SUBLANES, :] for r in range(nv)])
    r8 = lax.broadcasted_iota(jnp.int32, (SUBLANES, s.shape[1]), 0)
    groups = []
    for g in range(nv // SUBLANES):
        acc = v[g * SUBLANES + SUBLANES - 1]
        for r in range(SUBLANES - 1):
            acc = jnp.where(r8 == r, v[g * SUBLANES + r], acc)
        groups.append(acc)
    groups.append(jnp.where(r8 == 0, nxt, NEG_INF))
    return jnp.concatenate(groups, axis=0)


def _peer_body(tm, te, ne, d, x_ref, mod_ref, ng_ref, wqt_ref, keys_ref, u_ref, vt_ref, vl_ref, o_ref,
               ht_s, s1_s, e1_s, thr_s, e2_s, gate_s, w_s, acc_s):
    j = pl.program_id(2)
    dk2 = keys_ref.shape[4]
    nsub = te // P_NKEYS

    @pl.when(j == 0)
    def _():
        x = x_ref[0]
        mod = mod_ref[0]
        h = _rmsnorm(x, ng_ref[...]) * (1.0 + mod[:, 4 * d:5 * d]) + mod[:, 3 * d:4 * d]
        ht = h.T.astype(BF16)
        ht_s[...] = ht
        qt = jnp.dot(wqt_ref[0], ht, preferred_element_type=F32)
        r8 = lax.broadcasted_iota(jnp.int32, (SUBLANES, LANES), 0)
        for hd in range(P_HEADS):
            s1_all = _bdot(keys_ref[0, hd, 0], qt[(2 * hd) * dk2:(2 * hd + 1) * dk2, :])
            s2_all = _bdot(keys_ref[0, hd, 1], qt[(2 * hd + 1) * dk2:(2 * hd + 2) * dk2, :])
            for tc in range(tm // LANES):
                ls = slice(tc * LANES, (tc + 1) * LANES)
                s1 = s1_all[:, ls]
                s2 = s2_all[:, ls]
                a = _top_sorted(s1)
                b = _top_sorted(s2)
                blocks = [a[0:1, :] + b]
                for i in range(2, SUBLANES + 1):
                    blocks.append(jnp.where(r8 < P_NCAND // i, a[i - 1:i, :] + b[0:SUBLANES, :], NEG_INF))
                blocks.append(a[SUBLANES:P_CAND_ROWS, :] + b[0:1, :])
                cand = jnp.concatenate(blocks, axis=0)
                ranked, nxt = _best_of_tiles([cand[r:r + SUBLANES, :] for r in range(0, cand.shape[0], SUBLANES)])
                tau = 0.5 * (ranked[P_TOPK - 1][0:1, :] + nxt)
                top = a[0:1, :] + b[0:1, :]
                z = jnp.sum(jnp.where(cand >= tau, jnp.exp(cand - top), 0.0), axis=0, keepdims=True)
                s1_s[hd, :, ls] = s1
                e1_s[hd, :, ls] = jnp.exp(s1 - a[0:1, :])
                thr_s[hd, :, ls] = tau - s2
                e2_s[hd, :, ls] = jnp.exp(s2 - b[0:1, :]) / z
        acc_s[...] = jnp.zeros_like(acc_s)
        w_s[1] = jnp.zeros((te, tm), BF16)

    assert nsub == SUBLANES
    grp = pl.ds(pl.multiple_of(j * nsub, nsub), nsub)

    def gates(tc):
        ls = slice(tc * LANES, (tc + 1) * LANES)
        s1g = [s1_s[hd, grp, ls] for hd in range(P_HEADS)]
        e1g = [e1_s[hd, grp, ls] for hd in range(P_HEADS)]
        for ii in range(nsub):
            rows = slice(ii * P_NKEYS, (ii + 1) * P_NKEYS)
            gate = jnp.zeros((P_NKEYS, LANES), F32)
            for hd in range(P_HEADS):
                keep = s1g[hd][ii:ii + 1, :] >= thr_s[hd, :, ls]
                gate = gate + jnp.where(keep, e2_s[hd, :, ls], 0.0) * e1g[hd][ii:ii + 1, :]
            gate_s[rows, ls] = gate

    piece = min(tm, MXU_WIDTH)
    tc_per_piece = piece // LANES
    for pc in range(tm // piece):
        cs = slice(pc * piece, (pc + 1) * piece)
        acc_s[:, cs] += jnp.dot(vt_ref[0], w_s[(j + 1) % 2, :, cs], preferred_element_type=F32)
        for tc in range(pc * tc_per_piece, pc * tc_per_piece + (tc_per_piece + 1) // 2):
            gates(tc)
        act = jnp.dot(u_ref[0], ht_s[:, cs], preferred_element_type=F32)
        for tc in range(pc * tc_per_piece + (tc_per_piece + 1) // 2, (pc + 1) * tc_per_piece):
            gates(tc)
        w_s[j % 2, :, cs] = (gate_s[:, cs] * _gelu_tanh(act)).astype(BF16)

    @pl.when(j == ne - 1)
    def _():
        acc = acc_s[...] + jnp.dot(vl_ref[0], w_s[(ne - 1) % 2], preferred_element_type=F32)
        o_ref[0] = x_ref[0] + mod_ref[0][:, 5 * d:6 * d] * acc.T


def _peer(x3, mod3, ng, wqt_all, keys_all, u_all, vt_all, layer, tm, te):
    g_, ttot, d = x3.shape
    ne = u_all.shape[1] // te
    nk = P_NKEYS
    if mod3.shape[1] == 1:
        mod_spec = pl.BlockSpec((1, 1, mod3.shape[2]), lambda g, i, j: (g, 0, 0))
    else:
        mod_spec = pl.BlockSpec((1, tm, mod3.shape[2]), lambda g, i, j: (g, i, 0))
    return pl.pallas_call(
        functools.partial(_peer_body, tm, te, ne, d),
        grid=(g_, ttot // tm, ne),
        in_specs=[pl.BlockSpec((1, tm, d), lambda g, i, j: (g, i, 0)), mod_spec,
                  pl.BlockSpec(ng.shape, lambda g, i, j: (0, 0)),
                  pl.BlockSpec((1,) + wqt_all.shape[1:], lambda g, i, j: (layer, 0, 0)),
                  pl.BlockSpec((1,) + keys_all.shape[1:], lambda g, i, j: (layer, 0, 0, 0, 0)),
                  pl.BlockSpec((1, te, d), lambda g, i, j: (layer, j, 0)),
                  pl.BlockSpec((1, d, te), lambda g, i, j: (layer, 0, jnp.maximum(j - 1, 0))),
                  pl.BlockSpec((1, d, te), lambda g, i, j: (layer, 0, ne - 1))],
        out_specs=pl.BlockSpec((1, tm, d), lambda g, i, j: (g, i, 0)),
        out_shape=jax.ShapeDtypeStruct(x3.shape, F32),
        scratch_shapes=[pltpu.VMEM((d, tm), BF16),
                        pltpu.VMEM((P_HEADS, nk, tm), F32), pltpu.VMEM((P_HEADS, nk, tm), F32),
                        pltpu.VMEM((P_HEADS, nk, tm), F32), pltpu.VMEM((P_HEADS, nk, tm), F32),
                        pltpu.VMEM((te, tm), F32), pltpu.VMEM((2, te, tm), BF16), pltpu.VMEM((d, tm), F32)],
        compiler_params=_params(3),
        name="peer",
    )(x3, mod3, ng, wqt_all, keys_all, u_all, vt_all, vt_all)


def _final_norm_body(x_ref, g_ref, o_ref):
    o_ref[0] = _rmsnorm(x_ref[0], g_ref[...])


def _final_norm(x3, g, tt):
    g_, ttot, d = x3.shape
    return pl.pallas_call(
        _final_norm_body,
        grid=(g_, ttot // tt),
        in_specs=[_tok_spec(tt, d), _full_spec(g)],
        out_specs=_tok_spec(tt, d),
        out_shape=jax.ShapeDtypeStruct(x3.shape, F32),
        compiler_params=_params(2),
        name="final_norm",
    )(x3, g)


def _pair_state(S):
    b = S.shape[0]
    return S.reshape(b, 2, B_HEADS // 2, B_HEAD, B_HEAD).transpose(0, 2, 3, 1, 4).reshape(b, B_HEADS // 2, B_HEAD, LANES)


def _unpair_state(Sp):
    b = Sp.shape[0]
    return Sp.reshape(b, B_HEADS // 2, B_HEAD, 2, B_HEAD).transpose(0, 3, 1, 2, 4).reshape(b, B_HEADS, B_HEAD, B_HEAD)


def _shift_perm(bw):
    return np.concatenate([_paired_channels(0), _paired_channels(bw), _paired_channels(2 * bw),
                           np.arange(3 * bw, 3 * bw + B_LORA_W + B_LORA_A + B_LORA_G)])


def _trunk(x, mod_all, st, w, seq):
    bsz, t, d = x.shape
    depth = mod_all.shape[0]
    aw = d // 2
    bw = d // 2
    if seq:
        x3 = x
        tt = min(256, t)
        tm = min(512, t)
        mods = [mod_all[l][:, None, :] for l in range(depth)]
    else:
        x3 = x.reshape(1, bsz * t, d)
        tt = bsz * t
        tm = bsz * t
        mods = [mod_all[l][None] for l in range(depth)]
    row = lambda v_: v_.reshape(1, -1)
    new_even, new_odd = [], []
    sp = _shift_perm(bw)
    sp_inv = np.argsort(sp)
    C_acc = S_acc = None
    for l in range(depth):
        j = l // 2
        if l % 2 == 0:
            e = w["even"][j]
            if seq:
                prev = None
            else:
                conv0, shift0 = st[3][j], st[5][j]
                prev = (conv0[:, 0][None], conv0[:, 1][None], conv0[:, 2][None], shift0[:, 0][None][..., sp])
            (q, k, v, so, g8, g8t, r, dec, kt, vb, al, be, gg, ctail, stail) = _even_pre(
                x3, mods[l], row(w["norm_mix_g"][l]), e["wcat"], e["conv_w"], row(e["conv_b"]), e["gate_b"],
                row(e["mu"]), row(e["w0"]), row(e["a0"]), e["wla"], e["g2"], row(e["k_k"]), row(e["k_a"]), prev, tt)
            if seq:
                C0 = jnp.zeros((bsz, A_HEADS, A_DK, A_DK), F32)
                nm0 = jnp.zeros((bsz, SUBLANES, LANES), F32)
                ha, C1, n1p, m1p = _mlstm_chunk(q, k, v, g8, g8t, C0, nm0, nm0, min(256, t))
                n1 = n1p[:, :A_HEADS, :]
                m1 = m1p[:, :A_HEADS, 0]
                S0p = jnp.zeros((bsz, B_HEADS // 2, B_HEAD, LANES), F32)
                y, S1p = _rwkv_scan(al, dec, be, kt, r, vb, S0p, min(8, bsz), min(128, t))
                conv1 = ctail[:, SUBLANES - (A_CONV - 1):, :]
                shift1 = stail[:, SUBLANES - 1:, :][..., sp_inv]
            else:
                ha2, C_acc, n1, m1 = _mlstm_step(q[0], k[0], v[0], g8[0], st[1][j], st[2][j], st[0], j, C_acc, 8)
                C1 = None
                ha = ha2[None]
                tok = lambda a_: a_[0][:, None, :]
                y2, S1p = _rwkv_scan(tok(al), tok(dec), tok(be), tok(kt), tok(r), tok(vb), _pair_state(st[4][j]), 8, 1)
                y = y2.reshape(1, bsz, bw)
                conv1 = jnp.concatenate([st[3][j][:, 1:], ctail[0][:, None, :]], axis=1)
                shift1 = stail[0][:, None, :][..., sp_inv]
            x3 = _even_post(x3, mods[l], ha, so, y, r, kt, vb, gg, row(e["a_norm_g"]), row(e["ln_g"]),
                            row(e["ln_b"]), row(e["r_k"]), e["w_out"], tt)
            new_even.append((C1, n1, m1, conv1, _unpair_state(S1p), shift1))
        else:
            o_ = w["odd"][j]
            q, k, v, lf, sg = _odd_pre(x3, mods[l], row(w["norm_mix_g"][l]), o_["w_in"], w["lb_logits"], j, tt)
            if seq:
                S0 = jnp.zeros((bsz, C_HEADS, C_DK, C_DK), F32)
                o, S1 = _hgrn_chunk(q, k, v, lf, S0, min(256, t), min(64, t), min(16, t), 8)
            else:
                o2, S_acc = _hgrn_step(q[0], k[0], v[0], lf[0], st[6], j, S_acc, 8)
                S1 = None
                o = o2[None]
            x3 = _odd_post(x3, mods[l], o, sg, row(jnp.tile(o_["norm_g"], C_HEADS)), o_["w_out"], tt)
            new_odd.append(S1)
        pw = w["peer"]
        x3 = _peer(x3, mods[l], row(w["norm_ffn_g"][l]), pw["wqt"], pw["keys"], pw["u"], pw["vt"], l, tm, 1024)
    y = _final_norm(x3, row(w["norm_final_g"]), tt).reshape(bsz, t, d)
    ev = [None if new_even[0][i] is None else jnp.stack([s[i] for s in new_even]) for i in range(6)]
    if seq:
        return (y, ev[0], ev[1], ev[2], ev[3], ev[4], ev[5], jnp.stack(new_odd))
    return (y, C_acc, ev[1], ev[2], ev[3], ev[4], ev[5], S_acc)


def _prepare_weights(norm_mix_g, norm_ffn_g, norm_final_g, even_w_in, even_w_out, mlstm_conv_w, mlstm_conv_b,
                     mlstm_gate_b, mlstm_norm_g, rwkv_mu, rwkv_w0, rwkv_w2, rwkv_a0, rwkv_a2, rwkv_g2, rwkv_k_k,
                     rwkv_k_a, rwkv_r_k, rwkv_ln_g, rwkv_ln_b, odd_w_in, odd_w_out, hgrn_lb_logits, hgrn_norm_g,
                     peer_w_q, peer_keys, peer_u, peer_v):
    d = even_w_in.shape[1]
    aw = d // 2
    bw = d // 2
    sw = 3 * bw + B_LORA_W + B_LORA_A + B_LORA_G
    even = []
    for j in range(even_w_in.shape[0]):
        wi = even_w_in[j]
        gates_w = wi[:, 4 * aw:4 * aw + 2 * A_HEADS]
        pc = _paired_channels()
        sp = _shift_perm(bw)
        wcat = jnp.concatenate([wi[:, 0:4 * aw], wi[:, 4 * aw + 2 * A_HEADS:][:, sp],
                                jnp.pad(gates_w, ((0, 0), (0, LANES - 2 * A_HEADS)))], axis=1).astype(BF16)
        assert wcat.shape[1] == 4 * aw + sw + LANES
        wla = jnp.zeros((B_LORA_W + B_LORA_A, 2 * bw), F32)
        wla = wla.at[:B_LORA_W, :bw].set(rwkv_w2[j][:, pc]).at[B_LORA_W:, bw:].set(rwkv_a2[j][:, pc]).astype(BF16)
        even.append(dict(
            wcat=wcat, conv_w=mlstm_conv_w[j], conv_b=mlstm_conv_b[j],
            gate_b=jnp.pad(mlstm_gate_b[j], (0, LANES - 2 * A_HEADS)).reshape(1, LANES),
            mu=rwkv_mu[j][sp], w0=rwkv_w0[j][pc], a0=rwkv_a0[j][pc], wla=wla, g2=rwkv_g2[j][:, pc].astype(BF16),
            k_k=rwkv_k_k[j][pc], k_a=rwkv_k_a[j][pc], r_k=rwkv_r_k[j], ln_g=rwkv_ln_g[j], ln_b=rwkv_ln_b[j],
            a_norm_g=mlstm_norm_g[j], w_out=even_w_out[j].astype(BF16)))
    odd = [dict(w_in=odd_w_in[j].astype(BF16), w_out=odd_w_out[j].astype(BF16), norm_g=hgrn_norm_g[j])
           for j in range(odd_w_in.shape[0])]
    peer = dict(wqt=jnp.swapaxes(peer_w_q, 1, 2).astype(BF16), keys=peer_keys.astype(BF16), u=peer_u.astype(BF16),
                vt=jnp.swapaxes(peer_v, 1, 2).astype(BF16))
    return dict(norm_mix_g=norm_mix_g, norm_ffn_g=norm_ffn_g, norm_final_g=norm_final_g, even=even, odd=odd,
                peer=peer, lb_logits=hgrn_lb_logits)


def kernel(x_prompt, x_sample, c_prompt, c_sample, state_mlstm_C, state_mlstm_n, state_mlstm_m, state_mlstm_conv, state_rwkv_S, state_rwkv_shift, state_hgrn_S, norm_mix_g, norm_ffn_g, norm_final_g, ada_w, ada_b, even_w_in, even_w_out, mlstm_conv_w, mlstm_conv_b, mlstm_gate_b, mlstm_norm_g, rwkv_mu, rwkv_w0, rwkv_w2, rwkv_a0, rwkv_a2, rwkv_g2, rwkv_k_k, rwkv_k_a, rwkv_r_k, rwkv_ln_g, rwkv_ln_b, odd_w_in, odd_w_out, hgrn_lb_logits, hgrn_norm_g, peer_w_q, peer_keys, peer_u, peer_v):
    w = _prepare_weights(norm_mix_g, norm_ffn_g, norm_final_g, even_w_in, even_w_out, mlstm_conv_w, mlstm_conv_b,
                         mlstm_gate_b, mlstm_norm_g, rwkv_mu, rwkv_w0, rwkv_w2, rwkv_a0, rwkv_a2, rwkv_g2, rwkv_k_k,
                         rwkv_k_a, rwkv_r_k, rwkv_ln_g, rwkv_ln_b, odd_w_in, odd_w_out, hgrn_lb_logits, hgrn_norm_g,
                         peer_w_q, peer_keys, peer_u, peer_v)
    bp = x_prompt.shape[0]
    mod = _ada(jnp.concatenate([c_prompt, c_sample], axis=0), ada_w, ada_b)
    out_p = _trunk(x_prompt, mod[:, :bp], None, w, True)
    st = (state_mlstm_C, state_mlstm_n, state_mlstm_m, state_mlstm_conv, state_rwkv_S, state_rwkv_shift,
          state_hgrn_S)
    out_s = _trunk(x_sample, mod[:, bp:], st, w, False)
    return (out_p[0], out_s[0]) + tuple(out_p[1:]) + tuple(out_s[1:])
```
